```python
import math
import jax, jax.numpy as jnp
from jax import lax
import numpy as np

D_MODEL = 1024
BATCH = 8
SEQ = 4096
DEPTH = 2
DEC_BATCH = 16
DEC_SEQ = 4096
PAST_LEN = 128

HEAD_DIM = 64
SCALE = HEAD_DIM ** -0.5
GRID_W = 64
QBLK = 128
EPS = 1e-6
A_Q_HEADS = 8
A_KV_HEADS = 2
A_GROUP = A_Q_HEADS // A_KV_HEADS
ROPE_THETA = 10000.0
B_HEADS = 4
C_HEADS = D_MODEL // HEAD_DIM
WIN_H = 8
WIN_W = 16
D_FF = 2816
N_EXPERTS = 8
TOP_K = 2
D_FF_EXPERT = 3584
N_EVEN = (DEPTH + 1) // 2
N_ODD = DEPTH // 2
A_Q_W = A_Q_HEADS * HEAD_DIM
A_KV_W = A_KV_HEADS * HEAD_DIM
B_QK_W = B_HEADS * 2 * HEAD_DIM
B_V_W = B_HEADS * 2 * HEAD_DIM
IN_W_EVEN = A_Q_W + 2 * A_KV_W + 2 * B_QK_W + B_V_W
MIX_W_EVEN = A_Q_W + B_V_W

kernel_name = "hybrid_bidir_encoder_gqa_diff_natten_moe"


def rms_norm(x, g):
    xf = x.astype(jnp.float32)
    y = xf * lax.rsqrt(jnp.mean(xf * xf, axis=-1, keepdims=True) + EPS)
    return (y * g.astype(jnp.float32)).astype(x.dtype)


def ada_modulation(c, w, b):
    mod = jax.nn.silu(c) @ w + b
    return jnp.split(mod[:, None, :], 6, axis=-1)


def modulate(h, shift, scale):
    return h * (1 + scale) + shift


def swiglu(h, w1, w3, w2):
    return (jax.nn.silu(h @ w1) * (h @ w3)) @ w2


def moe_swiglu(h, router_w, router_b, w1, w3, w2):
    logits = (h @ router_w).astype(jnp.float32) + router_b.astype(jnp.float32)
    top_val, top_idx = lax.top_k(logits, TOP_K)
    gates = jax.nn.softmax(top_val, axis=-1)
    weights = jnp.sum(jax.nn.one_hot(top_idx, N_EXPERTS, dtype=jnp.float32) * gates[..., None], axis=-2)
    weights = weights.astype(h.dtype)
    out = jnp.zeros_like(h)
    for e in range(N_EXPERTS):
        out = out + weights[..., e:e + 1] * swiglu(h, w1[e], w3[e], w2[e])
    return out


def axial_rope(x, seq_len):
    t = jnp.arange(seq_len)
    row = (t // GRID_W).astype(jnp.float32)
    col = (t % GRID_W).astype(jnp.float32)
    half = HEAD_DIM // 2
    quarter = half // 2
    inv = ROPE_THETA ** (-jnp.arange(quarter, dtype=jnp.float32) / quarter)

    def rot(xh, pos):
        ang = pos[:, None] * inv[None, :]
        cos = jnp.cos(ang)[None, :, None, :]
        sin = jnp.sin(ang)[None, :, None, :]
        x1 = xh[..., :quarter].astype(jnp.float32)
        x2 = xh[..., quarter:].astype(jnp.float32)
        return jnp.concatenate([x1 * cos - x2 * sin, x1 * sin + x2 * cos], axis=-1)

    out = jnp.concatenate([rot(x[..., :half], row), rot(x[..., half:], col)], axis=-1)
    return out.astype(x.dtype)


def alibi_slopes(n):
    return 2.0 ** (-8.0 * (jnp.arange(n, dtype=jnp.float32) + 1.0) / n)


def global_mix(qa, ka, va, qb, kb, vb, lam, lam_init, subln_g):
    B, S = qa.shape[0], qa.shape[1]
    nblk = S // QBLK
    qa_blocks = qa.reshape(B, nblk, QBLK, A_KV_HEADS, A_GROUP, HEAD_DIM).transpose(1, 0, 3, 4, 2, 5)
    qb_blocks = qb.reshape(B, nblk, QBLK, B_HEADS, 2, HEAD_DIM).transpose(1, 0, 3, 4, 2, 5)
    ka_t = ka.transpose(0, 2, 1, 3)
    va_t = va.transpose(0, 2, 1, 3)
    kb_t = kb.transpose(0, 2, 3, 1, 4)
    vb_t = vb.transpose(0, 2, 1, 3)
    slopes = alibi_slopes(B_HEADS)
    pos_k = jnp.arange(S)

    def step(args):
        i, qa_i, qb_i = args
        sa = jnp.einsum('bkgqd,bksd->bkgqs', qa_i, ka_t).astype(jnp.float32) * SCALE
        pa = jax.nn.softmax(sa, axis=-1)
        oa = jnp.einsum('bkgqs,bksd->bqkgd', pa.astype(va.dtype), va_t).reshape(B, QBLK, A_Q_W)
        pos_q = i * QBLK + jnp.arange(QBLK)
        dist = jnp.abs(pos_q[:, None] - pos_k[None, :]).astype(jnp.float32)
        bias = -slopes[:, None, None, None] * dist[None, None]
        sb = jnp.einsum('bhjqd,bhjsd->bhjqs', qb_i, kb_t).astype(jnp.float32) * SCALE + bias
        pb = jax.nn.softmax(sb, axis=-1)
        a = pb[:, :, 0] - lam * pb[:, :, 1]
        ob = jnp.einsum('bhqs,bhse->bqhe', a.astype(vb.dtype), vb_t)
        ob = (rms_norm(ob, subln_g) * (1.0 - lam_init)).reshape(B, QBLK, B_V_W)
        return jnp.concatenate([oa, ob.astype(oa.dtype)], axis=-1)

    out = lax.map(step, (jnp.arange(nblk), qa_blocks, qb_blocks))
    return out.transpose(1, 0, 2, 3).reshape(B, S, MIX_W_EVEN)


def neighbourhood_attention(q, k, v, rpb):
    B, S, H, d = q.shape
    rows = S // GRID_W
    kh = min(WIN_H, rows)
    kw = WIN_W
    qg = q.reshape(B, rows, GRID_W, H, d).transpose(1, 0, 2, 3, 4)
    kg = k.reshape(B, rows, GRID_W, H, d)
    vg = v.reshape(B, rows, GRID_W, H, d)
    col = jnp.arange(GRID_W)
    cstart = jnp.clip(col - kw // 2, 0, GRID_W - kw)
    col_valid = (col[None, :] >= cstart[:, None]) & (col[None, :] < cstart[:, None] + kw)
    dc_idx = jnp.clip(col[None, :] - col[:, None] + kw - 1, 0, 2 * kw - 2)
    rpb_cols = rpb[:, :, dc_idx]

    def row_step(args):
        r, q_row = args
        rstart = jnp.clip(r - kh // 2, 0, rows - kh)
        k_band = lax.dynamic_slice_in_dim(kg, rstart, kh, axis=1)
        v_band = lax.dynamic_slice_in_dim(vg, rstart, kh, axis=1)
        s = jnp.einsum('bqhd,bikhd->bhqik', q_row, k_band).astype(jnp.float32) * SCALE
        dr_idx = rstart + jnp.arange(kh) - r + (WIN_H - 1)
        bias = jnp.take(rpb_cols, dr_idx, axis=1).transpose(0, 2, 1, 3).astype(jnp.float32)
        s = jnp.where(col_valid[:, None, :], s + bias, -jnp.inf)
        p = jax.nn.softmax(s.reshape(B, H, GRID_W, kh * GRID_W), axis=-1).reshape(B, H, GRID_W, kh, GRID_W)
        return jnp.einsum('bhqik,bikhd->bqhd', p.astype(v.dtype), v_band)

    out = lax.map(row_step, (jnp.arange(rows), qg))
    return out.transpose(1, 0, 2, 3, 4).reshape(B, S, H * d)


def even_layer(x, c, l, ada_w, ada_b, norm_mix, norm_ffn, w_in, qn_a, kn_a, qn_b, kn_b,
               lam_q1, lam_k1, lam_q2, lam_k2, subln_g, w_out, ffn_w1, ffn_w3, ffn_w2):
    B, S, _ = x.shape
    sh1, sc1, g1, sh2, sc2, g2 = ada_modulation(c, ada_w, ada_b)
    h = modulate(rms_norm(x, norm_mix), sh1, sc1)
    proj = h @ w_in
    splits = [A_Q_W, A_Q_W + A_KV_W, A_Q_W + 2 * A_KV_W, A_Q_W + 2 * A_KV_W + B_QK_W,
              A_Q_W + 2 * A_KV_W + 2 * B_QK_W]
    qa, ka, va, qb, kb, vb = jnp.split(proj, splits, axis=-1)
    qa = axial_rope(rms_norm(qa.reshape(B, S, A_Q_HEADS, HEAD_DIM), qn_a), S)
    ka = axial_rope(rms_norm(ka.reshape(B, S, A_KV_HEADS, HEAD_DIM), kn_a), S)
    va = va.reshape(B, S, A_KV_HEADS, HEAD_DIM)
    qb = rms_norm(qb.reshape(B, S, B_HEADS, 2, HEAD_DIM), qn_b)
    kb = rms_norm(kb.reshape(B, S, B_HEADS, 2, HEAD_DIM), kn_b)
    vb = vb.reshape(B, S, B_HEADS, 2 * HEAD_DIM)
    lam_init = 0.8 - 0.6 * math.exp(-0.3 * l)
    lam = (jnp.exp(jnp.sum(lam_q1.astype(jnp.float32) * lam_k1.astype(jnp.float32)))
           - jnp.exp(jnp.sum(lam_q2.astype(jnp.float32) * lam_k2.astype(jnp.float32))) + lam_init)
    mix = global_mix(qa, ka, va, qb, kb, vb, lam, lam_init, subln_g)
    x = x + g1 * (mix.astype(x.dtype) @ w_out)
    h = modulate(rms_norm(x, norm_ffn), sh2, sc2)
    return x + g2 * swiglu(h, ffn_w1, ffn_w3, ffn_w2)


def odd_layer(x, c, ada_w, ada_b, norm_mix, norm_ffn, w_qkv, qn_c, kn_c, rpb, w_out,
              router_w, router_b, moe_w1, moe_w3, moe_w2):
    B, S, _ = x.shape
    sh1, sc1, g1, sh2, sc2, g2 = ada_modulation(c, ada_w, ada_b)
    h = modulate(rms_norm(x, norm_mix), sh1, sc1)
    q, k, v = jnp.split(h @ w_qkv, 3, axis=-1)
    q = rms_norm(q.reshape(B, S, C_HEADS, HEAD_DIM), qn_c)
    k = rms_norm(k.reshape(B, S, C_HEADS, HEAD_DIM), kn_c)
    v = v.reshape(B, S, C_HEADS, HEAD_DIM)
    mix = neighbourhood_attention(q, k, v, rpb)
    x = x + g1 * (mix.astype(x.dtype) @ w_out)
    h = modulate(rms_norm(x, norm_ffn), sh2, sc2)
    return x + g2 * moe_swiglu(h, router_w, router_b, moe_w1, moe_w3, moe_w2)


def trunk(x, c, even_params, odd_params):
    for l in range(DEPTH):
        if l % 2 == 0:
            p = [w[l // 2] for w in even_params]
            x = even_layer(x, c, l, *p)
        else:
            p = [w[l // 2] for w in odd_params]
            x = odd_layer(x, c, *p)
    return x


def setup_inputs(seed: int = 0) -> dict:
    key = jax.random.key(seed)
    ks = jax.random.split(key, 40)
    f32 = jnp.float32

    def dense(k, shape, fan_in):
        return jax.random.normal(k, shape, f32) * fan_in ** -0.5

    def gain(k, shape):
        return 1.0 + 0.02 * jax.random.normal(k, shape, f32)

    D = D_MODEL
    return {
        "x_prompt": jax.random.normal(ks[0], (BATCH, SEQ, D), f32),
        "x_sample": jax.random.normal(ks[1], (DEC_BATCH, DEC_SEQ, D), f32),
        "c_prompt": jax.random.normal(ks[2], (BATCH, D), f32),
        "c_sample": jax.random.normal(ks[3], (DEC_BATCH, D), f32),
        "ada_w_even": dense(ks[4], (N_EVEN, D, 6 * D), D),
        "ada_b_even": 0.02 * jax.random.normal(ks[5], (N_EVEN, 6 * D), f32),
        "norm_mix_even": gain(ks[6], (N_EVEN, D)),
        "norm_ffn_even": gain(ks[7], (N_EVEN, D)),
        "w_in_even": dense(ks[8], (N_EVEN, D, IN_W_EVEN), D),
        "qnorm_a": gain(ks[9], (N_EVEN, HEAD_DIM)),
        "knorm_a": gain(ks[10], (N_EVEN, HEAD_DIM)),
        "qnorm_b": gain(ks[11], (N_EVEN, HEAD_DIM)),
        "knorm_b": gain(ks[12], (N_EVEN, HEAD_DIM)),
        "lam_q1": 0.1 * jax.random.normal(ks[13], (N_EVEN, HEAD_DIM), f32),
        "lam_k1": 0.1 * jax.random.normal(ks[14], (N_EVEN, HEAD_DIM), f32),
        "lam_q2": 0.1 * jax.random.normal(ks[15], (N_EVEN, HEAD_DIM), f32),
        "lam_k2": 0.1 * jax.random.normal(ks[16], (N_EVEN, HEAD_DIM), f32),
        "subln_b": gain(ks[17], (N_EVEN, 2 * HEAD_DIM)),
        "w_out_even": dense(ks[18], (N_EVEN, MIX_W_EVEN, D), MIX_W_EVEN),
        "ffn_w1": dense(ks[19], (N_EVEN, D, D_FF), D),
        "ffn_w3": dense(ks[20], (N_EVEN, D, D_FF), D),
        "ffn_w2": dense(ks[21], (N_EVEN, D_FF, D), D_FF),
        "ada_w_odd": dense(ks[22], (N_ODD, D, 6 * D), D),
        "ada_b_odd": 0.02 * jax.random.normal(ks[23], (N_ODD, 6 * D), f32),
        "norm_mix_odd": gain(ks[24], (N_ODD, D)),
        "norm_ffn_odd": gain(ks[25], (N_ODD, D)),
        "w_qkv_odd": dense(ks[26], (N_ODD, D, 3 * C_HEADS * HEAD_DIM), D),
        "qnorm_c": gain(ks[27], (N_ODD, HEAD_DIM)),
        "knorm_c": gain(ks[28], (N_ODD, HEAD_DIM)),
        "rpb_c": 0.1 * jax.random.normal(ks[29], (N_ODD, C_HEADS, 2 * WIN_H - 1, 2 * WIN_W - 1), f32),
        "w_out_odd": dense(ks[30], (N_ODD, C_HEADS * HEAD_DIM, D), C_HEADS * HEAD_DIM),
        "router_w": dense(ks[31], (N_ODD, D, N_EXPERTS), D),
        "router_b": 0.01 * jax.random.normal(ks[32], (N_ODD, N_EXPERTS), f32),
        "moe_w1": dense(ks[33], (N_ODD, N_EXPERTS, D, D_FF_EXPERT), D),
        "moe_w3": dense(ks[34], (N_ODD, N_EXPERTS, D, D_FF_EXPERT), D),
        "moe_w2": dense(ks[35], (N_ODD, N_EXPERTS, D_FF_EXPERT, D), D_FF_EXPERT),
    }


def reference(x_prompt, x_sample, c_prompt, c_sample,
              ada_w_even, ada_b_even, norm_mix_even, norm_ffn_even, w_in_even,
              qnorm_a, knorm_a, qnorm_b, knorm_b, lam_q1, lam_k1, lam_q2, lam_k2, subln_b,
              w_out_even, ffn_w1, ffn_w3, ffn_w2,
              ada_w_odd, ada_b_odd, norm_mix_odd, norm_ffn_odd, w_qkv_odd, qnorm_c, knorm_c, rpb_c,
              w_out_odd, router_w, router_b, moe_w1, moe_w3, moe_w2):
    even_params = (ada_w_even, ada_b_even, norm_mix_even, norm_ffn_even, w_in_even,
                   qnorm_a, knorm_a, qnorm_b, knorm_b, lam_q1, lam_k1, lam_q2, lam_k2, subln_b,
                   w_out_even, ffn_w1, ffn_w3, ffn_w2)
    odd_params = (ada_w_odd, ada_b_odd, norm_mix_odd, norm_ffn_odd, w_qkv_odd, qnorm_c, knorm_c, rpb_c,
                  w_out_odd, router_w, router_b, moe_w1, moe_w3, moe_w2)
    y_prompt = trunk(x_prompt, c_prompt, even_params, odd_params)
    y_sample = trunk(x_sample, c_sample, even_params, odd_params)
    return (y_prompt, y_sample)
```

```python
import functools
import math

import numpy as np
import jax
import jax.numpy as jnp
from jax import lax
from jax.experimental import pallas as pl
from jax.experimental.pallas import tpu as pltpu

F32 = jnp.float32
BF16 = jnp.bfloat16

D_MODEL = 1024
HEAD_DIM = 64
LANES = 128
SCALE = HEAD_DIM ** -0.5
LOG2E = 1.4426950408889634
GRID_W = 64
EPS = 1e-6
ROPE_THETA = 10000.0
A_Q_HEADS = 8
A_KV_HEADS = 2
B_HEADS = 4
C_HEADS = 16
WIN_H = 8
WIN_W = 16
N_EXPERTS = 8
D_FF = 2816
D_FF_EXPERT = 3584
VMEM_LIMIT = 56 * 1024 * 1024

NAT_QROWS = 8
NAT_KROWS = 16
NAT_TQ = NAT_QROWS * GRID_W
NAT_TK = NAT_KROWS * GRID_W
NAT_KBLK = 256


def _cparams(sem):
    return pltpu.CompilerParams(dimension_semantics=sem, vmem_limit_bytes=VMEM_LIMIT)


def _norm_mod(x, g, shift, scale):
    ms = jnp.mean(x * x, axis=-1, keepdims=True)
    y = x * lax.rsqrt(ms + EPS) * g
    return y * (1.0 + scale) + shift


def _head_norm(x, gain, gmat):
    ms = jnp.dot((x * x).astype(BF16), gmat, preferred_element_type=F32)
    return x * lax.rsqrt(ms + EPS) * gain


def _rope(x, cos, sn, first_quarter):
    up = pltpu.roll(x, LANES - 16, 1)
    down = pltpu.roll(x, 16, 1)
    return x * cos + sn * jnp.where(first_quarter, -up, down)


def _sigmoid(a):
    return 1.0 / (1.0 + jnp.exp(-a))


def _ada_kernel(c_ref, w_ref, b_ref, o_ref):
    c = c_ref[...]
    s = c * _sigmoid(c)
    o_ref[...] = jnp.dot(s, w_ref[...], preferred_element_type=F32,
                         precision=lax.Precision.HIGHEST) + b_ref[...]


def _ada_modulation(c, w, b):
    nb, d = c.shape
    n = w.shape[1]
    tn = 512
    mod = pl.pallas_call(
        _ada_kernel,
        grid=(n // tn,),
        in_specs=[pl.BlockSpec((nb, d), lambda j: (0, 0)),
                  pl.BlockSpec((d, tn), lambda j: (0, j)),
                  pl.BlockSpec((1, tn), lambda j: (0, j))],
        out_specs=pl.BlockSpec((nb, tn), lambda j: (0, j)),
        out_shape=jax.ShapeDtypeStruct((nb, n), F32),
        compiler_params=_cparams(("arbitrary",)),
        name="ada_mod",
    )(c, w, b.reshape(1, n))
    return mod.reshape(nb, 6, 1, d)


def _mod_spec(k, tm, seq):
    return pl.BlockSpec((None, None, 1, D_MODEL), lambda i, *_: ((i * tm) // seq, k, 0, 0))


def _proj_kernel(x_ref, sh_ref, sc_ref, g_ref, w_ref, cos_ref, sn_ref, gains_ref, gmat_ref, *o_refs,
                 groups):
    h = _norm_mod(x_ref[...], g_ref[...], sh_ref[...], sc_ref[...])
    y = jnp.dot(h.astype(BF16), w_ref[...], preferred_element_type=F32)
    lane = lax.broadcasted_iota(jnp.int32, (1, LANES), 1)
    first_quarter = (lane % 32) < 16
    gmat = gmat_ref[...]
    off = 0
    for o_ref, (width, gain_row, rope, mult) in zip(o_refs, groups):
        if gain_row is None:
            o_ref[...] = y[:, off:off + width].astype(o_ref.dtype)
        else:
            gain = gains_ref[gain_row:gain_row + 1, :]
            for t in range(width // LANES):
                z = _head_norm(y[:, off + t * LANES: off + (t + 1) * LANES], gain, gmat)
                if rope:
                    z = _rope(z, cos_ref[...], sn_ref[...], first_quarter)
                if mult != 1.0:
                    z = z * mult
                o_ref[:, t * LANES:(t + 1) * LANES] = z.astype(o_ref.dtype)
        off += width


def _projection(x2, mod, norm_g, w, cos_t, sn_t, gains, gmat, groups, seq, tm=512):
    t = x2.shape[0]
    n = w.shape[1]
    nseq = seq // tm
    row = lambda i: (i, 0)
    const = lambda i: (0, 0)
    tab = lambda i: (i % nseq, 0)
    return pl.pallas_call(
        functools.partial(_proj_kernel, groups=groups),
        grid=(t // tm,),
        in_specs=[pl.BlockSpec((tm, D_MODEL), row),
                  _mod_spec(0, tm, seq), _mod_spec(1, tm, seq),
                  pl.BlockSpec((1, D_MODEL), const),
                  pl.BlockSpec((D_MODEL, n), const),
                  pl.BlockSpec((tm, LANES), tab), pl.BlockSpec((tm, LANES), tab),
                  pl.BlockSpec(gains.shape, const),
                  pl.BlockSpec((LANES, LANES), const)],
        out_specs=[pl.BlockSpec((tm, g[0]), row) for g in groups],
        out_shape=[jax.ShapeDtypeStruct((t, g[0]), BF16) for g in groups],
        compiler_params=_cparams(("arbitrary",)),
        name="norm_mod_proj",
    )(x2, mod, mod, norm_g.reshape(1, D_MODEL), w, cos_t, sn_t, gains, gmat)


def _flash_kernel(slope_ref, q_ref, k_ref, v_ref, lam_ref, subg_ref, o_ref,
                  q_sc, v_sc, m_sc, acc_sc, *, tq, tk, seq, nstack, alibi, lam_init):
    g = pl.program_id(1)
    i = pl.program_id(2)
    lane = lax.broadcasted_iota(jnp.int32, (1, LANES), 1)
    low_half = lane < HEAD_DIM

    @pl.when(i == 0)
    def _():
        v_sc[:, 0:LANES] = v_ref[...]
        v_sc[:, LANES:2 * LANES] = jnp.ones((seq, LANES), BF16)

    for u in range(nstack):
        src = q_ref[:, (u // 2) * LANES:(u // 2 + 1) * LANES]
        keep = low_half if u % 2 == 0 else jnp.logical_not(low_half)
        q_sc[u * tq:(u + 1) * tq, :] = jnp.where(keep, src, jnp.zeros_like(src))
    m_sc[...] = jnp.full(m_sc.shape, -jnp.inf, F32)
    acc_sc[...] = jnp.zeros(acc_sc.shape, F32)

    if alibi:
        rc = (lax.broadcasted_iota(jnp.int32, (tq, tk), 0)
              - lax.broadcasted_iota(jnp.int32, (tq, tk), 1)).astype(F32)
        neg_slope = -slope_ref[g]

    def body(c, carry):
        rows = pl.ds(pl.multiple_of(c * tk, tk), tk)
        s = lax.dot_general(q_sc[...], k_ref[rows, :], (((1,), (1,)), ((), ())),
                            preferred_element_type=F32)
        if alibi:
            base = (i * tq - c * tk).astype(F32)
            bias = neg_slope * jnp.abs(rc + base)
            s = s + jnp.concatenate([bias] * nstack, axis=0)
        m_old = m_sc[...]
        m_new = jnp.maximum(m_old, jnp.max(s, axis=-1, keepdims=True))
        alpha = jnp.exp2(m_old - m_new)
        p = jnp.exp2(s - m_new)
        acc_sc[...] = alpha * acc_sc[...] + jnp.dot(p.astype(BF16), v_sc[rows, :],
                                                     preferred_element_type=F32)
        m_sc[...] = m_new
        return carry

    lax.fori_loop(0, seq // tk, body, 0)

    acc = acc_sc[...]
    o = acc[:, 0:LANES] * (1.0 / acc[:, LANES:2 * LANES])
    if alibi:
        lp = lam_ref[...]
        l1 = jnp.sum(lp[0:1, :] * lp[1:2, :], axis=-1, keepdims=True)
        l2 = jnp.sum(lp[2:3, :] * lp[3:4, :], axis=-1, keepdims=True)
        lam = jnp.exp(l1) - jnp.exp(l2) + lam_init
        ob = o[0:tq, :] - lam * o[tq:2 * tq, :]
        ms = jnp.mean(ob * ob, axis=-1, keepdims=True)
        ob = ob * lax.rsqrt(ms + EPS) * subg_ref[...] * (1.0 - lam_init)
        o_ref[...] = ob.astype(o_ref.dtype)
    else:
        for pair in range(nstack // 2):
            lo = o[(2 * pair) * tq:(2 * pair + 1) * tq, :]
            hi = o[(2 * pair + 1) * tq:(2 * pair + 2) * tq, :]
            o_ref[:, pair * LANES:(pair + 1) * LANES] = jnp.where(low_half, lo, hi).astype(o_ref.dtype)


def _flash_attention(q, k, v, slopes, lam_pack, subg, *, nbatch, seq, ngroups, nstack, alibi, lam_init,
                     tq=256, tk=512):
    qw = (nstack // 2) * LANES
    kernel = functools.partial(_flash_kernel, tq=tq, tk=tk, seq=seq, nstack=nstack, alibi=alibi,
                               lam_init=lam_init)
    grid_spec = pltpu.PrefetchScalarGridSpec(
        num_scalar_prefetch=1,
        grid=(nbatch, ngroups, seq // tq),
        in_specs=[pl.BlockSpec((None, tq, qw), lambda b, g, i, s: (b, i, g)),
                  pl.BlockSpec((None, seq, LANES), lambda b, g, i, s: (b, 0, g)),
                  pl.BlockSpec((None, seq, LANES), lambda b, g, i, s: (b, 0, g)),
                  pl.BlockSpec(lam_pack.shape, lambda b, g, i, s: (0, 0)),
                  pl.BlockSpec(subg.shape, lambda b, g, i, s: (0, 0))],
        out_specs=pl.BlockSpec((None, tq, qw), lambda b, g, i, s: (b, i, g)),
        scratch_shapes=[pltpu.VMEM((nstack * tq, LANES), BF16),
                        pltpu.VMEM((seq, 2 * LANES), BF16),
                        pltpu.VMEM((nstack * tq, 1), F32),
                        pltpu.VMEM((nstack * tq, 2 * LANES), F32)])
    return pl.pallas_call(
        kernel,
        grid_spec=grid_spec,
        out_shape=jax.ShapeDtypeStruct((nbatch, seq, ngroups * qw), BF16),
        compiler_params=_cparams(("arbitrary", "arbitrary", "arbitrary")),
        name="flash_alibi" if alibi else "flash_gqa",
    )(slopes, q, k, v, lam_pack, subg)


def _natten_kernel(q_ref, k0, k1, k2, k3, v0, v1, v2, v3, bias_ref, o_ref):
    lane = lax.broadcasted_iota(jnp.int32, (1, LANES), 1)
    low_half = lane < HEAD_DIM
    q = q_ref[...]
    zero = jnp.zeros_like(q)
    q2 = jnp.concatenate([jnp.where(low_half, q, zero), jnp.where(low_half, zero, q)], axis=0)
    s_parts = []
    for c, k_ref in enumerate((k0, k1, k2, k3)):
        s = lax.dot_general(q2, k_ref[...], (((1,), (1,)), ((), ())), preferred_element_type=F32)
        b = jnp.concatenate([bias_ref[0, :, c * NAT_KBLK:(c + 1) * NAT_KBLK],
                             bias_ref[1, :, c * NAT_KBLK:(c + 1) * NAT_KBLK]], axis=0)
        s_parts.append(s + b)
    m = functools.reduce(jnp.maximum, [jnp.max(s, axis=-1, keepdims=True) for s in s_parts])
    ones = jnp.ones((NAT_KBLK, LANES), BF16)
    acc = None
    for s, v_ref in zip(s_parts, (v0, v1, v2, v3)):
        p = jnp.exp2(s - m).astype(BF16)
        vext = jnp.concatenate([v_ref[...], ones], axis=1)
        d = jnp.dot(p, vext, preferred_element_type=F32)
        acc = d if acc is None else acc + d
    o = acc[:, 0:LANES] * (1.0 / acc[:, LANES:2 * LANES])
    o_ref[...] = jnp.where(low_half, o[0:NAT_TQ, :], o[NAT_TQ:, :]).astype(o_ref.dtype)


def _natten(q, k, v, bias_tab, *, nbatch, seq):
    ntiles = seq // NAT_TQ
    npairs = C_HEADS // 2
    kblocks = seq // NAT_KBLK

    def win(i):
        return jnp.clip(2 * i - 1, 0, kblocks - NAT_TK // NAT_KBLK)

    def cls(i):
        return jnp.where(i == 0, 0, jnp.where(i == ntiles - 1, 2, 1))

    def kv_spec(j):
        return pl.BlockSpec((None, NAT_KBLK, LANES), lambda i, p, b: (b, win(i) + j, p))

    return pl.pallas_call(
        _natten_kernel,
        grid=(ntiles, npairs, nbatch),
        in_specs=[pl.BlockSpec((None, NAT_TQ, LANES), lambda i, p, b: (b, i, p))]
                 + [kv_spec(j) for j in range(4)] + [kv_spec(j) for j in range(4)]
                 + [pl.BlockSpec((None, None, 2, NAT_TQ, NAT_TK), lambda i, p, b: (cls(i), p, 0, 0, 0))],
        out_specs=pl.BlockSpec((None, NAT_TQ, LANES), lambda i, p, b: (b, i, p)),
        out_shape=jax.ShapeDtypeStruct((nbatch, seq, C_HEADS * HEAD_DIM), BF16),
        compiler_params=_cparams(("arbitrary", "arbitrary", "arbitrary")),
        name="natten",
    )(q, k, k, k, k, v, v, v, v, bias_tab)


def _natten_bias_table(rpb, seq):
    rows = seq // GRID_W
    ntiles = rows // NAT_QROWS
    col = jnp.arange(GRID_W)
    cstart = jnp.clip(col - WIN_W // 2, 0, GRID_W - WIN_W)
    col_valid = (col[None, :] >= cstart[:, None]) & (col[None, :] < cstart[:, None] + WIN_W)
    dc_idx = jnp.clip(col[None, :] - col[:, None] + WIN_W - 1, 0, 2 * WIN_W - 2)
    rpb_cols = rpb[:, :, dc_idx]
    tabs = []
    for tile in (0, 1, ntiles - 1):
        r = tile * NAT_QROWS + jnp.arange(NAT_QROWS)
        w0 = int(np.clip(tile * NAT_QROWS - WIN_H // 2, 0, rows - NAT_KROWS))
        kr = w0 + jnp.arange(NAT_KROWS)
        rstart = jnp.clip(r - WIN_H // 2, 0, rows - WIN_H)
        row_valid = (kr[None, :] >= rstart[:, None]) & (kr[None, :] < rstart[:, None] + WIN_H)
        dr_idx = jnp.clip(kr[None, :] - r[:, None] + WIN_H - 1, 0, 2 * WIN_H - 2)
        b = rpb_cols[:, dr_idx]
        valid = row_valid[:, :, None, None] & col_valid[None, None, :, :]
        b = jnp.where(valid[None], b * LOG2E, -jnp.inf)
        b = b.transpose(0, 1, 3, 2, 4).reshape(C_HEADS, NAT_TQ, NAT_TK)
        tabs.append(b)
    return jnp.stack(tabs).reshape(3, C_HEADS // 2, 2, NAT_TQ, NAT_TK)


def _outproj_kernel(x_ref, gate_ref, *rest, nmix):
    mix_refs, w_refs, o_ref = rest[:nmix], rest[nmix:2 * nmix], rest[2 * nmix]
    y = None
    for m_ref, w_ref in zip(mix_refs, w_refs):
        d = jnp.dot(m_ref[...], w_ref[...], preferred_element_type=F32)
        y = d if y is None else y + d
    o_ref[...] = x_ref[...] + gate_ref[...] * y


def _outproj(x2, mod, mixes, ws, seq, tm=512):
    t = x2.shape[0]
    row = lambda i: (i, 0)
    const = lambda i: (0, 0)
    return pl.pallas_call(
        functools.partial(_outproj_kernel, nmix=len(mixes)),
        grid=(t // tm,),
        in_specs=[pl.BlockSpec((tm, D_MODEL), row), _mod_spec(2, tm, seq)]
                 + [pl.BlockSpec((tm, m.shape[1]), row) for m in mixes]
                 + [pl.BlockSpec(w.shape, const) for w in ws],
        out_specs=pl.BlockSpec((tm, D_MODEL), row),
        out_shape=jax.ShapeDtypeStruct((t, D_MODEL), F32),
        compiler_params=_cparams(("arbitrary",)),
        name="outproj_residual",
    )(x2, mod, *mixes, *ws)


def _ffn_kernel(x_ref, sh_ref, sc_ref, gate_ref, g_ref, w1_ref, w3_ref, w2_ref, o_ref, h_sc, acc_sc, *, nf):
    f = pl.program_id(1)

    @pl.when(f == 0)
    def _():
        h_sc[...] = _norm_mod(x_ref[...], g_ref[...], sh_ref[...], sc_ref[...]).astype(BF16)

    h = h_sc[...]
    a = jnp.dot(h, w1_ref[...], preferred_element_type=F32)
    b = jnp.dot(h, w3_ref[...], preferred_element_type=F32)
    y = jnp.dot((a * _sigmoid(a) * b).astype(BF16), w2_ref[...], preferred_element_type=F32)

    @pl.when(f == 0)
    def _():
        acc_sc[...] = y

    @pl.when(f > 0)
    def _():
        acc_sc[...] += y

    @pl.when(f == nf - 1)
    def _():
        o_ref[...] = x_ref[...] + gate_ref[...] * acc_sc[...]


def _ffn(x2, mod, norm_g, w1, w3, w2, seq, tm=512, tf=1408):
    t = x2.shape[0]
    nf = w1.shape[1] // tf
    row = lambda i, f: (i, 0)
    return pl.pallas_call(
        functools.partial(_ffn_kernel, nf=nf),
        grid=(t // tm, nf),
        in_specs=[pl.BlockSpec((tm, D_MODEL), row),
                  _mod_spec(3, tm, seq), _mod_spec(4, tm, seq), _mod_spec(5, tm, seq),
                  pl.BlockSpec((1, D_MODEL), lambda i, f: (0, 0)),
                  pl.BlockSpec((D_MODEL, tf), lambda i, f: (0, f)),
                  pl.BlockSpec((D_MODEL, tf), lambda i, f: (0, f)),
                  pl.BlockSpec((tf, D_MODEL), lambda i, f: (f, 0))],
        out_specs=pl.BlockSpec((tm, D_MODEL), row),
        out_shape=jax.ShapeDtypeStruct((t, D_MODEL), F32),
        scratch_shapes=[pltpu.VMEM((tm, D_MODEL), BF16), pltpu.VMEM((tm, D_MODEL), F32)],
        compiler_params=_cparams(("arbitrary", "arbitrary")),
        name="ffn_swiglu",
    )(x2, mod, mod, mod, norm_g.reshape(1, D_MODEL), w1, w3, w2)


def _router_kernel(x_ref, sh_ref, sc_ref, g_ref, rw_ref, rb_ref, h_ref, meta_ref):
    h = _norm_mod(x_ref[...], g_ref[...], sh_ref[...], sc_ref[...])
    h_ref[...] = h.astype(BF16)
    lane = lax.broadcasted_iota(jnp.int32, (1, LANES), 1).astype(F32)
    logits = jnp.dot(h, rw_ref[...], preferred_element_type=F32,
                     precision=lax.Precision.HIGHEST) + rb_ref[...]
    logits = jnp.where(lane < N_EXPERTS, logits, -jnp.inf)
    m1 = jnp.max(logits, axis=-1, keepdims=True)
    i1 = jnp.min(jnp.where(logits == m1, lane, float(LANES)), axis=-1, keepdims=True)
    rest = jnp.where(lane == i1, -jnp.inf, logits)
    m2 = jnp.max(rest, axis=-1, keepdims=True)
    i2 = jnp.min(jnp.where(rest == m2, lane, float(LANES)), axis=-1, keepdims=True)
    e = jnp.exp(m2 - m1)
    g1 = 1.0 / (1.0 + e)
    g2 = e * g1
    meta = jnp.where(lane == 0, i1, jnp.where(lane == 1, i2, jnp.where(lane == 2, g1,
                                                                       jnp.where(lane == 3, g2, 0.0))))
    meta_ref[...] = meta


def _router(x2, mod, norm_g, rw_pad, rb_pad, seq, tm=512):
    t = x2.shape[0]
    row = lambda i: (i, 0)
    const = lambda i: (0, 0)
    return pl.pallas_call(
        _router_kernel,
        grid=(t // tm,),
        in_specs=[pl.BlockSpec((tm, D_MODEL), row), _mod_spec(3, tm, seq), _mod_spec(4, tm, seq),
                  pl.BlockSpec((1, D_MODEL), const),
                  pl.BlockSpec((D_MODEL, LANES), const), pl.BlockSpec((1, LANES), const)],
        out_specs=[pl.BlockSpec((tm, D_MODEL), row), pl.BlockSpec((tm, LANES), row)],
        out_shape=[jax.ShapeDtypeStruct((t, D_MODEL), BF16), jax.ShapeDtypeStruct((t, LANES), F32)],
        compiler_params=_cparams(("arbitrary",)),
        name="moe_router",
    )(x2, mod, mod, norm_g.reshape(1, D_MODEL), rw_pad, rb_pad)


def _moe_kernel(te_ref, nt_ref, x_ref, w1_ref, w3_ref, w2_ref, o_ref, acc_sc, *, nf):
    i = pl.program_id(0)
    f = pl.program_id(1)
    active = i < nt_ref[0]

    @pl.when(active)
    def _():
        x = x_ref[...]
        a = jnp.dot(x, w1_ref[...], preferred_element_type=F32)
        b = jnp.dot(x, w3_ref[...], preferred_element_type=F32)
        y = jnp.dot((a * _sigmoid(a) * b).astype(BF16), w2_ref[...], preferred_element_type=F32)

        @pl.when(f == 0)
        def _():
            acc_sc[...] = y

        @pl.when(f > 0)
        def _():
            acc_sc[...] += y

        @pl.when(f == nf - 1)
        def _():
            o_ref[...] = acc_sc[...].astype(o_ref.dtype)

    @pl.when(jnp.logical_and(jnp.logical_not(active), f == nf - 1))
    def _():
        o_ref[...] = jnp.zeros(o_ref.shape, o_ref.dtype)


def _moe_grouped(xs, tile_expert, num_tiles, w1, w3, w2, tg, tf=512):
    p = xs.shape[0]
    nf = w1.shape[2] // tf
    grid_spec = pltpu.PrefetchScalarGridSpec(
        num_scalar_prefetch=2,
        grid=(p // tg, nf),
        in_specs=[pl.BlockSpec((tg, D_MODEL), lambda i, f, te, nt: (i, 0)),
                  pl.BlockSpec((None, D_MODEL, tf), lambda i, f, te, nt: (te[i], 0, f)),
                  pl.BlockSpec((None, D_MODEL, tf), lambda i, f, te, nt: (te[i], 0, f)),
                  pl.BlockSpec((None, tf, D_MODEL), lambda i, f, te, nt: (te[i], f, 0))],
        out_specs=pl.BlockSpec((tg, D_MODEL), lambda i, f, te, nt: (i, 0)),
        scratch_shapes=[pltpu.VMEM((tg, D_MODEL), F32)])
    return pl.pallas_call(
        functools.partial(_moe_kernel, nf=nf),
        grid_spec=grid_spec,
        out_shape=jax.ShapeDtypeStruct((p, D_MODEL), BF16),
        compiler_params=_cparams(("arbitrary", "arbitrary")),
        name="moe_grouped",
    )(tile_expert, num_tiles, xs, w1, w3, w2)


def _combine_kernel(x_ref, gate_ref, meta_ref, y1_ref, y2_ref, o_ref):
    meta = meta_ref[...]
    moe = meta[:, 2:3] * y1_ref[...].astype(F32) + meta[:, 3:4] * y2_ref[...].astype(F32)
    o_ref[...] = x_ref[...] + gate_ref[...] * moe


def _combine(x2, mod, meta, y1, y2, seq, tm=512):
    t = x2.shape[0]
    row = lambda i: (i, 0)
    return pl.pallas_call(
        _combine_kernel,
        grid=(t // tm,),
        in_specs=[pl.BlockSpec((tm, D_MODEL), row), _mod_spec(5, tm, seq),
                  pl.BlockSpec((tm, LANES), row),
                  pl.BlockSpec((tm, D_MODEL), row), pl.BlockSpec((tm, D_MODEL), row)],
        out_specs=pl.BlockSpec((tm, D_MODEL), row),
        out_shape=jax.ShapeDtypeStruct((t, D_MODEL), F32),
        compiler_params=_cparams(("arbitrary",)),
        name="moe_combine",
    )(x2, mod, meta, y1, y2)


def _moe(x2, mod, norm_g, rw_pad, rb_pad, w1, w3, w2, seq, tg=1024):
    t = x2.shape[0]
    h, meta = _router(x2, mod, norm_g, rw_pad, rb_pad, seq)
    e_flat = meta[:, 0:2].astype(jnp.int32).reshape(-1)
    onehot = (e_flat[:, None] == jnp.arange(N_EXPERTS)[None, :]).astype(jnp.int32)
    csum = jnp.cumsum(onehot, axis=0)
    counts = csum[-1]
    rank = jnp.take_along_axis(csum, e_flat[:, None], axis=1)[:, 0] - 1
    padded = ((counts + tg - 1) // tg) * tg
    pend = jnp.cumsum(padded)
    pos = (pend - padded)[e_flat] + rank
    p_rows = 2 * t + N_EXPERTS * tg
    row_token = jnp.zeros((p_rows,), jnp.int32).at[pos].set(jnp.arange(2 * t, dtype=jnp.int32) // 2)
    tile_start = jnp.arange(p_rows // tg, dtype=jnp.int32) * tg
    tile_expert = jnp.minimum(jnp.searchsorted(pend, tile_start, side="right"), N_EXPERTS - 1).astype(jnp.int32)
    num_tiles = (pend[-1] // tg).astype(jnp.int32).reshape(1)
    xs = jnp.take(h, row_token, axis=0)
    ys = _moe_grouped(xs, tile_expert, num_tiles, w1, w3, w2, tg)
    y1 = jnp.take(ys, pos[0::2], axis=0)
    y2 = jnp.take(ys, pos[1::2], axis=0)
    return _combine(x2, mod, meta, y1, y2, seq)


def _rope_tables(seq):
    t = np.arange(seq)
    lane = np.arange(LANES)
    d = lane % HEAD_DIM
    pos = np.where((d // 32)[None, :] == 0, (t // GRID_W)[:, None], (t % GRID_W)[:, None]).astype(np.float32)
    inv = (ROPE_THETA ** (-np.arange(16, dtype=np.float32) / 16)).astype(np.float32)
    ang = pos * inv[(d % 16)][None, :]
    return jnp.asarray(np.cos(ang), F32), jnp.asarray(np.sin(ang), F32)


def _pair_gain(g):
    return jnp.concatenate([g, g]).astype(F32)


def _prepare(p, seq):
    even, odd = {}, {}
    w_in = p["w_in_even"][0]
    qa, ka, va, qb, kb, vb = jnp.split(w_in, [512, 640, 768, 1280, 1792], axis=1)
    dup = lambda w: jnp.concatenate([w[:, 0:64], w[:, 0:64], w[:, 64:128], w[:, 64:128]], axis=1)
    even["w_in"] = jnp.concatenate([qa, qb, dup(ka), dup(va), kb, vb], axis=1).astype(BF16)
    gains = jnp.zeros((8, LANES), F32)
    gains = gains.at[0].set(_pair_gain(p["qnorm_a"][0])).at[1].set(_pair_gain(p["knorm_a"][0]))
    gains = gains.at[2].set(_pair_gain(p["qnorm_b"][0])).at[3].set(_pair_gain(p["knorm_b"][0]))
    even["gains"] = gains
    qscale = SCALE * LOG2E
    even["groups"] = ((512, 0, True, qscale), (512, 2, False, qscale), (256, 1, True, 1.0),
                      (256, None, False, 1.0), (512, 3, False, 1.0), (512, None, False, 1.0))
    lam = jnp.zeros((8, LANES), F32)
    for r, name in enumerate(("lam_q1", "lam_k1", "lam_q2", "lam_k2")):
        lam = lam.at[r, 0:HEAD_DIM].set(p[name][0])
    even["lam"] = lam
    even["subg"] = p["subln_b"][0].reshape(1, LANES).astype(F32)
    wo = p["w_out_even"][0].astype(BF16)
    even["wo"] = (wo[0:512], wo[512:1024])
    even["slopes"] = jnp.asarray(LOG2E * 2.0 ** (-8.0 * (np.arange(B_HEADS) + 1.0) / B_HEADS), F32)
    for name in ("ffn_w1", "ffn_w3", "ffn_w2"):
        even[name] = p[name][0].astype(BF16)
    for name in ("ada_w", "ada_b", "norm_mix", "norm_ffn"):
        even[name] = p[name + "_even"][0]
        odd[name] = p[name + "_odd"][0]

    odd["w_qkv"] = p["w_qkv_odd"][0].astype(BF16)
    gains = jnp.zeros((8, LANES), F32)
    odd["gains"] = gains.at[0].set(_pair_gain(p["qnorm_c"][0])).at[1].set(_pair_gain(p["knorm_c"][0]))
    odd["groups"] = ((1024, 0, False, qscale), (1024, 1, False, 1.0), (1024, None, False, 1.0))
    odd["bias_tab"] = _natten_bias_table(p["rpb_c"][0], seq)
    odd["wo"] = (p["w_out_odd"][0].astype(BF16),)
    odd["rw"] = jnp.zeros((D_MODEL, LANES), F32).at[:, 0:N_EXPERTS].set(p["router_w"][0])
    odd["rb"] = jnp.zeros((1, LANES), F32).at[0, 0:N_EXPERTS].set(p["router_b"][0])
    for name in ("moe_w1", "moe_w3", "moe_w2"):
        odd[name] = p[name][0].astype(BF16)
    cos_t, sn_t = _rope_tables(seq)
    gmat = jnp.asarray(np.kron(np.eye(2), np.full((HEAD_DIM, HEAD_DIM), 1.0 / HEAD_DIM)), BF16)
    return even, odd, (cos_t, sn_t, gmat)


def _even_layer(x2, c, ev, shared, nbatch, seq):
    cos_t, sn_t, gmat = shared
    mod = _ada_modulation(c, ev["ada_w"], ev["ada_b"])
    qa, qb, ka, va, kb, vb = _projection(x2, mod, ev["norm_mix"], ev["w_in"], cos_t, sn_t, ev["gains"], gmat,
                                         ev["groups"], seq)
    shp = lambda a: a.reshape(nbatch, seq, a.shape[1])
    lam_init = 0.8 - 0.6 * math.exp(-0.3 * 0)
    mix_a = _flash_attention(shp(qa), shp(ka), shp(va), ev["slopes"], ev["lam"], ev["subg"], nbatch=nbatch, seq=seq,
                             ngroups=A_KV_HEADS, nstack=4, alibi=False, lam_init=lam_init)
    mix_b = _flash_attention(shp(qb), shp(kb), shp(vb), ev["slopes"], ev["lam"], ev["subg"], nbatch=nbatch, seq=seq,
                             ngroups=B_HEADS, nstack=2, alibi=True, lam_init=lam_init)
    t = nbatch * seq
    x2 = _outproj(x2, mod, (mix_a.reshape(t, -1), mix_b.reshape(t, -1)), ev["wo"], seq)
    return _ffn(x2, mod, ev["norm_ffn"], ev["ffn_w1"], ev["ffn_w3"], ev["ffn_w2"], seq)


def _odd_layer(x2, c, od, shared, nbatch, seq):
    cos_t, sn_t, gmat = shared
    mod = _ada_modulation(c, od["ada_w"], od["ada_b"])
    q, k, v = _projection(x2, mod, od["norm_mix"], od["w_qkv"], cos_t, sn_t, od["gains"], gmat, od["groups"], seq)
    shp = lambda a: a.reshape(nbatch, seq, a.shape[1])
    mix = _natten(shp(q), shp(k), shp(v), od["bias_tab"], nbatch=nbatch, seq=seq)
    x2 = _outproj(x2, mod, (mix.reshape(nbatch * seq, -1),), od["wo"], seq)
    return _moe(x2, mod, od["norm_ffn"], od["rw"], od["rb"], od["moe_w1"], od["moe_w3"], od["moe_w2"], seq)


def _trunk(x, c, ev, od, shared):
    nbatch, seq, d = x.shape
    x2 = x.reshape(nbatch * seq, d)
    x2 = _even_layer(x2, c, ev, shared, nbatch, seq)
    x2 = _odd_layer(x2, c, od, shared, nbatch, seq)
    return x2.reshape(nbatch, seq, d)


def kernel(x_prompt, x_sample, c_prompt, c_sample, ada_w_even, ada_b_even, norm_mix_even, norm_ffn_even, w_in_even, qnorm_a, knorm_a, qnorm_b, knorm_b, lam_q1, lam_k1, lam_q2, lam_k2, subln_b, w_out_even, ffn_w1, ffn_w3, ffn_w2, ada_w_odd, ada_b_odd, norm_mix_odd, norm_ffn_odd, w_qkv_odd, qnorm_c, knorm_c, rpb_c, w_out_odd, router_w, router_b, moe_w1, moe_w3, moe_w2):
    params = dict(ada_w_even=ada_w_even, ada_b_even=ada_b_even, norm_mix_even=norm_mix_even,
                  norm_ffn_even=norm_ffn_even, w_in_even=w_in_even, qnorm_a=qnorm_a, knorm_a=knorm_a,
                  qnorm_b=qnorm_b, knorm_b=knorm_b, lam_q1=lam_q1, lam_k1=lam_k1, lam_q2=lam_q2, lam_k2=lam_k2,
                  subln_b=subln_b, w_out_even=w_out_even, ffn_w1=ffn_w1, ffn_w3=ffn_w3, ffn_w2=ffn_w2,
                  ada_w_odd=ada_w_odd, ada_b_odd=ada_b_odd, norm_mix_odd=norm_mix_odd, norm_ffn_odd=norm_ffn_odd,
                  w_qkv_odd=w_qkv_odd, qnorm_c=qnorm_c, knorm_c=knorm_c, rpb_c=rpb_c, w_out_odd=w_out_odd,
                  router_w=router_w, router_b=router_b, moe_w1=moe_w1, moe_w3=moe_w3, moe_w2=moe_w2)
    seq = x_prompt.shape[1]
    ev, od, shared = _prepare(params, seq)
    y_prompt = _trunk(x_prompt, c_prompt, ev, od, shared)
    y_sample = _trunk(x_sample, c_sample, ev, od, shared)
    return (y_prompt, y_sample)
```

```python
import functools
import math

import numpy as np
import jax
import jax.numpy as jnp
from jax import lax
from jax.experimental import pallas as pl
from jax.experimental.pallas import tpu as pltpu

F32 = jnp.float32
BF16 = jnp.bfloat16

D_MODEL = 1024
HEAD_DIM = 64
LANES = 128
SCALE = HEAD_DIM ** -0.5
LOG2E = 1.4426950408889634
GRID_W = 64
EPS = 1e-6
ROPE_THETA = 10000.0
A_Q_HEADS = 8
A_KV_HEADS = 2
B_HEADS = 4
C_HEADS = 16
WIN_H = 8
WIN_W = 16
N_EXPERTS = 8
D_FF = 2816
D_FF_EXPERT = 3584
VMEM_LIMIT = 56 * 1024 * 1024

NAT_QROWS = 8
NAT_KROWS = 16
NAT_TQ = NAT_QROWS * GRID_W
NAT_TK = NAT_KROWS * GRID_W
NAT_KBLK = 256


def _cparams(sem):
    return pltpu.CompilerParams(dimension_semantics=sem, vmem_limit_bytes=VMEM_LIMIT)


def _norm_mod(x, g, shift, scale):
    ms = jnp.mean(x * x, axis=-1, keepdims=True)
    y = x * lax.rsqrt(ms + EPS) * g
    return y * (1.0 + scale) + shift


def _head_norm(x, gain, gmat):
    ms = jnp.dot((x * x).astype(BF16), gmat, preferred_element_type=F32)
    return x * lax.rsqrt(ms + EPS) * gain


def _rope(x, cos, sn, first_quarter):
    up = pltpu.roll(x, LANES - 16, 1)
    down = pltpu.roll(x, 16, 1)
    return x * cos + sn * jnp.where(first_quarter, -up, down)


def _sigmoid(a):
    return 1.0 / (1.0 + jnp.exp(-a))


def _ada_kernel(c_ref, w_ref, b_ref, o_ref):
    c = c_ref[...]
    s = c * _sigmoid(c)
    o_ref[...] = jnp.dot(s, w_ref[...], preferred_element_type=F32,
                         precision=lax.Precision.HIGHEST) + b_ref[...]


def _ada_modulation(c, w, b):
    nb, d = c.shape
    n = w.shape[1]
    tn = 512
    mod = pl.pallas_call(
        _ada_kernel,
        grid=(n // tn,),
        in_specs=[pl.BlockSpec((nb, d), lambda j: (0, 0)),
                  pl.BlockSpec((d, tn), lambda j: (0, j)),
                  pl.BlockSpec((1, tn), lambda j: (0, j))],
        out_specs=pl.BlockSpec((nb, tn), lambda j: (0, j)),
        out_shape=jax.ShapeDtypeStruct((nb, n), F32),
        compiler_params=_cparams(("arbitrary",)),
        name="ada_mod",
    )(c, w, b.reshape(1, n))
    return mod.reshape(nb, 6, 1, d)


def _mod_spec(k, tm, seq):
    return pl.BlockSpec((None, None, 1, D_MODEL), lambda i, *_: ((i * tm) // seq, k, 0, 0))


def _proj_kernel(x_ref, sh_ref, sc_ref, g_ref, w_ref, cos_ref, sn_ref, gains_ref, gmat_ref, *o_refs,
                 groups):
    h = _norm_mod(x_ref[...], g_ref[...], sh_ref[...], sc_ref[...])
    y = jnp.dot(h.astype(BF16), w_ref[...], preferred_element_type=F32)
    lane = lax.broadcasted_iota(jnp.int32, (1, LANES), 1)
    first_quarter = (lane % 32) < 16
    gmat = gmat_ref[...]
    off = 0
    for o_ref, (width, gain_row, rope, mult) in zip(o_refs, groups):
        if gain_row is None:
            o_ref[...] = y[:, off:off + width].astype(o_ref.dtype)
        else:
            gain = gains_ref[gain_row:gain_row + 1, :]
            for t in range(width // LANES):
                z = _head_norm(y[:, off + t * LANES: off + (t + 1) * LANES], gain, gmat)
                if rope:
                    z = _rope(z, cos_ref[...], sn_ref[...], first_quarter)
                if mult != 1.0:
                    z = z * mult
                o_ref[:, t * LANES:(t + 1) * LANES] = z.astype(o_ref.dtype)
        off += width


def _projection(x2, mod, norm_g, w, cos_t, sn_t, gains, gmat, groups, seq, tm=512):
    t = x2.shape[0]
    n = w.shape[1]
    nseq = seq // tm
    row = lambda i: (i, 0)
    const = lambda i: (0, 0)
    tab = lambda i: (i % nseq, 0)
    return pl.pallas_call(
        functools.partial(_proj_kernel, groups=groups),
        grid=(t // tm,),
        in_specs=[pl.BlockSpec((tm, D_MODEL), row),
                  _mod_spec(0, tm, seq), _mod_spec(1, tm, seq),
                  pl.BlockSpec((1, D_MODEL), const),
                  pl.BlockSpec((D_MODEL, n), const),
                  pl.BlockSpec((tm, LANES), tab), pl.BlockSpec((tm, LANES), tab),
                  pl.BlockSpec(gains.shape, const),
                  pl.BlockSpec((LANES, LANES), const)],
        out_specs=[pl.BlockSpec((tm, g[0]), row) for g in groups],
        out_shape=[jax.ShapeDtypeStruct((t, g[0]), BF16) for g in groups],
        compiler_params=_cparams(("arbitrary",)),
        name="norm_mod_proj",
    )(x2, mod, mod, norm_g.reshape(1, D_MODEL), w, cos_t, sn_t, gains, gmat)


def _flash_kernel(slope_ref, q_ref, k_ref, v_ref, lam_ref, subg_ref, o_ref,
                  q_sc, v_sc, m_sc, acc_sc, s0_sc, s1_sc, p0_sc, p1_sc, a0_sc, a1_sc,
                  *, tq, tk, seq, nstack, alibi, lam_init):
    g = pl.program_id(1)
    lane = lax.broadcasted_iota(jnp.int32, (1, LANES), 1)
    low_half = lane < HEAD_DIM
    s_bufs, p_bufs, a_bufs = (s0_sc, s1_sc), (p0_sc, p1_sc), (a0_sc, a1_sc)
    nchunks = seq // tk
    ntiles = seq // tq
    nrows = nstack * tq

    v_sc[:, 0:LANES] = v_ref[...]
    v_sc[:, LANES:2 * LANES] = jnp.ones((seq, LANES), BF16)

    if alibi:
        rc = (lax.broadcasted_iota(jnp.int32, (tq, tk), 0)
              - lax.broadcasted_iota(jnp.int32, (tq, tk), 1)).astype(F32)
        neg_slope = -slope_ref[g]
        lp = lam_ref[...]
        l1 = jnp.sum(lp[0:1, :] * lp[1:2, :], axis=-1, keepdims=True)
        l2 = jnp.sum(lp[2:3, :] * lp[3:4, :], axis=-1, keepdims=True)
        lam = jnp.exp(l1) - jnp.exp(l2) + lam_init

    def tile_rows(t):
        return pl.ds(t * tq, tq) if isinstance(t, int) else pl.ds(pl.multiple_of(t * tq, tq), tq)

    def chunk_rows(c):
        return pl.ds(c * tk, tk) if isinstance(c, int) else pl.ds(pl.multiple_of(c * tk, tk), tk)

    def load_queries(t, slot):
        for u in range(nstack):
            src = q_ref[tile_rows(t), (u // 2) * LANES:(u // 2 + 1) * LANES]
            keep = low_half if u % 2 == 0 else jnp.logical_not(low_half)
            q_sc[slot, u * tq:(u + 1) * tq, :] = jnp.where(keep, src, jnp.zeros_like(src))

    def scores(t, c, slot, par):
        s = lax.dot_general(q_sc[slot], k_ref[chunk_rows(c), :], (((1,), (1,)), ((), ())),
                            preferred_element_type=F32)
        if alibi:
            base = (t * tq - c * tk).astype(F32) if not (isinstance(t, int) and isinstance(c, int)) \
                else float(t * tq - c * tk)
            bias = neg_slope * jnp.abs(rc + base)
            s = s + jnp.concatenate([bias] * nstack, axis=0)
        s_bufs[par][...] = s

    def softmax(slot, par, first):
        s = s_bufs[par][...]
        m_cur = jnp.max(s, axis=-1, keepdims=True)
        if first:
            m_new = jnp.broadcast_to(m_cur, (nrows, LANES))
        else:
            m_old = m_sc[slot]
            m_new = jnp.maximum(m_old, m_cur)
            a_bufs[par][...] = jnp.exp2(m_old - m_new)
        p_bufs[par][...] = jnp.exp2(s - pltpu.repeat(m_new, tk // LANES, 1)).astype(BF16)
        m_sc[slot] = m_new

    def values(c, slot, par, first):
        d = jnp.dot(p_bufs[par][...], v_sc[chunk_rows(c), :], preferred_element_type=F32)
        if first:
            acc_sc[slot] = d
        else:
            acc_sc[slot] = pltpu.repeat(a_bufs[par][...], 2, 1) * acc_sc[slot] + d

    def finalize(t, slot):
        acc = acc_sc[slot]
        o = acc[:, 0:LANES] * (1.0 / acc[:, LANES:2 * LANES])
        if alibi:
            ob = o[0:tq, :] - lam * o[tq:2 * tq, :]
            ms = jnp.mean(ob * ob, axis=-1, keepdims=True)
            ob = ob * lax.rsqrt(ms + EPS) * subg_ref[...] * (1.0 - lam_init)
            o_ref[tile_rows(t), :] = ob.astype(o_ref.dtype)
        else:
            for pair in range(nstack // 2):
                lo = o[(2 * pair) * tq:(2 * pair + 1) * tq, :]
                hi = o[(2 * pair + 1) * tq:(2 * pair + 2) * tq, :]
                o_ref[tile_rows(t), pair * LANES:(pair + 1) * LANES] = (
                    jnp.where(low_half, lo, hi).astype(o_ref.dtype))

    def step(t, t_next, slot, c):
        static = isinstance(c, int)
        par = c % 2 if static else None
        ahead2 = c + 2
        if static and ahead2 >= nchunks:
            scores(t_next, ahead2 - nchunks, 1 - slot, par)
        else:
            scores(t, ahead2, slot, par)
        if static and c + 1 >= nchunks:
            softmax(1 - slot, 1 - par, first=True)
        else:
            softmax(slot, 1 - par, first=False)
        values(c, slot, par, first=static and c == 0)

    load_queries(0, 0)
    scores(0, 0, 0, 0)
    scores(0, 1, 0, 1)
    softmax(0, 0, first=True)

    def tile_body(t, carry):
        slot = t % 2
        t_next = (t + 1) % ntiles
        load_queries(t_next, 1 - slot)
        lead = min(2, nchunks - 2)
        for c in range(lead):
            step(t, t_next, slot, c)

        def pair_body(j, inner):
            for par in range(2):
                c = 2 * j + par
                scores(t, c + 2, slot, par)
                softmax(slot, 1 - par, first=False)
                values(c, slot, par, first=False)
            return inner

        lax.fori_loop(lead // 2, (nchunks - 2) // 2, pair_body, 0)
        for c in range(nchunks - 2, nchunks):
            step(t, t_next, slot, c)
        finalize(t, slot)
        return carry

    lax.fori_loop(0, ntiles, tile_body, 0)


def _flash_attention(q, k, v, slopes, lam_pack, subg, *, nbatch, seq, ngroups, nstack, alibi, lam_init,
                     nrows=1024):
    qw = (nstack // 2) * LANES
    tq = nrows // nstack
    tk = min(1024, seq // 2)
    kernel = functools.partial(_flash_kernel, tq=tq, tk=tk, seq=seq, nstack=nstack, alibi=alibi,
                               lam_init=lam_init)
    grid_spec = pltpu.PrefetchScalarGridSpec(
        num_scalar_prefetch=1,
        grid=(nbatch, ngroups),
        in_specs=[pl.BlockSpec((None, seq, qw), lambda b, g, s: (b, 0, g)),
                  pl.BlockSpec((None, seq, LANES), lambda b, g, s: (b, 0, g)),
                  pl.BlockSpec((None, seq, LANES), lambda b, g, s: (b, 0, g)),
                  pl.BlockSpec(lam_pack.shape, lambda b, g, s: (0, 0)),
                  pl.BlockSpec(subg.shape, lambda b, g, s: (0, 0))],
        out_specs=pl.BlockSpec((None, seq, qw), lambda b, g, s: (b, 0, g)),
        scratch_shapes=[pltpu.VMEM((2, nrows, LANES), BF16),
                        pltpu.VMEM((seq, 2 * LANES), BF16),
                        pltpu.VMEM((2, nrows, LANES), F32),
                        pltpu.VMEM((2, nrows, 2 * LANES), F32),
                        pltpu.VMEM((nrows, tk), F32), pltpu.VMEM((nrows, tk), F32),
                        pltpu.VMEM((nrows, tk), BF16), pltpu.VMEM((nrows, tk), BF16),
                        pltpu.VMEM((nrows, LANES), F32), pltpu.VMEM((nrows, LANES), F32)])
    return pl.pallas_call(
        kernel,
        grid_spec=grid_spec,
        out_shape=jax.ShapeDtypeStruct((nbatch, seq, ngroups * qw), BF16),
        compiler_params=_cparams(("arbitrary", "arbitrary")),
        name="flash_alibi" if alibi else "flash_gqa",
    )(slopes, q, k, v, lam_pack, subg)


def _natten_kernel(q_ref, k0, k1, k2, k3, v0, v1, v2, v3, bias_ref, o_ref):
    lane = lax.broadcasted_iota(jnp.int32, (1, LANES), 1)
    low_half = lane < HEAD_DIM
    ones = jnp.ones((NAT_KBLK, LANES), BF16)
    vexts = [jnp.concatenate([v_ref[...], ones], axis=1) for v_ref in (v0, v1, v2, v3)]
    nsub = 2
    rb = NAT_TQ // nsub
    for r in range(nsub):
        q = q_ref[r * rb:(r + 1) * rb, :]
        zero = jnp.zeros_like(q)
        outs = []
        for head in range(2):
            qh = jnp.where(low_half, q, zero) if head == 0 else jnp.where(low_half, zero, q)
            s_parts = []
            for c, k_ref in enumerate((k0, k1, k2, k3)):
                s = lax.dot_general(qh, k_ref[...], (((1,), (1,)), ((), ())), preferred_element_type=F32)
                s_parts.append(s + bias_ref[head, r * rb:(r + 1) * rb, c * NAT_KBLK:(c + 1) * NAT_KBLK])
            m = functools.reduce(jnp.maximum, [jnp.max(s, axis=-1, keepdims=True) for s in s_parts])
            acc = None
            for s, vext in zip(s_parts, vexts):
                d = jnp.dot(jnp.exp2(s - m).astype(BF16), vext, preferred_element_type=F32)
                acc = d if acc is None else acc + d
            outs.append(acc[:, 0:LANES] * (1.0 / acc[:, LANES:2 * LANES]))
        o_ref[r * rb:(r + 1) * rb, :] = jnp.where(low_half, outs[0], outs[1]).astype(o_ref.dtype)


def _natten(q, k, v, bias_tab, *, nbatch, seq):
    ntiles = seq // NAT_TQ
    npairs = C_HEADS // 2
    kblocks = seq // NAT_KBLK

    def win(i):
        return jnp.clip(2 * i - 1, 0, kblocks - NAT_TK // NAT_KBLK)

    def cls(i):
        return jnp.where(i == 0, 0, jnp.where(i == ntiles - 1, 2, 1))

    def kv_spec(j):
        return pl.BlockSpec((None, NAT_KBLK, LANES), lambda i, p, b: (b, win(i) + j, p))

    return pl.pallas_call(
        _natten_kernel,
        grid=(ntiles, npairs, nbatch),
        in_specs=[pl.BlockSpec((None, NAT_TQ, LANES), lambda i, p, b: (b, i, p))]
                 + [kv_spec(j) for j in range(4)] + [kv_spec(j) for j in range(4)]
                 + [pl.BlockSpec((None, None, 2, NAT_TQ, NAT_TK), lambda i, p, b: (cls(i), p, 0, 0, 0))],
        out_specs=pl.BlockSpec((None, NAT_TQ, LANES), lambda i, p, b: (b, i, p)),
        out_shape=jax.ShapeDtypeStruct((nbatch, seq, C_HEADS * HEAD_DIM), BF16),
        compiler_params=_cparams(("arbitrary", "arbitrary", "arbitrary")),
        name="natten",
    )(q, k, k, k, k, v, v, v, v, bias_tab)


def _natten_bias_table(rpb, seq):
    rows = seq // GRID_W
    ntiles = rows // NAT_QROWS
    col = jnp.arange(GRID_W)
    cstart = jnp.clip(col - WIN_W // 2, 0, GRID_W - WIN_W)
    col_valid = (col[None, :] >= cstart[:, None]) & (col[None, :] < cstart[:, None] + WIN_W)
    dc_idx = jnp.clip(col[None, :] - col[:, None] + WIN_W - 1, 0, 2 * WIN_W - 2)
    rpb_cols = rpb[:, :, dc_idx]
    tabs = []
    for tile in (0, 1, ntiles - 1):
        r = tile * NAT_QROWS + jnp.arange(NAT_QROWS)
        w0 = int(np.clip(tile * NAT_QROWS - WIN_H // 2, 0, rows - NAT_KROWS))
        kr = w0 + jnp.arange(NAT_KROWS)
        rstart = jnp.clip(r - WIN_H // 2, 0, rows - WIN_H)
        row_valid = (kr[None, :] >= rstart[:, None]) & (kr[None, :] < rstart[:, None] + WIN_H)
        dr_idx = jnp.clip(kr[None, :] - r[:, None] + WIN_H - 1, 0, 2 * WIN_H - 2)
        pick = (dr_idx[:, :, None] == jnp.arange(2 * WIN_H - 1)[None, None, :]).astype(F32)
        b = jnp.einsum("qkd,hdcx->hqckx", pick, rpb_cols * LOG2E,
                       precision=lax.Precision.HIGHEST)
        valid = row_valid[:, None, :, None] & col_valid[None, :, None, :]
        b = jnp.where(valid[None], b, -jnp.inf)
        tabs.append(b.reshape(C_HEADS, NAT_TQ, NAT_TK))
    return jnp.stack(tabs).reshape(3, C_HEADS // 2, 2, NAT_TQ, NAT_TK)


def _outproj_kernel(x_ref, gate_ref, *rest, nmix):
    mix_refs, w_refs, o_ref = rest[:nmix], rest[nmix:2 * nmix], rest[2 * nmix]
    y = None
    for m_ref, w_ref in zip(mix_refs, w_refs):
        d = jnp.dot(m_ref[...], w_ref[...], preferred_element_type=F32)
        y = d if y is None else y + d
    o_ref[...] = x_ref[...] + gate_ref[...] * y


def _outproj(x2, mod, mixes, ws, seq, tm=512):
    t = x2.shape[0]
    row = lambda i: (i, 0)
    const = lambda i: (0, 0)
    return pl.pallas_call(
        functools.partial(_outproj_kernel, nmix=len(mixes)),
        grid=(t // tm,),
        in_specs=[pl.BlockSpec((tm, D_MODEL), row), _mod_spec(2, tm, seq)]
                 + [pl.BlockSpec((tm, m.shape[1]), row) for m in mixes]
                 + [pl.BlockSpec(w.shape, const) for w in ws],
        out_specs=pl.BlockSpec((tm, D_MODEL), row),
        out_shape=jax.ShapeDtypeStruct((t, D_MODEL), F32),
        compiler_params=_cparams(("arbitrary",)),
        name="outproj_residual",
    )(x2, mod, *mixes, *ws)


def _ffn_kernel(x_ref, sh_ref, sc_ref, gate_ref, g_ref, w1_ref, w3_ref, w2_ref, o_ref, h_sc, acc_sc, *, nf):
    f = pl.program_id(1)

    @pl.when(f == 0)
    def _():
        h_sc[...] = _norm_mod(x_ref[...], g_ref[...], sh_ref[...], sc_ref[...]).astype(BF16)

    h = h_sc[...]
    a = jnp.dot(h, w1_ref[...], preferred_element_type=F32)
    b = jnp.dot(h, w3_ref[...], preferred_element_type=F32)
    y = jnp.dot((a * _sigmoid(a) * b).astype(BF16), w2_ref[...], preferred_element_type=F32)

    @pl.when(f == 0)
    def _():
        acc_sc[...] = y

    @pl.when(f > 0)
    def _():
        acc_sc[...] += y

    @pl.when(f == nf - 1)
    def _():
        o_ref[...] = x_ref[...] + gate_ref[...] * acc_sc[...]


def _ffn(x2, mod, norm_g, w1, w3, w2, seq, tm=512, tf=1408):
    t = x2.shape[0]
    nf = w1.shape[1] // tf
    row = lambda i, f: (i, 0)
    return pl.pallas_call(
        functools.partial(_ffn_kernel, nf=nf),
        grid=(t // tm, nf),
        in_specs=[pl.BlockSpec((tm, D_MODEL), row),
                  _mod_spec(3, tm, seq), _mod_spec(4, tm, seq), _mod_spec(5, tm, seq),
                  pl.BlockSpec((1, D_MODEL), lambda i, f: (0, 0)),
                  pl.BlockSpec((D_MODEL, tf), lambda i, f: (0, f)),
                  pl.BlockSpec((D_MODEL, tf), lambda i, f: (0, f)),
                  pl.BlockSpec((tf, D_MODEL), lambda i, f: (f, 0))],
        out_specs=pl.BlockSpec((tm, D_MODEL), row),
        out_shape=jax.ShapeDtypeStruct((t, D_MODEL), F32),
        scratch_shapes=[pltpu.VMEM((tm, D_MODEL), BF16), pltpu.VMEM((tm, D_MODEL), F32)],
        compiler_params=_cparams(("arbitrary", "arbitrary")),
        name="ffn_swiglu",
    )(x2, mod, mod, mod, norm_g.reshape(1, D_MODEL), w1, w3, w2)


def _router_kernel(x_ref, sh_ref, sc_ref, g_ref, rw_ref, rb_ref, h_ref, meta_ref):
    h = _norm_mod(x_ref[...], g_ref[...], sh_ref[...], sc_ref[...])
    h_ref[...] = h.astype(BF16)
    lane = lax.broadcasted_iota(jnp.int32, (1, LANES), 1).astype(F32)
    logits = jnp.dot(h, rw_ref[...], preferred_element_type=F32,
                     precision=lax.Precision.HIGHEST) + rb_ref[...]
    logits = jnp.where(lane < N_EXPERTS, logits, -jnp.inf)
    m1 = jnp.max(logits, axis=-1, keepdims=True)
    i1 = jnp.min(jnp.where(logits == m1, lane, float(LANES)), axis=-1, keepdims=True)
    rest = jnp.where(lane == i1, -jnp.inf, logits)
    m2 = jnp.max(rest, axis=-1, keepdims=True)
    i2 = jnp.min(jnp.where(rest == m2, lane, float(LANES)), axis=-1, keepdims=True)
    e = jnp.exp(m2 - m1)
    g1 = 1.0 / (1.0 + e)
    g2 = e * g1
    meta = jnp.where(lane == 0, i1, jnp.where(lane == 1, i2, jnp.where(lane == 2, g1,
                                                                       jnp.where(lane == 3, g2, 0.0))))
    meta_ref[...] = meta


def _router(x2, mod, norm_g, rw_pad, rb_pad, seq, tm=512):
    t = x2.shape[0]
    row = lambda i: (i, 0)
    const = lambda i: (0, 0)
    return pl.pallas_call(
        _router_kernel,
        grid=(t // tm,),
        in_specs=[pl.BlockSpec((tm, D_MODEL), row), _mod_spec(3, tm, seq), _mod_spec(4, tm, seq),
                  pl.BlockSpec((1, D_MODEL), const),
                  pl.BlockSpec((D_MODEL, LANES), const), pl.BlockSpec((1, LANES), const)],
        out_specs=[pl.BlockSpec((tm, D_MODEL), row), pl.BlockSpec((tm, LANES), row)],
        out_shape=[jax.ShapeDtypeStruct((t, D_MODEL), BF16), jax.ShapeDtypeStruct((t, LANES), F32)],
        compiler_params=_cparams(("arbitrary",)),
        name="moe_router",
    )(x2, mod, mod, norm_g.reshape(1, D_MODEL), rw_pad, rb_pad)


def _moe_kernel(te_ref, nt_ref, x_ref, w1_ref, w3_ref, w2_ref, o_ref, acc_sc, *, nf):
    i = pl.program_id(0)
    f = pl.program_id(1)
    active = i < nt_ref[0]

    @pl.when(active)
    def _():
        x = x_ref[...]
        a = jnp.dot(x, w1_ref[...], preferred_element_type=F32)
        b = jnp.dot(x, w3_ref[...], preferred_element_type=F32)
        y = jnp.dot((a * _sigmoid(a) * b).astype(BF16), w2_ref[...], preferred_element_type=F32)

        @pl.when(f == 0)
        def _():
            acc_sc[...] = y

        @pl.when(f > 0)
        def _():
            acc_sc[...] += y

        @pl.when(f == nf - 1)
        def _():
            o_ref[...] = acc_sc[...].astype(o_ref.dtype)

    @pl.when(jnp.logical_and(jnp.logical_not(active), f == nf - 1))
    def _():
        o_ref[...] = jnp.zeros(o_ref.shape, o_ref.dtype)


def _moe_grouped(xs, tile_expert, num_tiles, w1, w3, w2, tg, tf=512):
    p = xs.shape[0]
    nf = w1.shape[2] // tf
    grid_spec = pltpu.PrefetchScalarGridSpec(
        num_scalar_prefetch=2,
        grid=(p // tg, nf),
        in_specs=[pl.BlockSpec((tg, D_MODEL), lambda i, f, te, nt: (i, 0)),
                  pl.BlockSpec((None, D_MODEL, tf), lambda i, f, te, nt: (te[i], 0, f)),
                  pl.BlockSpec((None, D_MODEL, tf), lambda i, f, te, nt: (te[i], 0, f)),
                  pl.BlockSpec((None, tf, D_MODEL), lambda i, f, te, nt: (te[i], f, 0))],
        out_specs=pl.BlockSpec((tg, D_MODEL), lambda i, f, te, nt: (i, 0)),
        scratch_shapes=[pltpu.VMEM((tg, D_MODEL), F32)])
    return pl.pallas_call(
        functools.partial(_moe_kernel, nf=nf),
        grid_spec=grid_spec,
        out_shape=jax.ShapeDtypeStruct((p, D_MODEL), BF16),
        compiler_params=_cparams(("arbitrary", "arbitrary")),
        name="moe_grouped",
    )(tile_expert, num_tiles, xs, w1, w3, w2)


def _combine_kernel(x_ref, gate_ref, meta_ref, y1_ref, y2_ref, o_ref):
    meta = meta_ref[...]
    moe = meta[:, 2:3] * y1_ref[...].astype(F32) + meta[:, 3:4] * y2_ref[...].astype(F32)
    o_ref[...] = x_ref[...] + gate_ref[...] * moe


def _combine(x2, mod, meta, y1, y2, seq, tm=512):
    t = x2.shape[0]
    row = lambda i: (i, 0)
    return pl.pallas_call(
        _combine_kernel,
        grid=(t // tm,),
        in_specs=[pl.BlockSpec((tm, D_MODEL), row), _mod_spec(5, tm, seq),
                  pl.BlockSpec((tm, LANES), row),
                  pl.BlockSpec((tm, D_MODEL), row), pl.BlockSpec((tm, D_MODEL), row)],
        out_specs=pl.BlockSpec((tm, D_MODEL), row),
        out_shape=jax.ShapeDtypeStruct((t, D_MODEL), F32),
        compiler_params=_cparams(("arbitrary",)),
        name="moe_combine",
    )(x2, mod, meta, y1, y2)


def _moe(x2, mod, norm_g, rw_pad, rb_pad, w1, w3, w2, seq, tg=1024):
    t = x2.shape[0]
    h, meta = _router(x2, mod, norm_g, rw_pad, rb_pad, seq)
    e_flat = meta[:, 0:2].astype(jnp.int32).reshape(-1)
    onehot = (e_flat[:, None] == jnp.arange(N_EXPERTS)[None, :]).astype(jnp.int32)
    csum = jnp.cumsum(onehot, axis=0)
    counts = csum[-1]
    rank = jnp.take_along_axis(csum, e_flat[:, None], axis=1)[:, 0] - 1
    padded = ((counts + tg - 1) // tg) * tg
    pend = jnp.cumsum(padded)
    pos = (pend - padded)[e_flat] + rank
    p_rows = 2 * t + N_EXPERTS * tg
    row_token = jnp.zeros((p_rows,), jnp.int32).at[pos].set(jnp.arange(2 * t, dtype=jnp.int32) // 2)
    tile_start = jnp.arange(p_rows // tg, dtype=jnp.int32) * tg
    tile_expert = jnp.minimum(jnp.sum((tile_start[:, None] >= pend[None, :]).astype(jnp.int32), axis=1),
                              N_EXPERTS - 1)
    num_tiles = (pend[-1] // tg).astype(jnp.int32).reshape(1)
    xs = jnp.take(h, row_token, axis=0)
    ys = _moe_grouped(xs, tile_expert, num_tiles, w1, w3, w2, tg)
    y1 = jnp.take(ys, pos[0::2], axis=0)
    y2 = jnp.take(ys, pos[1::2], axis=0)
    return _combine(x2, mod, meta, y1, y2, seq)


def _rope_tables(seq):
    t = np.arange(seq)
    lane = np.arange(LANES)
    d = lane % HEAD_DIM
    pos = np.where((d // 32)[None, :] == 0, (t // GRID_W)[:, None], (t % GRID_W)[:, None]).astype(np.float32)
    inv = (ROPE_THETA ** (-np.arange(16, dtype=np.float32) / 16)).astype(np.float32)
    ang = pos * inv[(d % 16)][None, :]
    return jnp.asarray(np.cos(ang), F32), jnp.asarray(np.sin(ang), F32)


def _pair_gain(g):
    return jnp.concatenate([g, g]).astype(F32)


def _prepare(p, seq):
    even, odd = {}, {}
    w_in = p["w_in_even"][0]
    qa, ka, va, qb, kb, vb = jnp.split(w_in, [512, 640, 768, 1280, 1792], axis=1)
    dup = lambda w: jnp.concatenate([w[:, 0:64], w[:, 0:64], w[:, 64:128], w[:, 64:128]], axis=1)
    even["w_in"] = jnp.concatenate([qa, qb, dup(ka), dup(va), kb, vb], axis=1).astype(BF16)
    gains = jnp.zeros((8, LANES), F32)
    gains = gains.at[0].set(_pair_gain(p["qnorm_a"][0])).at[1].set(_pair_gain(p["knorm_a"][0]))
    gains = gains.at[2].set(_pair_gain(p["qnorm_b"][0])).at[3].set(_pair_gain(p["knorm_b"][0]))
    even["gains"] = gains
    qscale = SCALE * LOG2E
    even["groups"] = ((512, 0, True, qscale), (512, 2, False, qscale), (256, 1, True, 1.0),
                      (256, None, False, 1.0), (512, 3, False, 1.0), (512, None, False, 1.0))
    lam = jnp.zeros((8, LANES), F32)
    for r, name in enumerate(("lam_q1", "lam_k1", "lam_q2", "lam_k2")):
        lam = lam.at[r, 0:HEAD_DIM].set(p[name][0])
    even["lam"] = lam
    even["subg"] = p["subln_b"][0].reshape(1, LANES).astype(F32)
    wo = p["w_out_even"][0].astype(BF16)
    even["wo"] = (wo[0:512], wo[512:1024])
    even["slopes"] = jnp.asarray(LOG2E * 2.0 ** (-8.0 * (np.arange(B_HEADS) + 1.0) / B_HEADS), F32)
    for name in ("ffn_w1", "ffn_w3", "ffn_w2"):
        even[name] = p[name][0].astype(BF16)
    for name in ("ada_w", "ada_b", "norm_mix", "norm_ffn"):
        even[name] = p[name + "_even"][0]
        odd[name] = p[name + "_odd"][0]

    odd["w_qkv"] = p["w_qkv_odd"][0].astype(BF16)
    gains = jnp.zeros((8, LANES), F32)
    odd["gains"] = gains.at[0].set(_pair_gain(p["qnorm_c"][0])).at[1].set(_pair_gain(p["knorm_c"][0]))
    odd["groups"] = ((1024, 0, False, qscale), (1024, 1, False, 1.0), (1024, None, False, 1.0))
    odd["bias_tab"] = _natten_bias_table(p["rpb_c"][0], seq)
    odd["wo"] = (p["w_out_odd"][0].astype(BF16),)
    odd["rw"] = jnp.zeros((D_MODEL, LANES), F32).at[:, 0:N_EXPERTS].set(p["router_w"][0])
    odd["rb"] = jnp.zeros((1, LANES), F32).at[0, 0:N_EXPERTS].set(p["router_b"][0])
    for name in ("moe_w1", "moe_w3", "moe_w2"):
        odd[name] = p[name][0].astype(BF16)
    cos_t, sn_t = _rope_tables(seq)
    gmat = jnp.asarray(np.kron(np.eye(2), np.full((HEAD_DIM, HEAD_DIM), 1.0 / HEAD_DIM)), BF16)
    return even, odd, (cos_t, sn_t, gmat)


def _even_layer(x2, c, ev, shared, nbatch, seq):
    cos_t, sn_t, gmat = shared
    mod = _ada_modulation(c, ev["ada_w"], ev["ada_b"])
    qa, qb, ka, va, kb, vb = _projection(x2, mod, ev["norm_mix"], ev["w_in"], cos_t, sn_t, ev["gains"], gmat,
                                         ev["groups"], seq)
    shp = lambda a: a.reshape(nbatch, seq, a.shape[1])
    lam_init = 0.8 - 0.6 * math.exp(-0.3 * 0)
    mix_a = _flash_attention(shp(qa), shp(ka), shp(va), ev["slopes"], ev["lam"], ev["subg"], nbatch=nbatch, seq=seq,
                             ngroups=A_KV_HEADS, nstack=4, alibi=False, lam_init=lam_init)
    mix_b = _flash_attention(shp(qb), shp(kb), shp(vb), ev["slopes"], ev["lam"], ev["subg"], nbatch=nbatch, seq=seq,
                             ngroups=B_HEADS, nstack=2, alibi=True, lam_init=lam_init)
    t = nbatch * seq
    x2 = _outproj(x2, mod, (mix_a.reshape(t, -1), mix_b.reshape(t, -1)), ev["wo"], seq)
    return _ffn(x2, mod, ev["norm_ffn"], ev["ffn_w1"], ev["ffn_w3"], ev["ffn_w2"], seq)


def _odd_layer(x2, c, od, shared, nbatch, seq):
    cos_t, sn_t, gmat = shared
    mod = _ada_modulation(c, od["ada_w"], od["ada_b"])
    q, k, v = _projection(x2, mod, od["norm_mix"], od["w_qkv"], cos_t, sn_t, od["gains"], gmat, od["groups"], seq)
    shp = lambda a: a.reshape(nbatch, seq, a.shape[1])
    mix = _natten(shp(q), shp(k), shp(v), od["bias_tab"], nbatch=nbatch, seq=seq)
    x2 = _outproj(x2, mod, (mix.reshape(nbatch * seq, -1),), od["wo"], seq)
    return _moe(x2, mod, od["norm_ffn"], od["rw"], od["rb"], od["moe_w1"], od["moe_w3"], od["moe_w2"], seq)


def _trunk(x, c, ev, od, shared):
    nbatch, seq, d = x.shape
    x2 = x.reshape(nbatch * seq, d)
    x2 = _even_layer(x2, c, ev, shared, nbatch, seq)
    x2 = _odd_layer(x2, c, od, shared, nbatch, seq)
    return x2.reshape(nbatch, seq, d)


def kernel(x_prompt, x_sample, c_prompt, c_sample, ada_w_even, ada_b_even, norm_mix_even, norm_ffn_even, w_in_even, qnorm_a, knorm_a, qnorm_b, knorm_b, lam_q1, lam_k1, lam_q2, lam_k2, subln_b, w_out_even, ffn_w1, ffn_w3, ffn_w2, ada_w_odd, ada_b_odd, norm_mix_odd, norm_ffn_odd, w_qkv_odd, qnorm_c, knorm_c, rpb_c, w_out_odd, router_w, router_b, moe_w1, moe_w3, moe_w2):
    params = dict(ada_w_even=ada_w_even, ada_b_even=ada_b_even, norm_mix_even=norm_mix_even,
                  norm_ffn_even=norm_ffn_even, w_in_even=w_in_even, qnorm_a=qnorm_a, knorm_a=knorm_a,
                  qnorm_b=qnorm_b, knorm_b=knorm_b, lam_q1=lam_q1, lam_k1=lam_k1, lam_q2=lam_q2, lam_k2=lam_k2,
                  subln_b=subln_b, w_out_even=w_out_even, ffn_w1=ffn_w1, ffn_w3=ffn_w3, ffn_w2=ffn_w2,
                  ada_w_odd=ada_w_odd, ada_b_odd=ada_b_odd, norm_mix_odd=norm_mix_odd, norm_ffn_odd=norm_ffn_odd,
                  w_qkv_odd=w_qkv_odd, qnorm_c=qnorm_c, knorm_c=knorm_c, rpb_c=rpb_c, w_out_odd=w_out_odd,
                  router_w=router_w, router_b=router_b, moe_w1=moe_w1, moe_w3=moe_w3, moe_w2=moe_w2)
    seq = x_prompt.shape[1]
    ev, od, shared = _prepare(params, seq)
    y_prompt = _trunk(x_prompt, c_prompt, ev, od, shared)
    y_sample = _trunk(x_sample, c_sample, ev, od, shared)
    return (y_prompt, y_sample)
```

```python
import functools
import math

import numpy as np
import jax
import jax.numpy as jnp
from jax import lax
from jax.experimental import pallas as pl
from jax.experimental.pallas import tpu as pltpu
from jax.experimental.pallas import tpu_sc as plsc

F32 = jnp.float32
BF16 = jnp.bfloat16

D_MODEL = 1024
HEAD_DIM = 64
LANES = 128
SCALE = HEAD_DIM ** -0.5
LOG2E = 1.4426950408889634
GRID_W = 64
EPS = 1e-6
ROPE_THETA = 10000.0
A_Q_HEADS = 8
A_KV_HEADS = 2
B_HEADS = 4
C_HEADS = 16
WIN_H = 8
WIN_W = 16
N_EXPERTS = 8
D_FF = 2816
D_FF_EXPERT = 3584
VMEM_LIMIT = 56 * 1024 * 1024

NAT_QROWS = 8
NAT_KROWS = 16
NAT_TQ = NAT_QROWS * GRID_W
NAT_TK = NAT_KROWS * GRID_W
NAT_KBLK = 256
PACK_W = D_MODEL // 2
SC_WINDOW = 128
SC_ROW_WORDS = 256


def _cparams(sem):
    return pltpu.CompilerParams(dimension_semantics=sem, vmem_limit_bytes=VMEM_LIMIT)


def _norm_mod(x, g, shift, scale):
    ms = jnp.mean(x * x, axis=-1, keepdims=True)
    y = x * lax.rsqrt(ms + EPS) * g
    return y * (1.0 + scale) + shift


def _head_norm(x, gain, gmat):
    ms = jnp.dot((x * x).astype(BF16), gmat, preferred_element_type=F32)
    return x * lax.rsqrt(ms + EPS) * gain


def _rope(x, cos, sn, first_quarter):
    up = pltpu.roll(x, LANES - 16, 1)
    down = pltpu.roll(x, 16, 1)
    return x * cos + sn * jnp.where(first_quarter, -up, down)


def _sigmoid(a):
    return 1.0 / (1.0 + jnp.exp(-a))


def _pack_halves(a, b):
    hi = lax.bitcast_convert_type(a.astype(BF16).astype(F32), jnp.int32)
    lo = lax.bitcast_convert_type(b.astype(BF16).astype(F32), jnp.int32)
    return hi | lax.shift_right_logical(lo, jnp.full_like(lo, 16))


def _unpack_halves(w):
    hi = lax.bitcast_convert_type(w & jnp.int32(-65536), F32)
    lo = lax.bitcast_convert_type(lax.shift_left(w, jnp.full_like(w, 16)), F32)
    return jnp.concatenate([hi, lo], axis=1)


def _ada_kernel(c_ref, w_ref, b_ref, o_ref):
    c = c_ref[...]
    s = c * _sigmoid(c)
    o_ref[...] = jnp.dot(s, w_ref[...], preferred_element_type=F32,
                         precision=lax.Precision.HIGHEST) + b_ref[...]


def _ada_modulation(c, w, b):
    nb, d = c.shape
    n = w.shape[1]
    tn = 512
    mod = pl.pallas_call(
        _ada_kernel,
        grid=(n // tn,),
        in_specs=[pl.BlockSpec((nb, d), lambda j: (0, 0)),
                  pl.BlockSpec((d, tn), lambda j: (0, j)),
                  pl.BlockSpec((1, tn), lambda j: (0, j))],
        out_specs=pl.BlockSpec((nb, tn), lambda j: (0, j)),
        out_shape=jax.ShapeDtypeStruct((nb, n), F32),
        compiler_params=_cparams(("arbitrary",)),
        name="ada_mod",
    )(c, w, b.reshape(1, n))
    return mod.reshape(nb, 6, 1, d)


def _mod_spec(k, tm, seq):
    return pl.BlockSpec((None, None, 1, D_MODEL), lambda i, *_: ((i * tm) // seq, k, 0, 0))


def _proj_kernel(x_ref, sh_ref, sc_ref, g_ref, w_ref, cos_ref, sn_ref, gains_ref, gmat_ref, *o_refs,
                 groups):
    h = _norm_mod(x_ref[...], g_ref[...], sh_ref[...], sc_ref[...])
    y = jnp.dot(h.astype(BF16), w_ref[...], preferred_element_type=F32)
    lane = lax.broadcasted_iota(jnp.int32, (1, LANES), 1)
    first_quarter = (lane % 32) < 16
    gmat = gmat_ref[...]
    off = 0
    for o_ref, (width, gain_row, rope, mult) in zip(o_refs, groups):
        if gain_row is None:
            o_ref[...] = y[:, off:off + width].astype(o_ref.dtype)
        else:
            gain = gains_ref[gain_row:gain_row + 1, :]
            for t in range(width // LANES):
                z = _head_norm(y[:, off + t * LANES: off + (t + 1) * LANES], gain, gmat)
                if rope:
                    z = _rope(z, cos_ref[...], sn_ref[...], first_quarter)
                if mult != 1.0:
                    z = z * mult
                o_ref[:, t * LANES:(t + 1) * LANES] = z.astype(o_ref.dtype)
        off += width


def _projection(x2, mod, norm_g, w, cos_t, sn_t, gains, gmat, groups, seq, tm=512):
    t = x2.shape[0]
    n = w.shape[1]
    nseq = seq // tm
    row = lambda i: (i, 0)
    const = lambda i: (0, 0)
    tab = lambda i: (i % nseq, 0)
    return pl.pallas_call(
        functools.partial(_proj_kernel, groups=groups),
        grid=(t // tm,),
        in_specs=[pl.BlockSpec((tm, D_MODEL), row),
                  _mod_spec(0, tm, seq), _mod_spec(1, tm, seq),
                  pl.BlockSpec((1, D_MODEL), const),
                  pl.BlockSpec((D_MODEL, n), const),
                  pl.BlockSpec((tm, LANES), tab), pl.BlockSpec((tm, LANES), tab),
                  pl.BlockSpec(gains.shape, const),
                  pl.BlockSpec((LANES, LANES), const)],
        out_specs=[pl.BlockSpec((tm, g[0]), row) for g in groups],
        out_shape=[jax.ShapeDtypeStruct((t, g[0]), BF16) for g in groups],
        compiler_params=_cparams(("arbitrary",)),
        name="norm_mod_proj",
    )(x2, mod, mod, norm_g.reshape(1, D_MODEL), w, cos_t, sn_t, gains, gmat)


def _flash_kernel(slope_ref, q_ref, k_ref, v_ref, lam_ref, subg_ref, o_ref,
                  q_sc, v_sc, m_sc, acc_sc, s0_sc, s1_sc, p0_sc, p1_sc, a0_sc, a1_sc,
                  *, tq, tk, seq, nstack, alibi, lam_init):
    g = pl.program_id(1)
    lane = lax.broadcasted_iota(jnp.int32, (1, LANES), 1)
    low_half = lane < HEAD_DIM
    s_bufs, p_bufs, a_bufs = (s0_sc, s1_sc), (p0_sc, p1_sc), (a0_sc, a1_sc)
    nchunks = seq // tk
    ntiles = seq // tq
    nrows = nstack * tq

    v_sc[:, 0:LANES] = v_ref[...]
    v_sc[:, LANES:2 * LANES] = jnp.ones((seq, LANES), BF16)

    if alibi:
        rc = (lax.broadcasted_iota(jnp.int32, (tq, tk), 0)
              - lax.broadcasted_iota(jnp.int32, (tq, tk), 1)).astype(F32)
        neg_slope = -slope_ref[g]
        lp = lam_ref[...]
        l1 = jnp.sum(lp[0:1, :] * lp[1:2, :], axis=-1, keepdims=True)
        l2 = jnp.sum(lp[2:3, :] * lp[3:4, :], axis=-1, keepdims=True)
        lam = jnp.exp(l1) - jnp.exp(l2) + lam_init

    def tile_rows(t):
        return pl.ds(t * tq, tq) if isinstance(t, int) else pl.ds(pl.multiple_of(t * tq, tq), tq)

    def chunk_rows(c):
        return pl.ds(c * tk, tk) if isinstance(c, int) else pl.ds(pl.multiple_of(c * tk, tk), tk)

    def load_queries(t, slot):
        for u in range(nstack):
            src = q_ref[tile_rows(t), (u // 2) * LANES:(u // 2 + 1) * LANES]
            keep = low_half if u % 2 == 0 else jnp.logical_not(low_half)
            q_sc[slot, u * tq:(u + 1) * tq, :] = jnp.where(keep, src, jnp.zeros_like(src))

    def scores(t, c, slot, par):
        s = lax.dot_general(q_sc[slot], k_ref[chunk_rows(c), :], (((1,), (1,)), ((), ())),
                            preferred_element_type=F32)
        if alibi:
            base = (t * tq - c * tk).astype(F32) if not (isinstance(t, int) and isinstance(c, int)) \
                else float(t * tq - c * tk)
            bias = neg_slope * jnp.abs(rc + base)
            s = s + jnp.concatenate([bias] * nstack, axis=0)
        s_bufs[par][...] = s

    def softmax(slot, par, first):
        s = s_bufs[par][...]
        m_cur = jnp.max(s, axis=-1, keepdims=True)
        if first:
            m_new = jnp.broadcast_to(m_cur, (nrows, LANES))
        else:
            m_old = m_sc[slot]
            m_new = jnp.maximum(m_old, m_cur)
            a_bufs[par][...] = jnp.exp2(m_old - m_new)
        p_bufs[par][...] = jnp.exp2(s - pltpu.repeat(m_new, tk // LANES, 1)).astype(BF16)
        m_sc[slot] = m_new

    def values(c, slot, par, first):
        d = jnp.dot(p_bufs[par][...], v_sc[chunk_rows(c), :], preferred_element_type=F32)
        if first:
            acc_sc[slot] = d
        else:
            acc_sc[slot] = pltpu.repeat(a_bufs[par][...], 2, 1) * acc_sc[slot] + d

    def finalize(t, slot):
        acc = acc_sc[slot]
        o = acc[:, 0:LANES] * (1.0 / acc[:, LANES:2 * LANES])
        if alibi:
            ob = o[0:tq, :] - lam * o[tq:2 * tq, :]
            ms = jnp.mean(ob * ob, axis=-1, keepdims=True)
            ob = ob * lax.rsqrt(ms + EPS) * subg_ref[...] * (1.0 - lam_init)
            o_ref[tile_rows(t), :] = ob.astype(o_ref.dtype)
        else:
            for pair in range(nstack // 2):
                lo = o[(2 * pair) * tq:(2 * pair + 1) * tq, :]
                hi = o[(2 * pair + 1) * tq:(2 * pair + 2) * tq, :]
                o_ref[tile_rows(t), pair * LANES:(pair + 1) * LANES] = (
                    jnp.where(low_half, lo, hi).astype(o_ref.dtype))

    def step(t, t_next, slot, c):
        static = isinstance(c, int)
        par = c % 2 if static else None
        ahead2 = c + 2
        if static and ahead2 >= nchunks:
            scores(t_next, ahead2 - nchunks, 1 - slot, par)
        else:
            scores(t, ahead2, slot, par)
        if static and c + 1 >= nchunks:
            softmax(1 - slot, 1 - par, first=True)
        else:
            softmax(slot, 1 - par, first=False)
        values(c, slot, par, first=static and c == 0)

    load_queries(0, 0)
    scores(0, 0, 0, 0)
    scores(0, 1, 0, 1)
    softmax(0, 0, first=True)

    def tile_body(t, carry):
        slot = t % 2
        t_next = (t + 1) % ntiles
        load_queries(t_next, 1 - slot)
        lead = min(2, nchunks - 2)
        for c in range(lead):
            step(t, t_next, slot, c)

        def pair_body(j, inner):
            for par in range(2):
                c = 2 * j + par
                scores(t, c + 2, slot, par)
                softmax(slot, 1 - par, first=False)
                values(c, slot, par, first=False)
            return inner

        lax.fori_loop(lead // 2, (nchunks - 2) // 2, pair_body, 0)
        for c in range(nchunks - 2, nchunks):
            step(t, t_next, slot, c)
        finalize(t, slot)
        return carry

    lax.fori_loop(0, ntiles, tile_body, 0)


def _flash_attention(q, k, v, slopes, lam_pack, subg, *, nbatch, seq, ngroups, nstack, alibi, lam_init,
                     nrows=1024):
    qw = (nstack // 2) * LANES
    tq = nrows // nstack
    tk = min(1024, seq // 2)
    kernel = functools.partial(_flash_kernel, tq=tq, tk=tk, seq=seq, nstack=nstack, alibi=alibi,
                               lam_init=lam_init)
    grid_spec = pltpu.PrefetchScalarGridSpec(
        num_scalar_prefetch=1,
        grid=(nbatch, ngroups),
        in_specs=[pl.BlockSpec((None, seq, qw), lambda b, g, s: (b, 0, g)),
                  pl.BlockSpec((None, seq, LANES), lambda b, g, s: (b, 0, g)),
                  pl.BlockSpec((None, seq, LANES), lambda b, g, s: (b, 0, g)),
                  pl.BlockSpec(lam_pack.shape, lambda b, g, s: (0, 0)),
                  pl.BlockSpec(subg.shape, lambda b, g, s: (0, 0))],
        out_specs=pl.BlockSpec((None, seq, qw), lambda b, g, s: (b, 0, g)),
        scratch_shapes=[pltpu.VMEM((2, nrows, LANES), BF16),
                        pltpu.VMEM((seq, 2 * LANES), BF16),
                        pltpu.VMEM((2, nrows, LANES), F32),
                        pltpu.VMEM((2, nrows, 2 * LANES), F32),
                        pltpu.VMEM((nrows, tk), F32), pltpu.VMEM((nrows, tk), F32),
                        pltpu.VMEM((nrows, tk), BF16), pltpu.VMEM((nrows, tk), BF16),
                        pltpu.VMEM((nrows, LANES), F32), pltpu.VMEM((nrows, LANES), F32)])
    return pl.pallas_call(
        kernel,
        grid_spec=grid_spec,
        out_shape=jax.ShapeDtypeStruct((nbatch, seq, ngroups * qw), BF16),
        compiler_params=_cparams(("arbitrary", "arbitrary")),
        name="flash_alibi" if alibi else "flash_gqa",
    )(slopes, q, k, v, lam_pack, subg)


def _natten_kernel(q_ref, k0, k1, k2, k3, v0, v1, v2, v3, bias_ref, o_ref):
    lane = lax.broadcasted_iota(jnp.int32, (1, LANES), 1)
    low_half = lane < HEAD_DIM
    ones = jnp.ones((NAT_KBLK, LANES), BF16)
    vexts = [jnp.concatenate([v_ref[...], ones], axis=1) for v_ref in (v0, v1, v2, v3)]
    nsub = 2
    rb = NAT_TQ // nsub
    for r in range(nsub):
        q = q_ref[r * rb:(r + 1) * rb, :]
        zero = jnp.zeros_like(q)
        outs = []
        for head in range(2):
            qh = jnp.where(low_half, q, zero) if head == 0 else jnp.where(low_half, zero, q)
            s_parts = []
            for c, k_ref in enumerate((k0, k1, k2, k3)):
                s = lax.dot_general(qh, k_ref[...], (((1,), (1,)), ((), ())), preferred_element_type=F32)
                s_parts.append(s + bias_ref[head, r * rb:(r + 1) * rb, c * NAT_KBLK:(c + 1) * NAT_KBLK])
            m = functools.reduce(jnp.maximum, [jnp.max(s, axis=-1, keepdims=True) for s in s_parts])
            acc = None
            for s, vext in zip(s_parts, vexts):
                d = jnp.dot(jnp.exp2(s - m).astype(BF16), vext, preferred_element_type=F32)
                acc = d if acc is None else acc + d
            outs.append(acc[:, 0:LANES] * (1.0 / acc[:, LANES:2 * LANES]))
        o_ref[r * rb:(r + 1) * rb, :] = jnp.where(low_half, outs[0], outs[1]).astype(o_ref.dtype)


def _natten(q, k, v, bias_tab, *, nbatch, seq):
    ntiles = seq // NAT_TQ
    npairs = C_HEADS // 2
    kblocks = seq // NAT_KBLK

    def win(i):
        return jnp.clip(2 * i - 1, 0, kblocks - NAT_TK // NAT_KBLK)

    def cls(i):
        return jnp.where(i == 0, 0, jnp.where(i == ntiles - 1, 2, 1))

    def kv_spec(j):
        return pl.BlockSpec((None, NAT_KBLK, LANES), lambda i, p, b: (b, win(i) + j, p))

    return pl.pallas_call(
        _natten_kernel,
        grid=(ntiles, npairs, nbatch),
        in_specs=[pl.BlockSpec((None, NAT_TQ, LANES), lambda i, p, b: (b, i, p))]
                 + [kv_spec(j) for j in range(4)] + [kv_spec(j) for j in range(4)]
                 + [pl.BlockSpec((None, None, 2, NAT_TQ, NAT_TK), lambda i, p, b: (cls(i), p, 0, 0, 0))],
        out_specs=pl.BlockSpec((None, NAT_TQ, LANES), lambda i, p, b: (b, i, p)),
        out_shape=jax.ShapeDtypeStruct((nbatch, seq, C_HEADS * HEAD_DIM), BF16),
        compiler_params=_cparams(("arbitrary", "arbitrary", "arbitrary")),
        name="natten",
    )(q, k, k, k, k, v, v, v, v, bias_tab)


def _natten_bias_table(rpb, seq):
    rows = seq // GRID_W
    ntiles = rows // NAT_QROWS
    col = jnp.arange(GRID_W)
    cstart = jnp.clip(col - WIN_W // 2, 0, GRID_W - WIN_W)
    col_valid = (col[None, :] >= cstart[:, None]) & (col[None, :] < cstart[:, None] + WIN_W)
    dc_idx = jnp.clip(col[None, :] - col[:, None] + WIN_W - 1, 0, 2 * WIN_W - 2)
    rpb_cols = rpb[:, :, dc_idx]
    tabs = []
    for tile in (0, 1, ntiles - 1):
        r = tile * NAT_QROWS + jnp.arange(NAT_QROWS)
        w0 = int(np.clip(tile * NAT_QROWS - WIN_H // 2, 0, rows - NAT_KROWS))
        kr = w0 + jnp.arange(NAT_KROWS)
        rstart = jnp.clip(r - WIN_H // 2, 0, rows - WIN_H)
        row_valid = (kr[None, :] >= rstart[:, None]) & (kr[None, :] < rstart[:, None] + WIN_H)
        dr_idx = jnp.clip(kr[None, :] - r[:, None] + WIN_H - 1, 0, 2 * WIN_H - 2)
        pick = (dr_idx[:, :, None] == jnp.arange(2 * WIN_H - 1)[None, None, :]).astype(F32)
        b = jnp.einsum("qkd,hdcx->hqckx", pick, rpb_cols * LOG2E,
                       precision=lax.Precision.HIGHEST)
        valid = row_valid[:, None, :, None] & col_valid[None, :, None, :]
        b = jnp.where(valid[None], b, -jnp.inf)
        tabs.append(b.reshape(C_HEADS, NAT_TQ, NAT_TK))
    return jnp.stack(tabs).reshape(3, C_HEADS // 2, 2, NAT_TQ, NAT_TK)


def _outproj_kernel(x_ref, gate_ref, *rest, nmix):
    mix_refs, w_refs, o_ref = rest[:nmix], rest[nmix:2 * nmix], rest[2 * nmix]
    y = None
    for m_ref, w_ref in zip(mix_refs, w_refs):
        d = jnp.dot(m_ref[...], w_ref[...], preferred_element_type=F32)
        y = d if y is None else y + d
    o_ref[...] = x_ref[...] + gate_ref[...] * y


def _outproj(x2, mod, mixes, ws, seq, tm=512):
    t = x2.shape[0]
    row = lambda i: (i, 0)
    const = lambda i: (0, 0)
    return pl.pallas_call(
        functools.partial(_outproj_kernel, nmix=len(mixes)),
        grid=(t // tm,),
        in_specs=[pl.BlockSpec((tm, D_MODEL), row), _mod_spec(2, tm, seq)]
                 + [pl.BlockSpec((tm, m.shape[1]), row) for m in mixes]
                 + [pl.BlockSpec(w.shape, const) for w in ws],
        out_specs=pl.BlockSpec((tm, D_MODEL), row),
        out_shape=jax.ShapeDtypeStruct((t, D_MODEL), F32),
        compiler_params=_cparams(("arbitrary",)),
        name="outproj_residual",
    )(x2, mod, *mixes, *ws)


def _ffn_kernel(x_ref, sh_ref, sc_ref, gate_ref, g_ref, w1_ref, w3_ref, w2_ref, o_ref, h_sc, acc_sc, *, nf):
    f = pl.program_id(1)

    @pl.when(f == 0)
    def _():
        h_sc[...] = _norm_mod(x_ref[...], g_ref[...], sh_ref[...], sc_ref[...]).astype(BF16)

    h = h_sc[...]
    a = jnp.dot(h, w1_ref[...], preferred_element_type=F32)
    b = jnp.dot(h, w3_ref[...], preferred_element_type=F32)
    y = jnp.dot((a * _sigmoid(a) * b).astype(BF16), w2_ref[...], preferred_element_type=F32)

    @pl.when(f == 0)
    def _():
        acc_sc[...] = y

    @pl.when(f > 0)
    def _():
        acc_sc[...] += y

    @pl.when(f == nf - 1)
    def _():
        o_ref[...] = x_ref[...] + gate_ref[...] * acc_sc[...]


def _ffn(x2, mod, norm_g, w1, w3, w2, seq, tm=512, tf=1408):
    t = x2.shape[0]
    nf = w1.shape[1] // tf
    row = lambda i, f: (i, 0)
    return pl.pallas_call(
        functools.partial(_ffn_kernel, nf=nf),
        grid=(t // tm, nf),
        in_specs=[pl.BlockSpec((tm, D_MODEL), row),
                  _mod_spec(3, tm, seq), _mod_spec(4, tm, seq), _mod_spec(5, tm, seq),
                  pl.BlockSpec((1, D_MODEL), lambda i, f: (0, 0)),
                  pl.BlockSpec((D_MODEL, tf), lambda i, f: (0, f)),
                  pl.BlockSpec((D_MODEL, tf), lambda i, f: (0, f)),
                  pl.BlockSpec((tf, D_MODEL), lambda i, f: (f, 0))],
        out_specs=pl.BlockSpec((tm, D_MODEL), row),
        out_shape=jax.ShapeDtypeStruct((t, D_MODEL), F32),
        scratch_shapes=[pltpu.VMEM((tm, D_MODEL), BF16), pltpu.VMEM((tm, D_MODEL), F32)],
        compiler_params=_cparams(("arbitrary", "arbitrary")),
        name="ffn_swiglu",
    )(x2, mod, mod, mod, norm_g.reshape(1, D_MODEL), w1, w3, w2)


def _router_kernel(x_ref, sh_ref, sc_ref, g_ref, rw_ref, rb_ref, h_ref, meta_ref):
    h = _norm_mod(x_ref[...], g_ref[...], sh_ref[...], sc_ref[...])
    h_ref[...] = _pack_halves(h[:, 0:PACK_W], h[:, PACK_W:D_MODEL])
    lane = lax.broadcasted_iota(jnp.int32, (1, LANES), 1).astype(F32)
    logits = jnp.dot(h, rw_ref[...], preferred_element_type=F32,
                     precision=lax.Precision.HIGHEST) + rb_ref[...]
    logits = jnp.where(lane < N_EXPERTS, logits, -jnp.inf)
    m1 = jnp.max(logits, axis=-1, keepdims=True)
    i1 = jnp.min(jnp.where(logits == m1, lane, float(LANES)), axis=-1, keepdims=True)
    rest = jnp.where(lane == i1, -jnp.inf, logits)
    m2 = jnp.max(rest, axis=-1, keepdims=True)
    i2 = jnp.min(jnp.where(rest == m2, lane, float(LANES)), axis=-1, keepdims=True)
    e = jnp.exp(m2 - m1)
    g1 = 1.0 / (1.0 + e)
    g2 = e * g1
    meta = jnp.where(lane == 0, i1, jnp.where(lane == 1, i2, jnp.where(lane == 2, g1,
                                                                       jnp.where(lane == 3, g2, 0.0))))
    meta_ref[...] = meta


def _router(x2, mod, norm_g, rw_pad, rb_pad, seq, tm=512):
    t = x2.shape[0]
    row = lambda i: (i, 0)
    const = lambda i: (0, 0)
    return pl.pallas_call(
        _router_kernel,
        grid=(t // tm,),
        in_specs=[pl.BlockSpec((tm, D_MODEL), row), _mod_spec(3, tm, seq), _mod_spec(4, tm, seq),
                  pl.BlockSpec((1, D_MODEL), const),
                  pl.BlockSpec((D_MODEL, LANES), const), pl.BlockSpec((1, LANES), const)],
        out_specs=[pl.BlockSpec((tm, PACK_W), row), pl.BlockSpec((tm, LANES), row)],
        out_shape=[jax.ShapeDtypeStruct((t, PACK_W), jnp.int32), jax.ShapeDtypeStruct((t, LANES), F32)],
        compiler_params=_cparams(("arbitrary",)),
        name="moe_router",
    )(x2, mod, mod, norm_g.reshape(1, D_MODEL), rw_pad, rb_pad)


def _moe_kernel(te_ref, nt_ref, x_ref, w1_ref, w3_ref, w2a_ref, w2b_ref, o_ref, x_sc, g_sc, *, nf, tf):
    i = pl.program_id(0)
    j = pl.program_id(1)
    active = i < nt_ref[0]

    @pl.when(jnp.logical_and(active, j == 0))
    def _():
        x_sc[...] = _unpack_halves(x_ref[...]).astype(BF16)

    @pl.when(jnp.logical_and(active, j < nf))
    def _():
        x = x_sc[...]
        a = jnp.dot(x, w1_ref[...], preferred_element_type=F32)
        b = jnp.dot(x, w3_ref[...], preferred_element_type=F32)
        g_sc[j] = (a * _sigmoid(a) * b).astype(BF16)

    @pl.when(jnp.logical_and(active, j >= nf))
    def _():
        ya, yb = None, None
        for f in range(nf):
            g = g_sc[f]
            da = jnp.dot(g, w2a_ref[f * tf:(f + 1) * tf, :], preferred_element_type=F32)
            db = jnp.dot(g, w2b_ref[f * tf:(f + 1) * tf, :], preferred_element_type=F32)
            ya = da if ya is None else ya + da
            yb = db if yb is None else yb + db
        o_ref[...] = _pack_halves(ya, yb)

    @pl.when(jnp.logical_and(jnp.logical_not(active), j >= nf))
    def _():
        o_ref[...] = jnp.zeros(o_ref.shape, o_ref.dtype)


def _moe_grouped(xs, tile_expert, num_tiles, w1, w3, w2, tg, tf=512, tn=256):
    p = xs.shape[0]
    nf = w1.shape[2] // tf
    nb = PACK_W // tn
    fcl = lambda j: jnp.minimum(j, nf - 1)
    ncl = lambda j: jnp.maximum(j - nf, 0)
    grid_spec = pltpu.PrefetchScalarGridSpec(
        num_scalar_prefetch=2,
        grid=(p // tg, nf + nb),
        in_specs=[pl.BlockSpec((tg, PACK_W), lambda i, j, te, nt: (i, 0)),
                  pl.BlockSpec((None, D_MODEL, tf), lambda i, j, te, nt: (te[i], 0, fcl(j))),
                  pl.BlockSpec((None, D_MODEL, tf), lambda i, j, te, nt: (te[i], 0, fcl(j))),
                  pl.BlockSpec((None, nf * tf, tn), lambda i, j, te, nt: (te[i], 0, ncl(j))),
                  pl.BlockSpec((None, nf * tf, tn), lambda i, j, te, nt: (te[i], 0, nb + ncl(j)))],
        out_specs=pl.BlockSpec((tg, tn), lambda i, j, te, nt: (i, ncl(j))),
        scratch_shapes=[pltpu.VMEM((tg, D_MODEL), BF16), pltpu.VMEM((nf, tg, tf), BF16)])
    return pl.pallas_call(
        functools.partial(_moe_kernel, nf=nf, tf=tf),
        grid_spec=grid_spec,
        out_shape=jax.ShapeDtypeStruct((p, PACK_W), jnp.int32),
        compiler_params=_cparams(("arbitrary", "arbitrary")),
        name="moe_grouped",
    )(tile_expert, num_tiles, xs, w1, w3, w2, w2)


def _combine_kernel(x_ref, gate_ref, meta_ref, y1_ref, y2_ref, o_ref):
    meta = meta_ref[...]
    moe = meta[:, 2:3] * _unpack_halves(y1_ref[...]) + meta[:, 3:4] * _unpack_halves(y2_ref[...])
    o_ref[...] = x_ref[...] + gate_ref[...] * moe


def _combine(x2, mod, meta, yg, seq, tm=512):
    t = x2.shape[0]
    nt = t // tm
    row = lambda i: (i, 0)
    return pl.pallas_call(
        _combine_kernel,
        grid=(nt,),
        in_specs=[pl.BlockSpec((tm, D_MODEL), row), _mod_spec(5, tm, seq),
                  pl.BlockSpec((tm, LANES), row),
                  pl.BlockSpec((tm, PACK_W), row), pl.BlockSpec((tm, PACK_W), lambda i: (nt + i, 0))],
        out_specs=pl.BlockSpec((tm, D_MODEL), row),
        out_shape=jax.ShapeDtypeStruct((t, D_MODEL), F32),
        compiler_params=_cparams(("arbitrary",)),
        name="moe_combine",
    )(x2, mod, meta, yg, yg)


def _sc_gather(table, idx):
    n = idx.shape[0]
    nrows, width = table.shape
    split = width // SC_ROW_WORDS
    pieces = n * split
    idx_pieces = (idx[:, None] * split + jnp.arange(split, dtype=jnp.int32)[None, :]).reshape(1, pieces)
    mesh = plsc.VectorSubcoreMesh(core_axis_name="core", subcore_axis_name="subcore")

    @pl.kernel(out_type=jax.ShapeDtypeStruct((pieces, SC_ROW_WORDS), table.dtype), mesh=mesh, scratch_types=[])
    def gather_kernel(table_hbm, idx_hbm, out_hbm):
        def body(idx_vmem, out_vmem):
            pltpu.sync_copy(table_hbm.at[idx_vmem.at[0]], out_vmem)

        pltpu.emit_pipeline(
            body,
            grid=(pieces // SC_WINDOW,),
            in_specs=[pl.BlockSpec((1, SC_WINDOW), lambda i: (0, i))],
            out_specs=[pl.BlockSpec((SC_WINDOW, SC_ROW_WORDS), lambda i: (i, 0))],
            core_axis_name=("core", "subcore"),
            dimension_semantics=(pltpu.PARALLEL,),
        )(idx_hbm, out_hbm)

    out = gather_kernel(table.reshape(nrows * split, SC_ROW_WORDS), idx_pieces)
    return out.reshape(n, width)


def _moe(x2, mod, norm_g, rw_pad, rb_pad, w1, w3, w2, seq, tg=1024):
    t = x2.shape[0]
    h, meta = _router(x2, mod, norm_g, rw_pad, rb_pad, seq)
    e_flat = meta[:, 0:2].astype(jnp.int32).reshape(-1)
    onehot = (e_flat[:, None] == jnp.arange(N_EXPERTS)[None, :]).astype(jnp.int32)
    csum = jnp.cumsum(onehot, axis=0)
    counts = csum[-1]
    rank = jnp.take_along_axis(csum, e_flat[:, None], axis=1)[:, 0] - 1
    padded = ((counts + tg - 1) // tg) * tg
    pend = jnp.cumsum(padded)
    pos = (pend - padded)[e_flat] + rank
    p_rows = 2 * t + N_EXPERTS * tg
    row_token = jnp.zeros((p_rows,), jnp.int32).at[pos].set(jnp.arange(2 * t, dtype=jnp.int32) // 2)
    tile_start = jnp.arange(p_rows // tg, dtype=jnp.int32) * tg
    tile_expert = jnp.minimum(jnp.sum((tile_start[:, None] >= pend[None, :]).astype(jnp.int32), axis=1),
                              N_EXPERTS - 1)
    num_tiles = (pend[-1] // tg).astype(jnp.int32).reshape(1)
    xs = _sc_gather(h, row_token)
    ys = _moe_grouped(xs, tile_expert, num_tiles, w1, w3, w2, tg)
    yg = _sc_gather(ys, jnp.concatenate([pos[0::2], pos[1::2]]))
    return _combine(x2, mod, meta, yg, seq)


def _rope_tables(seq):
    t = np.arange(seq)
    lane = np.arange(LANES)
    d = lane % HEAD_DIM
    pos = np.where((d // 32)[None, :] == 0, (t // GRID_W)[:, None], (t % GRID_W)[:, None]).astype(np.float32)
    inv = (ROPE_THETA ** (-np.arange(16, dtype=np.float32) / 16)).astype(np.float32)
    ang = pos * inv[(d % 16)][None, :]
    return jnp.asarray(np.cos(ang), F32), jnp.asarray(np.sin(ang), F32)


def _pair_gain(g):
    return jnp.concatenate([g, g]).astype(F32)


def _prepare(p, seq):
    even, odd = {}, {}
    w_in = p["w_in_even"][0]
    qa, ka, va, qb, kb, vb = jnp.split(w_in, [512, 640, 768, 1280, 1792], axis=1)
    dup = lambda w: jnp.concatenate([w[:, 0:64], w[:, 0:64], w[:, 64:128], w[:, 64:128]], axis=1)
    even["w_in"] = jnp.concatenate([qa, qb, dup(ka), dup(va), kb, vb], axis=1).astype(BF16)
    gains = jnp.zeros((8, LANES), F32)
    gains = gains.at[0].set(_pair_gain(p["qnorm_a"][0])).at[1].set(_pair_gain(p["knorm_a"][0]))
    gains = gains.at[2].set(_pair_gain(p["qnorm_b"][0])).at[3].set(_pair_gain(p["knorm_b"][0]))
    even["gains"] = gains
    qscale = SCALE * LOG2E
    even["groups"] = ((512, 0, True, qscale), (512, 2, False, qscale), (256, 1, True, 1.0),
                      (256, None, False, 1.0), (512, 3, False, 1.0), (512, None, False, 1.0))
    lam = jnp.zeros((8, LANES), F32)
    for r, name in enumerate(("lam_q1", "lam_k1", "lam_q2", "lam_k2")):
        lam = lam.at[r, 0:HEAD_DIM].set(p[name][0])
    even["lam"] = lam
    even["subg"] = p["subln_b"][0].reshape(1, LANES).astype(F32)
    wo = p["w_out_even"][0].astype(BF16)
    even["wo"] = (wo[0:512], wo[512:1024])
    even["slopes"] = jnp.asarray(LOG2E * 2.0 ** (-8.0 * (np.arange(B_HEADS) + 1.0) / B_HEADS), F32)
    for name in ("ffn_w1", "ffn_w3", "ffn_w2"):
        even[name] = p[name][0].astype(BF16)
    for name in ("ada_w", "ada_b", "norm_mix", "norm_ffn"):
        even[name] = p[name + "_even"][0]
        odd[name] = p[name + "_odd"][0]

    odd["w_qkv"] = p["w_qkv_odd"][0].astype(BF16)
    gains = jnp.zeros((8, LANES), F32)
    odd["gains"] = gains.at[0].set(_pair_gain(p["qnorm_c"][0])).at[1].set(_pair_gain(p["knorm_c"][0]))
    odd["groups"] = ((1024, 0, False, qscale), (1024, 1, False, 1.0), (1024, None, False, 1.0))
    odd["bias_tab"] = _natten_bias_table(p["rpb_c"][0], seq)
    odd["wo"] = (p["w_out_odd"][0].astype(BF16),)
    odd["rw"] = jnp.zeros((D_MODEL, LANES), F32).at[:, 0:N_EXPERTS].set(p["router_w"][0])
    odd["rb"] = jnp.zeros((1, LANES), F32).at[0, 0:N_EXPERTS].set(p["router_b"][0])
    for name in ("moe_w1", "moe_w3", "moe_w2"):
        odd[name] = p[name][0].astype(BF16)
    cos_t, sn_t = _rope_tables(seq)
    gmat = jnp.asarray(np.kron(np.eye(2), np.full((HEAD_DIM, HEAD_DIM), 1.0 / HEAD_DIM)), BF16)
    return even, odd, (cos_t, sn_t, gmat)


def _even_layer(x2, c, ev, shared, nbatch, seq):
    cos_t, sn_t, gmat = shared
    mod = _ada_modulation(c, ev["ada_w"], ev["ada_b"])
    qa, qb, ka, va, kb, vb = _projection(x2, mod, ev["norm_mix"], ev["w_in"], cos_t, sn_t, ev["gains"], gmat,
                                         ev["groups"], seq)
    shp = lambda a: a.reshape(nbatch, seq, a.shape[1])
    lam_init = 0.8 - 0.6 * math.exp(-0.3 * 0)
    mix_a = _flash_attention(shp(qa), shp(ka), shp(va), ev["slopes"], ev["lam"], ev["subg"], nbatch=nbatch, seq=seq,
                             ngroups=A_KV_HEADS, nstack=4, alibi=False, lam_init=lam_init)
    mix_b = _flash_attention(shp(qb), shp(kb), shp(vb), ev["slopes"], ev["lam"], ev["subg"], nbatch=nbatch, seq=seq,
                             ngroups=B_HEADS, nstack=2, alibi=True, lam_init=lam_init)
    t = nbatch * seq
    x2 = _outproj(x2, mod, (mix_a.reshape(t, -1), mix_b.reshape(t, -1)), ev["wo"], seq)
    return _ffn(x2, mod, ev["norm_ffn"], ev["ffn_w1"], ev["ffn_w3"], ev["ffn_w2"], seq)


def _odd_layer(x2, c, od, shared, nbatch, seq):
    cos_t, sn_t, gmat = shared
    mod = _ada_modulation(c, od["ada_w"], od["ada_b"])
    q, k, v = _projection(x2, mod, od["norm_mix"], od["w_qkv"], cos_t, sn_t, od["gains"], gmat, od["groups"], seq)
    shp = lambda a: a.reshape(nbatch, seq, a.shape[1])
    mix = _natten(shp(q), shp(k), shp(v), od["bias_tab"], nbatch=nbatch, seq=seq)
    x2 = _outproj(x2, mod, (mix.reshape(nbatch * seq, -1),), od["wo"], seq)
    return _moe(x2, mod, od["norm_ffn"], od["rw"], od["rb"], od["moe_w1"], od["moe_w3"], od["moe_w2"], seq)


def _trunk(x, c, ev, od, shared):
    nbatch, seq, d = x.shape
    x2 = x.reshape(nbatch * seq, d)
    x2 = _even_layer(x2, c, ev, shared, nbatch, seq)
    x2 = _odd_layer(x2, c, od, shared, nbatch, seq)
    return x2.reshape(nbatch, seq, d)


def kernel(x_prompt, x_sample, c_prompt, c_sample, ada_w_even, ada_b_even, norm_mix_even, norm_ffn_even, w_in_even, qnorm_a, knorm_a, qnorm_b, knorm_b, lam_q1, lam_k1, lam_q2, lam_k2, subln_b, w_out_even, ffn_w1, ffn_w3, ffn_w2, ada_w_odd, ada_b_odd, norm_mix_odd, norm_ffn_odd, w_qkv_odd, qnorm_c, knorm_c, rpb_c, w_out_odd, router_w, router_b, moe_w1, moe_w3, moe_w2):
    params = dict(ada_w_even=ada_w_even, ada_b_even=ada_b_even, norm_mix_even=norm_mix_even,
                  norm_ffn_even=norm_ffn_even, w_in_even=w_in_even, qnorm_a=qnorm_a, knorm_a=knorm_a,
                  qnorm_b=qnorm_b, knorm_b=knorm_b, lam_q1=lam_q1, lam_k1=lam_k1, lam_q2=lam_q2, lam_k2=lam_k2,
                  subln_b=subln_b, w_out_even=w_out_even, ffn_w1=ffn_w1, ffn_w3=ffn_w3, ffn_w2=ffn_w2,
                  ada_w_odd=ada_w_odd, ada_b_odd=ada_b_odd, norm_mix_odd=norm_mix_odd, norm_ffn_odd=norm_ffn_odd,
                  w_qkv_odd=w_qkv_odd, qnorm_c=qnorm_c, knorm_c=knorm_c, rpb_c=rpb_c, w_out_odd=w_out_odd,
                  router_w=router_w, router_b=router_b, moe_w1=moe_w1, moe_w3=moe_w3, moe_w2=moe_w2)
    seq = x_prompt.shape[1]
    ev, od, shared = _prepare(params, seq)
    y_prompt = _trunk(x_prompt, c_prompt, ev, od, shared)
    y_sample = _trunk(x_sample, c_sample, ev, od, shared)
    return (y_prompt, y_sample)
```

```python
import functools
import math

import numpy as np
import jax
import jax.numpy as jnp
from jax import lax
from jax.experimental import pallas as pl
from jax.experimental.pallas import tpu as pltpu
from jax.experimental.pallas import tpu_sc as plsc

F32 = jnp.float32
BF16 = jnp.bfloat16

D_MODEL = 1024
HEAD_DIM = 64
LANES = 128
SCALE = HEAD_DIM ** -0.5
LOG2E = 1.4426950408889634
GRID_W = 64
EPS = 1e-6
ROPE_THETA = 10000.0
A_Q_HEADS = 8
A_KV_HEADS = 2
B_HEADS = 4
C_HEADS = 16
WIN_H = 8
WIN_W = 16
N_EXPERTS = 8
D_FF = 2816
D_FF_EXPERT = 3584
VMEM_LIMIT = 56 * 1024 * 1024

NAT_QROWS = 4
NAT_KROWS = 12
NAT_TQ = NAT_QROWS * GRID_W
NAT_TK = NAT_KROWS * GRID_W
PACK_W = D_MODEL // 2
SC_WINDOW = 128
SC_ROW_WORDS = 256
SC_PIECES = PACK_W // SC_ROW_WORDS


def _cparams(sem):
    return pltpu.CompilerParams(dimension_semantics=sem, vmem_limit_bytes=VMEM_LIMIT)


def _norm_mod(x, g, shift, scale):
    ms = jnp.mean(x * x, axis=-1, keepdims=True)
    y = x * lax.rsqrt(ms + EPS) * g
    return y * (1.0 + scale) + shift


def _head_norm(x, gain, gmat):
    ms = jnp.dot((x * x).astype(BF16), gmat, preferred_element_type=F32)
    return x * lax.rsqrt(ms + EPS) * gain


def _rope(x, cos, sn, first_quarter):
    up = pltpu.roll(x, LANES - 16, 1)
    down = pltpu.roll(x, 16, 1)
    return x * cos + sn * jnp.where(first_quarter, -up, down)


def _sigmoid(a):
    return 1.0 / (1.0 + jnp.exp(-a))


def _pack_halves(a, b):
    hi = lax.bitcast_convert_type(a.astype(BF16).astype(F32), jnp.int32)
    lo = lax.bitcast_convert_type(b.astype(BF16).astype(F32), jnp.int32)
    return hi | lax.shift_right_logical(lo, jnp.full_like(lo, 16))


def _unpack_halves(w):
    hi = lax.bitcast_convert_type(w & jnp.int32(-65536), F32)
    lo = lax.bitcast_convert_type(lax.shift_left(w, jnp.full_like(w, 16)), F32)
    return jnp.concatenate([hi, lo], axis=1)


def _store_pieces(ref, words):
    for s in range(SC_PIECES):
        ref[s] = words[:, s * SC_ROW_WORDS:(s + 1) * SC_ROW_WORDS]


def _load_pieces(ref):
    return jnp.concatenate([ref[s] for s in range(SC_PIECES)], axis=1)


def _ada_kernel(c_ref, w_ref, b_ref, o_ref):
    c = c_ref[...]
    s = c * _sigmoid(c)
    o_ref[...] = jnp.dot(s, w_ref[...], preferred_element_type=F32,
                         precision=lax.Precision.HIGHEST) + b_ref[...]


def _ada_modulation(c, w, b):
    nb, d = c.shape
    n = w.shape[1]
    tn = 512
    mod = pl.pallas_call(
        _ada_kernel,
        grid=(n // tn,),
        in_specs=[pl.BlockSpec((nb, d), lambda j: (0, 0)),
                  pl.BlockSpec((d, tn), lambda j: (0, j)),
                  pl.BlockSpec((1, tn), lambda j: (0, j))],
        out_specs=pl.BlockSpec((nb, tn), lambda j: (0, j)),
        out_shape=jax.ShapeDtypeStruct((nb, n), F32),
        compiler_params=_cparams(("arbitrary",)),
        name="ada_mod",
    )(c, w, b.reshape(1, n))
    return mod.reshape(nb, 6, 1, d)


def _mod_spec(k, tm, seq):
    return pl.BlockSpec((None, None, 1, D_MODEL), lambda i, *_: ((i * tm) // seq, k, 0, 0))


def _proj_kernel(x_ref, sh_ref, sc_ref, g_ref, w_ref, cos_ref, sn_ref, gains_ref, gmat_ref, *o_refs,
                 groups):
    h = _norm_mod(x_ref[...], g_ref[...], sh_ref[...], sc_ref[...])
    y = jnp.dot(h.astype(BF16), w_ref[...], preferred_element_type=F32)
    lane = lax.broadcasted_iota(jnp.int32, (1, LANES), 1)
    first_quarter = (lane % 32) < 16
    gmat = gmat_ref[...]
    off = 0
    for o_ref, (width, gain_row, rope, mult) in zip(o_refs, groups):
        if gain_row is None:
            o_ref[...] = y[:, off:off + width].astype(o_ref.dtype)
        else:
            gain = gains_ref[gain_row:gain_row + 1, :]
            for t in range(width // LANES):
                z = _head_norm(y[:, off + t * LANES: off + (t + 1) * LANES], gain, gmat)
                if rope:
                    z = _rope(z, cos_ref[...], sn_ref[...], first_quarter)
                if mult != 1.0:
                    z = z * mult
                o_ref[:, t * LANES:(t + 1) * LANES] = z.astype(o_ref.dtype)
        off += width


def _projection(x2, mod, norm_g, w, cos_t, sn_t, gains, gmat, groups, seq, tm=512):
    t = x2.shape[0]
    n = w.shape[1]
    nseq = seq // tm
    row = lambda i: (i, 0)
    const = lambda i: (0, 0)
    tab = lambda i: (i % nseq, 0)
    return pl.pallas_call(
        functools.partial(_proj_kernel, groups=groups),
        grid=(t // tm,),
        in_specs=[pl.BlockSpec((tm, D_MODEL), row),
                  _mod_spec(0, tm, seq), _mod_spec(1, tm, seq),
                  pl.BlockSpec((1, D_MODEL), const),
                  pl.BlockSpec((D_MODEL, n), const),
                  pl.BlockSpec((tm, LANES), tab), pl.BlockSpec((tm, LANES), tab),
                  pl.BlockSpec(gains.shape, const),
                  pl.BlockSpec((LANES, LANES), const)],
        out_specs=[pl.BlockSpec((tm, g[0]), row) for g in groups],
        out_shape=[jax.ShapeDtypeStruct((t, g[0]), BF16) for g in groups],
        compiler_params=_cparams(("arbitrary",)),
        name="norm_mod_proj",
    )(x2, mod, mod, norm_g.reshape(1, D_MODEL), w, cos_t, sn_t, gains, gmat)


def _flash_kernel(slope_ref, q_ref, k_ref, v_ref, lam_ref, subg_ref, o_ref,
                  q_sc, v_sc, m_sc, acc_sc, s0_sc, s1_sc, p0_sc, p1_sc, a0_sc, a1_sc,
                  *, tq, tk, seq, nstack, alibi, lam_init):
    g = pl.program_id(1)
    lane = lax.broadcasted_iota(jnp.int32, (1, LANES), 1)
    low_half = lane < HEAD_DIM
    s_bufs, p_bufs, a_bufs = (s0_sc, s1_sc), (p0_sc, p1_sc), (a0_sc, a1_sc)
    nchunks = seq // tk
    ntiles = seq // tq
    nrows = nstack * tq

    v_sc[:, 0:LANES] = v_ref[...]
    v_sc[:, LANES:2 * LANES] = jnp.ones((seq, LANES), BF16)

    if alibi:
        rc = (lax.broadcasted_iota(jnp.int32, (tq, tk), 0)
              - lax.broadcasted_iota(jnp.int32, (tq, tk), 1)).astype(F32)
        neg_slope = -slope_ref[g]
        lp = lam_ref[...]
        l1 = jnp.sum(lp[0:1, :] * lp[1:2, :], axis=-1, keepdims=True)
        l2 = jnp.sum(lp[2:3, :] * lp[3:4, :], axis=-1, keepdims=True)
        lam = jnp.exp(l1) - jnp.exp(l2) + lam_init

    def tile_rows(t):
        return pl.ds(t * tq, tq) if isinstance(t, int) else pl.ds(pl.multiple_of(t * tq, tq), tq)

    def chunk_rows(c):
        return pl.ds(c * tk, tk) if isinstance(c, int) else pl.ds(pl.multiple_of(c * tk, tk), tk)

    def load_queries(t, slot):
        for u in range(nstack):
            src = q_ref[tile_rows(t), (u // 2) * LANES:(u // 2 + 1) * LANES]
            keep = low_half if u % 2 == 0 else jnp.logical_not(low_half)
            q_sc[slot, u * tq:(u + 1) * tq, :] = jnp.where(keep, src, jnp.zeros_like(src))

    def scores(t, c, slot, par):
        s = lax.dot_general(q_sc[slot], k_ref[chunk_rows(c), :], (((1,), (1,)), ((), ())),
                            preferred_element_type=F32)
        if alibi:
            base = (t * tq - c * tk).astype(F32) if not (isinstance(t, int) and isinstance(c, int)) \
                else float(t * tq - c * tk)
            bias = neg_slope * jnp.abs(rc + base)
            s = s + jnp.concatenate([bias] * nstack, axis=0)
        s_bufs[par][...] = s

    def softmax(slot, par, first):
        s = s_bufs[par][...]
        m_cur = jnp.max(s, axis=-1, keepdims=True)
        if first:
            m_new = jnp.broadcast_to(m_cur, (nrows, LANES))
        else:
            m_old = m_sc[slot]
            m_new = jnp.maximum(m_old, m_cur)
            a_bufs[par][...] = jnp.exp2(m_old - m_new)
        p_bufs[par][...] = jnp.exp2(s - pltpu.repeat(m_new, tk // LANES, 1)).astype(BF16)
        m_sc[slot] = m_new

    def values(c, slot, par, first):
        d = jnp.dot(p_bufs[par][...], v_sc[chunk_rows(c), :], preferred_element_type=F32)
        if first:
            acc_sc[slot] = d
        else:
            acc_sc[slot] = pltpu.repeat(a_bufs[par][...], 2, 1) * acc_sc[slot] + d

    def finalize(t, slot):
        acc = acc_sc[slot]
        o = acc[:, 0:LANES] * (1.0 / acc[:, LANES:2 * LANES])
        if alibi:
            ob = o[0:tq, :] - lam * o[tq:2 * tq, :]
            ms = jnp.mean(ob * ob, axis=-1, keepdims=True)
            ob = ob * lax.rsqrt(ms + EPS) * subg_ref[...] * (1.0 - lam_init)
            o_ref[tile_rows(t), :] = ob.astype(o_ref.dtype)
        else:
            for pair in range(nstack // 2):
                lo = o[(2 * pair) * tq:(2 * pair + 1) * tq, :]
                hi = o[(2 * pair + 1) * tq:(2 * pair + 2) * tq, :]
                o_ref[tile_rows(t), pair * LANES:(pair + 1) * LANES] = (
                    jnp.where(low_half, lo, hi).astype(o_ref.dtype))

    def step(t, t_next, slot, c):
        static = isinstance(c, int)
        par = c % 2 if static else None
        ahead2 = c + 2
        if static and ahead2 >= nchunks:
            scores(t_next, ahead2 - nchunks, 1 - slot, par)
        else:
            scores(t, ahead2, slot, par)
        if static and c + 1 >= nchunks:
            softmax(1 - slot, 1 - par, first=True)
        else:
            softmax(slot, 1 - par, first=False)
        values(c, slot, par, first=static and c == 0)

    load_queries(0, 0)
    scores(0, 0, 0, 0)
    scores(0, 1, 0, 1)
    softmax(0, 0, first=True)

    def tile_body(t, carry):
        slot = t % 2
        t_next = (t + 1) % ntiles
        load_queries(t_next, 1 - slot)
        lead = min(2, nchunks - 2)
        for c in range(lead):
            step(t, t_next, slot, c)

        def pair_body(j, inner):
            for par in range(2):
                c = 2 * j + par
                scores(t, c + 2, slot, par)
                softmax(slot, 1 - par, first=False)
                values(c, slot, par, first=False)
            return inner

        lax.fori_loop(lead // 2, (nchunks - 2) // 2, pair_body, 0)
        for c in range(nchunks - 2, nchunks):
            step(t, t_next, slot, c)
        finalize(t, slot)
        return carry

    lax.fori_loop(0, ntiles, tile_body, 0)


def _flash_attention(q, k, v, slopes, lam_pack, subg, *, nbatch, seq, ngroups, nstack, alibi, lam_init,
                     nrows=1024):
    qw = (nstack // 2) * LANES
    tq = nrows // nstack
    tk = min(1024, seq // 2)
    kernel = functools.partial(_flash_kernel, tq=tq, tk=tk, seq=seq, nstack=nstack, alibi=alibi,
                               lam_init=lam_init)
    grid_spec = pltpu.PrefetchScalarGridSpec(
        num_scalar_prefetch=1,
        grid=(nbatch, ngroups),
        in_specs=[pl.BlockSpec((None, seq, qw), lambda b, g, s: (b, 0, g)),
                  pl.BlockSpec((None, seq, LANES), lambda b, g, s: (b, 0, g)),
                  pl.BlockSpec((None, seq, LANES), lambda b, g, s: (b, 0, g)),
                  pl.BlockSpec(lam_pack.shape, lambda b, g, s: (0, 0)),
                  pl.BlockSpec(subg.shape, lambda b, g, s: (0, 0))],
        out_specs=pl.BlockSpec((None, seq, qw), lambda b, g, s: (b, 0, g)),
        scratch_shapes=[pltpu.VMEM((2, nrows, LANES), BF16),
                        pltpu.VMEM((seq, 2 * LANES), BF16),
                        pltpu.VMEM((2, nrows, LANES), F32),
                        pltpu.VMEM((2, nrows, 2 * LANES), F32),
                        pltpu.VMEM((nrows, tk), F32), pltpu.VMEM((nrows, tk), F32),
                        pltpu.VMEM((nrows, tk), BF16), pltpu.VMEM((nrows, tk), BF16),
                        pltpu.VMEM((nrows, LANES), F32), pltpu.VMEM((nrows, LANES), F32)])
    return pl.pallas_call(
        kernel,
        grid_spec=grid_spec,
        out_shape=jax.ShapeDtypeStruct((nbatch, seq, ngroups * qw), BF16),
        compiler_params=_cparams(("arbitrary", "arbitrary")),
        name="flash_alibi" if alibi else "flash_gqa",
    )(slopes, q, k, v, lam_pack, subg)


def _natten_kernel(q_ref, k_ref, v_ref, bias_ref, o_ref, v_sc, s0_sc, s1_sc, *, seq):
    lane = lax.broadcasted_iota(jnp.int32, (1, LANES), 1)
    low_half = lane < HEAD_DIM
    ntiles = seq // NAT_TQ
    rows = seq // GRID_W
    s_bufs = (s0_sc, s1_sc)

    v_sc[:, 0:LANES] = v_ref[...]
    v_sc[:, LANES:2 * LANES] = jnp.ones((seq, LANES), BF16)

    def window(t):
        w0 = jnp.clip(t * NAT_QROWS - WIN_H // 2, 0, rows - NAT_KROWS)
        return pl.multiple_of(w0 * GRID_W, NAT_TQ)

    def scores(t, par):
        q = q_ref[pl.ds(pl.multiple_of(t * NAT_TQ, NAT_TQ), NAT_TQ), :]
        zero = jnp.zeros_like(q)
        q2 = jnp.concatenate([jnp.where(low_half, q, zero), jnp.where(low_half, zero, q)], axis=0)
        s = lax.dot_general(q2, k_ref[pl.ds(window(t), NAT_TK), :], (((1,), (1,)), ((), ())),
                            preferred_element_type=F32)
        cls = jnp.where(t == 0, 0, jnp.where(t == ntiles - 1, 2, 1))
        bias = bias_ref[cls].astype(F32).reshape(2 * NAT_TQ, NAT_TK)
        s_bufs[par][...] = s + bias

    def finish(t, par):
        s = s_bufs[par][...]
        m = jnp.max(s, axis=-1, keepdims=True)
        p = jnp.exp2(s - m).astype(BF16)
        acc = jnp.dot(p, v_sc[pl.ds(window(t), NAT_TK), :], preferred_element_type=F32)
        o = acc[:, 0:LANES] * (1.0 / acc[:, LANES:2 * LANES])
        out = jnp.where(low_half, o[0:NAT_TQ, :], o[NAT_TQ:, :])
        o_ref[pl.ds(pl.multiple_of(t * NAT_TQ, NAT_TQ), NAT_TQ), :] = out.astype(o_ref.dtype)

    scores(jnp.int32(0), 0)

    def pair_body(u, carry):
        t = 2 * u
        scores(t + 1, 1)
        finish(t, 0)
        scores((t + 2) % ntiles, 0)
        finish(t + 1, 1)
        return carry

    lax.fori_loop(0, ntiles // 2, pair_body, 0)


def _natten(q, k, v, bias_tab, *, nbatch, seq):
    npairs = C_HEADS // 2
    blk = pl.BlockSpec((None, seq, LANES), lambda p, b: (b, 0, p))
    return pl.pallas_call(
        functools.partial(_natten_kernel, seq=seq),
        grid=(npairs, nbatch),
        in_specs=[blk, blk, blk,
                  pl.BlockSpec((3, None, 2, NAT_TQ, NAT_TK), lambda p, b: (0, p, 0, 0, 0))],
        out_specs=blk,
        out_shape=jax.ShapeDtypeStruct((nbatch, seq, C_HEADS * HEAD_DIM), BF16),
        scratch_shapes=[pltpu.VMEM((seq, 2 * LANES), BF16),
                        pltpu.VMEM((2 * NAT_TQ, NAT_TK), F32), pltpu.VMEM((2 * NAT_TQ, NAT_TK), F32)],
        compiler_params=_cparams(("arbitrary", "arbitrary")),
        name="natten",
    )(q, k, v, bias_tab)


def _natten_bias_table(rpb, seq):
    rows = seq // GRID_W
    ntiles = rows // NAT_QROWS
    col = jnp.arange(GRID_W)
    cstart = jnp.clip(col - WIN_W // 2, 0, GRID_W - WIN_W)
    col_valid = (col[None, :] >= cstart[:, None]) & (col[None, :] < cstart[:, None] + WIN_W)
    dc_idx = jnp.clip(col[None, :] - col[:, None] + WIN_W - 1, 0, 2 * WIN_W - 2)
    rpb_cols = rpb[:, :, dc_idx]
    tabs = []
    for tile in (0, 1, ntiles - 1):
        r = tile * NAT_QROWS + jnp.arange(NAT_QROWS)
        w0 = int(np.clip(tile * NAT_QROWS - WIN_H // 2, 0, rows - NAT_KROWS))
        kr = w0 + jnp.arange(NAT_KROWS)
        rstart = jnp.clip(r - WIN_H // 2, 0, rows - WIN_H)
        row_valid = (kr[None, :] >= rstart[:, None]) & (kr[None, :] < rstart[:, None] + WIN_H)
        dr_idx = jnp.clip(kr[None, :] - r[:, None] + WIN_H - 1, 0, 2 * WIN_H - 2)
        pick = (dr_idx[:, :, None] == jnp.arange(2 * WIN_H - 1)[None, None, :]).astype(F32)
        b = jnp.einsum("qkd,hdcx->hqckx", pick, rpb_cols * LOG2E,
                       precision=lax.Precision.HIGHEST)
        valid = row_valid[:, None, :, None] & col_valid[None, :, None, :]
        b = jnp.where(valid[None], b, -jnp.inf)
        tabs.append(b.reshape(C_HEADS, NAT_TQ, NAT_TK))
    return jnp.stack(tabs).reshape(3, C_HEADS // 2, 2, NAT_TQ, NAT_TK).astype(BF16)


def _outproj_kernel(x_ref, gate_ref, *rest, nmix):
    mix_refs, w_refs, o_ref = rest[:nmix], rest[nmix:2 * nmix], rest[2 * nmix]
    y = None
    for m_ref, w_ref in zip(mix_refs, w_refs):
        d = jnp.dot(m_ref[...], w_ref[...], preferred_element_type=F32)
        y = d if y is None else y + d
    o_ref[...] = x_ref[...] + gate_ref[...] * y


def _outproj(x2, mod, mixes, ws, seq, tm=512):
    t = x2.shape[0]
    row = lambda i: (i, 0)
    const = lambda i: (0, 0)
    return pl.pallas_call(
        functools.partial(_outproj_kernel, nmix=len(mixes)),
        grid=(t // tm,),
        in_specs=[pl.BlockSpec((tm, D_MODEL), row), _mod_spec(2, tm, seq)]
                 + [pl.BlockSpec((tm, m.shape[1]), row) for m in mixes]
                 + [pl.BlockSpec(w.shape, const) for w in ws],
        out_specs=pl.BlockSpec((tm, D_MODEL), row),
        out_shape=jax.ShapeDtypeStruct((t, D_MODEL), F32),
        compiler_params=_cparams(("arbitrary",)),
        name="outproj_residual",
    )(x2, mod, *mixes, *ws)


def _ffn_kernel(x_ref, sh_ref, sc_ref, gate_ref, g_ref, w1_ref, w3_ref, w2_ref, o_ref, h_sc, acc_sc, *, nf):
    f = pl.program_id(1)

    @pl.when(f == 0)
    def _():
        h_sc[...] = _norm_mod(x_ref[...], g_ref[...], sh_ref[...], sc_ref[...]).astype(BF16)

    h = h_sc[...]
    a = jnp.dot(h, w1_ref[...], preferred_element_type=F32)
    b = jnp.dot(h, w3_ref[...], preferred_element_type=F32)
    y = jnp.dot((a * _sigmoid(a) * b).astype(BF16), w2_ref[...], preferred_element_type=F32)

    @pl.when(f == 0)
    def _():
        acc_sc[...] = y

    @pl.when(f > 0)
    def _():
        acc_sc[...] += y

    @pl.when(f == nf - 1)
    def _():
        o_ref[...] = x_ref[...] + gate_ref[...] * acc_sc[...]


def _ffn(x2, mod, norm_g, w1, w3, w2, seq, tm=512, tf=1408):
    t = x2.shape[0]
    nf = w1.shape[1] // tf
    row = lambda i, f: (i, 0)
    return pl.pallas_call(
        functools.partial(_ffn_kernel, nf=nf),
        grid=(t // tm, nf),
        in_specs=[pl.BlockSpec((tm, D_MODEL), row),
                  _mod_spec(3, tm, seq), _mod_spec(4, tm, seq), _mod_spec(5, tm, seq),
                  pl.BlockSpec((1, D_MODEL), lambda i, f: (0, 0)),
                  pl.BlockSpec((D_MODEL, tf), lambda i, f: (0, f)),
                  pl.BlockSpec((D_MODEL, tf), lambda i, f: (0, f)),
                  pl.BlockSpec((tf, D_MODEL), lambda i, f: (f, 0))],
        out_specs=pl.BlockSpec((tm, D_MODEL), row),
        out_shape=jax.ShapeDtypeStruct((t, D_MODEL), F32),
        scratch_shapes=[pltpu.VMEM((tm, D_MODEL), BF16), pltpu.VMEM((tm, D_MODEL), F32)],
        compiler_params=_cparams(("arbitrary", "arbitrary")),
        name="ffn_swiglu",
    )(x2, mod, mod, mod, norm_g.reshape(1, D_MODEL), w1, w3, w2)


def _router_kernel(x_ref, sh_ref, sc_ref, g_ref, rw_ref, rb_ref, h_ref, meta_ref):
    h = _norm_mod(x_ref[...], g_ref[...], sh_ref[...], sc_ref[...])
    _store_pieces(h_ref, _pack_halves(h[:, 0:PACK_W], h[:, PACK_W:D_MODEL]))
    lane = lax.broadcasted_iota(jnp.int32, (1, LANES), 1).astype(F32)
    logits = jnp.dot(h, rw_ref[...], preferred_element_type=F32,
                     precision=lax.Precision.HIGHEST) + rb_ref[...]
    logits = jnp.where(lane < N_EXPERTS, logits, -jnp.inf)
    m1 = jnp.max(logits, axis=-1, keepdims=True)
    i1 = jnp.min(jnp.where(logits == m1, lane, float(LANES)), axis=-1, keepdims=True)
    rest = jnp.where(lane == i1, -jnp.inf, logits)
    m2 = jnp.max(rest, axis=-1, keepdims=True)
    i2 = jnp.min(jnp.where(rest == m2, lane, float(LANES)), axis=-1, keepdims=True)
    e = jnp.exp(m2 - m1)
    g1 = 1.0 / (1.0 + e)
    g2 = e * g1
    meta = jnp.where(lane == 0, i1, jnp.where(lane == 1, i2, jnp.where(lane == 2, g1,
                                                                       jnp.where(lane == 3, g2, 0.0))))
    meta_ref[...] = meta


def _router(x2, mod, norm_g, rw_pad, rb_pad, seq, tm=512):
    t = x2.shape[0]
    row = lambda i: (i, 0)
    const = lambda i: (0, 0)
    return pl.pallas_call(
        _router_kernel,
        grid=(t // tm,),
        in_specs=[pl.BlockSpec((tm, D_MODEL), row), _mod_spec(3, tm, seq), _mod_spec(4, tm, seq),
                  pl.BlockSpec((1, D_MODEL), const),
                  pl.BlockSpec((D_MODEL, LANES), const), pl.BlockSpec((1, LANES), const)],
        out_specs=[pl.BlockSpec((SC_PIECES, tm, SC_ROW_WORDS), lambda i: (0, i, 0)),
                   pl.BlockSpec((tm, LANES), row)],
        out_shape=[jax.ShapeDtypeStruct((SC_PIECES, t, SC_ROW_WORDS), jnp.int32),
                   jax.ShapeDtypeStruct((t, LANES), F32)],
        compiler_params=_cparams(("arbitrary",)),
        name="moe_router",
    )(x2, mod, mod, norm_g.reshape(1, D_MODEL), rw_pad, rb_pad)


def _moe_kernel(te_ref, nt_ref, x_ref, w1_ref, w3_ref, w2a_ref, w2b_ref, o_ref, x_sc, g_sc, *, nf, tf):
    i = pl.program_id(0)
    j = pl.program_id(1)
    active = i < nt_ref[0]

    @pl.when(jnp.logical_and(active, j == 0))
    def _():
        x_sc[...] = _unpack_halves(_load_pieces(x_ref)).astype(BF16)

    @pl.when(jnp.logical_and(active, j < nf))
    def _():
        x = x_sc[...]
        a = jnp.dot(x, w1_ref[...], preferred_element_type=F32)
        b = jnp.dot(x, w3_ref[...], preferred_element_type=F32)
        g_sc[j] = (a * _sigmoid(a) * b).astype(BF16)

    @pl.when(jnp.logical_and(active, j >= nf))
    def _():
        ya, yb = None, None
        for f in range(nf):
            g = g_sc[f]
            da = jnp.dot(g, w2a_ref[f * tf:(f + 1) * tf, :], preferred_element_type=F32)
            db = jnp.dot(g, w2b_ref[f * tf:(f + 1) * tf, :], preferred_element_type=F32)
            ya = da if ya is None else ya + da
            yb = db if yb is None else yb + db
        o_ref[...] = _pack_halves(ya, yb)

    @pl.when(jnp.logical_and(jnp.logical_not(active), j >= nf))
    def _():
        o_ref[...] = jnp.zeros(o_ref.shape, o_ref.dtype)


def _moe_grouped(xs, tile_expert, num_tiles, w1, w3, w2, tg, tf=512):
    p = xs.shape[1]
    nf = w1.shape[2] // tf
    tn = SC_ROW_WORDS
    nb = SC_PIECES
    fcl = lambda j: jnp.minimum(j, nf - 1)
    ncl = lambda j: jnp.maximum(j - nf, 0)
    grid_spec = pltpu.PrefetchScalarGridSpec(
        num_scalar_prefetch=2,
        grid=(p // tg, nf + nb),
        in_specs=[pl.BlockSpec((SC_PIECES, tg, SC_ROW_WORDS), lambda i, j, te, nt: (0, i, 0)),
                  pl.BlockSpec((None, D_MODEL, tf), lambda i, j, te, nt: (te[i], 0, fcl(j))),
                  pl.BlockSpec((None, D_MODEL, tf), lambda i, j, te, nt: (te[i], 0, fcl(j))),
                  pl.BlockSpec((None, nf * tf, tn), lambda i, j, te, nt: (te[i], 0, ncl(j))),
                  pl.BlockSpec((None, nf * tf, tn), lambda i, j, te, nt: (te[i], 0, nb + ncl(j)))],
        out_specs=pl.BlockSpec((None, tg, tn), lambda i, j, te, nt: (ncl(j), i, 0)),
        scratch_shapes=[pltpu.VMEM((tg, D_MODEL), BF16), pltpu.VMEM((nf, tg, tf), BF16)])
    return pl.pallas_call(
        functools.partial(_moe_kernel, nf=nf, tf=tf),
        grid_spec=grid_spec,
        out_shape=jax.ShapeDtypeStruct((SC_PIECES, p, SC_ROW_WORDS), jnp.int32),
        compiler_params=_cparams(("arbitrary", "arbitrary")),
        name="moe_grouped",
    )(tile_expert, num_tiles, xs, w1, w3, w2, w2)


def _combine_kernel(x_ref, gate_ref, meta_ref, y1_ref, y2_ref, o_ref):
    meta = meta_ref[...]
    moe = (meta[:, 2:3] * _unpack_halves(_load_pieces(y1_ref))
           + meta[:, 3:4] * _unpack_halves(_load_pieces(y2_ref)))
    o_ref[...] = x_ref[...] + gate_ref[...] * moe


def _combine(x2, mod, meta, yg, seq, tm=512):
    t = x2.shape[0]
    nt = t // tm
    row = lambda i: (i, 0)
    return pl.pallas_call(
        _combine_kernel,
        grid=(nt,),
        in_specs=[pl.BlockSpec((tm, D_MODEL), row), _mod_spec(5, tm, seq),
                  pl.BlockSpec((tm, LANES), row),
                  pl.BlockSpec((SC_PIECES, tm, SC_ROW_WORDS), lambda i: (0, i, 0)),
                  pl.BlockSpec((SC_PIECES, tm, SC_ROW_WORDS), lambda i: (0, nt + i, 0))],
        out_specs=pl.BlockSpec((tm, D_MODEL), row),
        out_shape=jax.ShapeDtypeStruct((t, D_MODEL), F32),
        compiler_params=_cparams(("arbitrary",)),
        name="moe_combine",
    )(x2, mod, meta, yg, yg)


def _sc_gather(table, idx):
    n = idx.shape[0]
    nrows = table.shape[1]
    pieces = n * SC_PIECES
    idx_pieces = jnp.concatenate([idx + s * nrows for s in range(SC_PIECES)]).reshape(1, pieces)
    mesh = plsc.VectorSubcoreMesh(core_axis_name="core", subcore_axis_name="subcore")

    @pl.kernel(out_type=jax.ShapeDtypeStruct((pieces, SC_ROW_WORDS), table.dtype), mesh=mesh, scratch_types=[])
    def gather_kernel(table_hbm, idx_hbm, out_hbm):
        def body(idx_vmem, out_vmem):
            pltpu.sync_copy(table_hbm.at[idx_vmem.at[0]], out_vmem)

        pltpu.emit_pipeline(
            body,
            grid=(pieces // SC_WINDOW,),
            in_specs=[pl.BlockSpec((1, SC_WINDOW), lambda i: (0, i))],
            out_specs=[pl.BlockSpec((SC_WINDOW, SC_ROW_WORDS), lambda i: (i, 0))],
            core_axis_name=("core", "subcore"),
            dimension_semantics=(pltpu.PARALLEL,),
        )(idx_hbm, out_hbm)

    out = gather_kernel(table.reshape(SC_PIECES * nrows, SC_ROW_WORDS), idx_pieces)
    return out.reshape(SC_PIECES, n, SC_ROW_WORDS)


def _moe(x2, mod, norm_g, rw_pad, rb_pad, w1, w3, w2, seq, tg=1024):
    t = x2.shape[0]
    h, meta = _router(x2, mod, norm_g, rw_pad, rb_pad, seq)
    e_flat = meta[:, 0:2].astype(jnp.int32).reshape(-1)
    onehot = (e_flat[:, None] == jnp.arange(N_EXPERTS)[None, :]).astype(jnp.int32)
    csum = jnp.cumsum(onehot, axis=0)
    counts = csum[-1]
    rank = jnp.take_along_axis(csum, e_flat[:, None], axis=1)[:, 0] - 1
    padded = ((counts + tg - 1) // tg) * tg
    pend = jnp.cumsum(padded)
    pos = (pend - padded)[e_flat] + rank
    p_rows = 2 * t + N_EXPERTS * tg
    row_token = jnp.zeros((p_rows,), jnp.int32).at[pos].set(jnp.arange(2 * t, dtype=jnp.int32) // 2)
    tile_start = jnp.arange(p_rows // tg, dtype=jnp.int32) * tg
    tile_expert = jnp.minimum(jnp.sum((tile_start[:, None] >= pend[None, :]).astype(jnp.int32), axis=1),
                              N_EXPERTS - 1)
    num_tiles = (pend[-1] // tg).astype(jnp.int32).reshape(1)
    xs = _sc_gather(h, row_token)
    ys = _moe_grouped(xs, tile_expert, num_tiles, w1, w3, w2, tg)
    yg = _sc_gather(ys, jnp.concatenate([pos[0::2], pos[1::2]]))
    return _combine(x2, mod, meta, yg, seq)


def _rope_tables(seq):
    t = np.arange(seq)
    lane = np.arange(LANES)
    d = lane % HEAD_DIM
    pos = np.where((d // 32)[None, :] == 0, (t // GRID_W)[:, None], (t % GRID_W)[:, None]).astype(np.float32)
    inv = (ROPE_THETA ** (-np.arange(16, dtype=np.float32) / 16)).astype(np.float32)
    ang = pos * inv[(d % 16)][None, :]
    return jnp.asarray(np.cos(ang), F32), jnp.asarray(np.sin(ang), F32)


def _pair_gain(g):
    return jnp.concatenate([g, g]).astype(F32)


def _prepare(p, seq):
    even, odd = {}, {}
    w_in = p["w_in_even"][0]
    qa, ka, va, qb, kb, vb = jnp.split(w_in, [512, 640, 768, 1280, 1792], axis=1)
    dup = lambda w: jnp.concatenate([w[:, 0:64], w[:, 0:64], w[:, 64:128], w[:, 64:128]], axis=1)
    even["w_in"] = jnp.concatenate([qa, qb, dup(ka), dup(va), kb, vb], axis=1).astype(BF16)
    gains = jnp.zeros((8, LANES), F32)
    gains = gains.at[0].set(_pair_gain(p["qnorm_a"][0])).at[1].set(_pair_gain(p["knorm_a"][0]))
    gains = gains.at[2].set(_pair_gain(p["qnorm_b"][0])).at[3].set(_pair_gain(p["knorm_b"][0]))
    even["gains"] = gains
    qscale = SCALE * LOG2E
    even["groups"] = ((512, 0, True, qscale), (512, 2, False, qscale), (256, 1, True, 1.0),
                      (256, None, False, 1.0), (512, 3, False, 1.0), (512, None, False, 1.0))
    lam = jnp.zeros((8, LANES), F32)
    for r, name in enumerate(("lam_q1", "lam_k1", "lam_q2", "lam_k2")):
        lam = lam.at[r, 0:HEAD_DIM].set(p[name][0])
    even["lam"] = lam
    even["subg"] = p["subln_b"][0].reshape(1, LANES).astype(F32)
    wo = p["w_out_even"][0].astype(BF16)
    even["wo"] = (wo[0:512], wo[512:1024])
    even["slopes"] = jnp.asarray(LOG2E * 2.0 ** (-8.0 * (np.arange(B_HEADS) + 1.0) / B_HEADS), F32)
    for name in ("ffn_w1", "ffn_w3", "ffn_w2"):
        even[name] = p[name][0].astype(BF16)
    for name in ("ada_w", "ada_b", "norm_mix", "norm_ffn"):
        even[name] = p[name + "_even"][0]
        odd[name] = p[name + "_odd"][0]

    odd["w_qkv"] = p["w_qkv_odd"][0].astype(BF16)
    gains = jnp.zeros((8, LANES), F32)
    odd["gains"] = gains.at[0].set(_pair_gain(p["qnorm_c"][0])).at[1].set(_pair_gain(p["knorm_c"][0]))
    odd["groups"] = ((1024, 0, False, qscale), (1024, 1, False, 1.0), (1024, None, False, 1.0))
    odd["bias_tab"] = _natten_bias_table(p["rpb_c"][0], seq)
    odd["wo"] = (p["w_out_odd"][0].astype(BF16),)
    odd["rw"] = jnp.zeros((D_MODEL, LANES), F32).at[:, 0:N_EXPERTS].set(p["router_w"][0])
    odd["rb"] = jnp.zeros((1, LANES), F32).at[0, 0:N_EXPERTS].set(p["router_b"][0])
    for name in ("moe_w1", "moe_w3", "moe_w2"):
        odd[name] = p[name][0].astype(BF16)
    cos_t, sn_t = _rope_tables(seq)
    gmat = jnp.asarray(np.kron(np.eye(2), np.full((HEAD_DIM, HEAD_DIM), 1.0 / HEAD_DIM)), BF16)
    return even, odd, (cos_t, sn_t, gmat)


def _even_layer(x2, c, ev, shared, nbatch, seq):
    cos_t, sn_t, gmat = shared
    mod = _ada_modulation(c, ev["ada_w"], ev["ada_b"])
    qa, qb, ka, va, kb, vb = _projection(x2, mod, ev["norm_mix"], ev["w_in"], cos_t, sn_t, ev["gains"], gmat,
                                         ev["groups"], seq)
    shp = lambda a: a.reshape(nbatch, seq, a.shape[1])
    lam_init = 0.8 - 0.6 * math.exp(-0.3 * 0)
    mix_a = _flash_attention(shp(qa), shp(ka), shp(va), ev["slopes"], ev["lam"], ev["subg"], nbatch=nbatch, seq=seq,
                             ngroups=A_KV_HEADS, nstack=4, alibi=False, lam_init=lam_init)
    mix_b = _flash_attention(shp(qb), shp(kb), shp(vb), ev["slopes"], ev["lam"], ev["subg"], nbatch=nbatch, seq=seq,
                             ngroups=B_HEADS, nstack=2, alibi=True, lam_init=lam_init)
    t = nbatch * seq
    x2 = _outproj(x2, mod, (mix_a.reshape(t, -1), mix_b.reshape(t, -1)), ev["wo"], seq)
    return _ffn(x2, mod, ev["norm_ffn"], ev["ffn_w1"], ev["ffn_w3"], ev["ffn_w2"], seq)


def _odd_layer(x2, c, od, shared, nbatch, seq):
    cos_t, sn_t, gmat = shared
    mod = _ada_modulation(c, od["ada_w"], od["ada_b"])
    q, k, v = _projection(x2, mod, od["norm_mix"], od["w_qkv"], cos_t, sn_t, od["gains"], gmat, od["groups"], seq)
    shp = lambda a: a.reshape(nbatch, seq, a.shape[1])
    mix = _natten(shp(q), shp(k), shp(v), od["bias_tab"], nbatch=nbatch, seq=seq)
    x2 = _outproj(x2, mod, (mix.reshape(nbatch * seq, -1),), od["wo"], seq)
    return _moe(x2, mod, od["norm_ffn"], od["rw"], od["rb"], od["moe_w1"], od["moe_w3"], od["moe_w2"], seq)


def _trunk(x, c, ev, od, shared):
    nbatch, seq, d = x.shape
    x2 = x.reshape(nbatch * seq, d)
    x2 = _even_layer(x2, c, ev, shared, nbatch, seq)
    x2 = _odd_layer(x2, c, od, shared, nbatch, seq)
    return x2.reshape(nbatch, seq, d)


def kernel(x_prompt, x_sample, c_prompt, c_sample, ada_w_even, ada_b_even, norm_mix_even, norm_ffn_even, w_in_even, qnorm_a, knorm_a, qnorm_b, knorm_b, lam_q1, lam_k1, lam_q2, lam_k2, subln_b, w_out_even, ffn_w1, ffn_w3, ffn_w2, ada_w_odd, ada_b_odd, norm_mix_odd, norm_ffn_odd, w_qkv_odd, qnorm_c, knorm_c, rpb_c, w_out_odd, router_w, router_b, moe_w1, moe_w3, moe_w2):
    params = dict(ada_w_even=ada_w_even, ada_b_even=ada_b_even, norm_mix_even=norm_mix_even,
                  norm_ffn_even=norm_ffn_even, w_in_even=w_in_even, qnorm_a=qnorm_a, knorm_a=knorm_a,
                  qnorm_b=qnorm_b, knorm_b=knorm_b, lam_q1=lam_q1, lam_k1=lam_k1, lam_q2=lam_q2, lam_k2=lam_k2,
                  subln_b=subln_b, w_out_even=w_out_even, ffn_w1=ffn_w1, ffn_w3=ffn_w3, ffn_w2=ffn_w2,
                  ada_w_odd=ada_w_odd, ada_b_odd=ada_b_odd, norm_mix_odd=norm_mix_odd, norm_ffn_odd=norm_ffn_odd,
                  w_qkv_odd=w_qkv_odd, qnorm_c=qnorm_c, knorm_c=knorm_c, rpb_c=rpb_c, w_out_odd=w_out_odd,
                  router_w=router_w, router_b=router_b, moe_w1=moe_w1, moe_w3=moe_w3, moe_w2=moe_w2)
    seq = x_prompt.shape[1]
    ev, od, shared = _prepare(params, seq)
    y_prompt = _trunk(x_prompt, c_prompt, ev, od, shared)
    y_sample = _trunk(x_sample, c_sample, ev, od, shared)
    return (y_prompt, y_sample)
```

```python
import functools
import math

import numpy as np
import jax
import jax.numpy as jnp
from jax import lax
from jax.experimental import pallas as pl
from jax.experimental.pallas import tpu as pltpu
from jax.experimental.pallas import tpu_sc as plsc

F32 = jnp.float32
BF16 = jnp.bfloat16

D_MODEL = 1024
HEAD_DIM = 64
LANES = 128
SCALE = HEAD_DIM ** -0.5
LOG2E = 1.4426950408889634
GRID_W = 64
EPS = 1e-6
ROPE_THETA = 10000.0
A_Q_HEADS = 8
A_KV_HEADS = 2
B_HEADS = 4
C_HEADS = 16
WIN_H = 8
WIN_W = 16
N_EXPERTS = 8
D_FF = 2816
D_FF_EXPERT = 3584
VMEM_LIMIT = 56 * 1024 * 1024

NAT_QROWS = 4
NAT_KROWS = 12
NAT_TQ = NAT_QROWS * GRID_W
NAT_TK = NAT_KROWS * GRID_W
PACK_W = D_MODEL // 2
SC_WINDOW = 128
SC_ROW_WORDS = 256
SC_PIECES = PACK_W // SC_ROW_WORDS
FFN_NF = 2
FFN_NB = 2
MOE_NF = 7


def _cparams(sem):
    return pltpu.CompilerParams(dimension_semantics=sem, vmem_limit_bytes=VMEM_LIMIT)


def _norm_mod(x, g, shift, scale):
    ms = jnp.mean(x * x, axis=-1, keepdims=True)
    y = x * lax.rsqrt(ms + EPS) * g
    return y * (1.0 + scale) + shift


def _head_norm(x, gain, gmat):
    ms = jnp.dot((x * x).astype(BF16), gmat, preferred_element_type=F32)
    return x * lax.rsqrt(ms + EPS) * gain


def _rope(x, cos, sn, first_quarter):
    up = pltpu.roll(x, LANES - 16, 1)
    down = pltpu.roll(x, 16, 1)
    return x * cos + sn * jnp.where(first_quarter, -up, down)


def _sigmoid(a):
    return 1.0 / (1.0 + jnp.exp(-a))


def _pack_halves(a, b):
    hi = lax.bitcast_convert_type(a.astype(BF16).astype(F32), jnp.int32)
    lo = lax.bitcast_convert_type(b.astype(BF16).astype(F32), jnp.int32)
    return hi | lax.shift_right_logical(lo, jnp.full_like(lo, 16))


def _unpack_halves(w):
    hi = lax.bitcast_convert_type(w & jnp.int32(-65536), F32)
    lo = lax.bitcast_convert_type(lax.shift_left(w, jnp.full_like(w, 16)), F32)
    return jnp.concatenate([hi, lo], axis=1)


def _store_pieces(ref, words):
    for s in range(SC_PIECES):
        ref[s] = words[:, s * SC_ROW_WORDS:(s + 1) * SC_ROW_WORDS]


def _load_pieces(ref):
    return jnp.concatenate([ref[s] for s in range(SC_PIECES)], axis=1)


def _ada_kernel(c_ref, w_ref, b_ref, o_ref):
    c = c_ref[...]
    s = c * _sigmoid(c)
    o_ref[...] = jnp.dot(s, w_ref[...], preferred_element_type=F32,
                         precision=lax.Precision.HIGHEST) + b_ref[...]


def _ada_modulation(c, w, b):
    nb, d = c.shape
    n = w.shape[1]
    tn = 512
    mod = pl.pallas_call(
        _ada_kernel,
        grid=(n // tn,),
        in_specs=[pl.BlockSpec((nb, d), lambda j: (0, 0)),
                  pl.BlockSpec((d, tn), lambda j: (0, j)),
                  pl.BlockSpec((1, tn), lambda j: (0, j))],
        out_specs=pl.BlockSpec((nb, tn), lambda j: (0, j)),
        out_shape=jax.ShapeDtypeStruct((nb, n), F32),
        compiler_params=_cparams(("arbitrary",)),
        name="ada_mod",
    )(c, w, b.reshape(1, n))
    return mod.reshape(nb, 6, 1, d)


def _mod_spec(k, tm, seq):
    return pl.BlockSpec((None, None, 1, D_MODEL), lambda i, *_: ((i * tm) // seq, k, 0, 0))


def _proj_kernel(x_ref, sh_ref, sc_ref, g_ref, w_ref, cos_ref, sn_ref, gains_ref, gmat_ref, *o_refs,
                 groups):
    h = _norm_mod(x_ref[...], g_ref[...], sh_ref[...], sc_ref[...])
    y = jnp.dot(h.astype(BF16), w_ref[...], preferred_element_type=F32)
    lane = lax.broadcasted_iota(jnp.int32, (1, LANES), 1)
    first_quarter = (lane % 32) < 16
    gmat = gmat_ref[...]
    off = 0
    for o_ref, (width, gain_row, rope, mult) in zip(o_refs, groups):
        if gain_row is None:
            o_ref[...] = y[:, off:off + width].astype(o_ref.dtype)
        else:
            gain = gains_ref[gain_row:gain_row + 1, :]
            for t in range(width // LANES):
                z = _head_norm(y[:, off + t * LANES: off + (t + 1) * LANES], gain, gmat)
                if rope:
                    z = _rope(z, cos_ref[...], sn_ref[...], first_quarter)
                if mult != 1.0:
                    z = z * mult
                o_ref[:, t * LANES:(t + 1) * LANES] = z.astype(o_ref.dtype)
        off += width


def _projection(x2, mod, norm_g, w, cos_t, sn_t, gains, gmat, groups, seq, tm=512):
    t = x2.shape[0]
    n = w.shape[1]
    nseq = seq // tm
    row = lambda i: (i, 0)
    const = lambda i: (0, 0)
    tab = lambda i: (i % nseq, 0)
    return pl.pallas_call(
        functools.partial(_proj_kernel, groups=groups),
        grid=(t // tm,),
        in_specs=[pl.BlockSpec((tm, D_MODEL), row),
                  _mod_spec(0, tm, seq), _mod_spec(1, tm, seq),
                  pl.BlockSpec((1, D_MODEL), const),
                  pl.BlockSpec((D_MODEL, n), const),
                  pl.BlockSpec((tm, LANES), tab), pl.BlockSpec((tm, LANES), tab),
                  pl.BlockSpec(gains.shape, const),
                  pl.BlockSpec((LANES, LANES), const)],
        out_specs=[pl.BlockSpec((tm, g[0]), row) for g in groups],
        out_shape=[jax.ShapeDtypeStruct((t, g[0]), BF16) for g in groups],
        compiler_params=_cparams(("arbitrary",)),
        name="norm_mod_proj",
    )(x2, mod, mod, norm_g.reshape(1, D_MODEL), w, cos_t, sn_t, gains, gmat)


def _flash_kernel(slope_ref, q_ref, k_ref, v_ref, lam_ref, subg_ref, o_ref,
                  q_sc, v_sc, m_sc, acc_sc, s0_sc, s1_sc, p0_sc, p1_sc, a0_sc, a1_sc,
                  *, tq, tk, seq, nstack, alibi, lam_init):
    g = pl.program_id(1)
    lane = lax.broadcasted_iota(jnp.int32, (1, LANES), 1)
    low_half = lane < HEAD_DIM
    s_bufs, p_bufs, a_bufs = (s0_sc, s1_sc), (p0_sc, p1_sc), (a0_sc, a1_sc)
    nchunks = seq // tk
    ntiles = seq // tq
    nrows = nstack * tq

    v_sc[:, 0:LANES] = v_ref[...]
    v_sc[:, LANES:2 * LANES] = jnp.ones((seq, LANES), BF16)

    if alibi:
        rc = (lax.broadcasted_iota(jnp.int32, (tq, tk), 0)
              - lax.broadcasted_iota(jnp.int32, (tq, tk), 1)).astype(F32)
        neg_slope = -slope_ref[g]
        lp = lam_ref[...]
        l1 = jnp.sum(lp[0:1, :] * lp[1:2, :], axis=-1, keepdims=True)
        l2 = jnp.sum(lp[2:3, :] * lp[3:4, :], axis=-1, keepdims=True)
        lam = jnp.exp(l1) - jnp.exp(l2) + lam_init

    def tile_rows(t):
        return pl.ds(t * tq, tq) if isinstance(t, int) else pl.ds(pl.multiple_of(t * tq, tq), tq)

    def chunk_rows(c):
        return pl.ds(c * tk, tk) if isinstance(c, int) else pl.ds(pl.multiple_of(c * tk, tk), tk)

    def load_queries(t, slot):
        for u in range(nstack):
            src = q_ref[tile_rows(t), (u // 2) * LANES:(u // 2 + 1) * LANES]
            keep = low_half if u % 2 == 0 else jnp.logical_not(low_half)
            q_sc[slot, u * tq:(u + 1) * tq, :] = jnp.where(keep, src, jnp.zeros_like(src))

    def scores(t, c, slot, par):
        s = lax.dot_general(q_sc[slot], k_ref[chunk_rows(c), :], (((1,), (1,)), ((), ())),
                            preferred_element_type=F32)
        if alibi:
            base = (t * tq - c * tk).astype(F32) if not (isinstance(t, int) and isinstance(c, int)) \
                else float(t * tq - c * tk)
            bias = neg_slope * jnp.abs(rc + base)
            s = s + jnp.concatenate([bias] * nstack, axis=0)
        s_bufs[par][...] = s

    def softmax(slot, par, first):
        s = s_bufs[par][...]
        m_cur = jnp.max(s, axis=-1, keepdims=True)
        if first:
            m_new = jnp.broadcast_to(m_cur, (nrows, LANES))
        else:
            m_old = m_sc[slot]
            m_new = jnp.maximum(m_old, m_cur)
            a_bufs[par][...] = jnp.exp2(m_old - m_new)
        p_bufs[par][...] = jnp.exp2(s - pltpu.repeat(m_new, tk // LANES, 1)).astype(BF16)
        m_sc[slot] = m_new

    def values(c, slot, par, first):
        d = jnp.dot(p_bufs[par][...], v_sc[chunk_rows(c), :], preferred_element_type=F32)
        if first:
            acc_sc[slot] = d
        else:
            acc_sc[slot] = pltpu.repeat(a_bufs[par][...], 2, 1) * acc_sc[slot] + d

    def finalize(t, slot):
        acc = acc_sc[slot]
        o = acc[:, 0:LANES] * (1.0 / acc[:, LANES:2 * LANES])
        if alibi:
            ob = o[0:tq, :] - lam * o[tq:2 * tq, :]
            ms = jnp.mean(ob * ob, axis=-1, keepdims=True)
            ob = ob * lax.rsqrt(ms + EPS) * subg_ref[...] * (1.0 - lam_init)
            o_ref[tile_rows(t), :] = ob.astype(o_ref.dtype)
        else:
            for pair in range(nstack // 2):
                lo = o[(2 * pair) * tq:(2 * pair + 1) * tq, :]
                hi = o[(2 * pair + 1) * tq:(2 * pair + 2) * tq, :]
                o_ref[tile_rows(t), pair * LANES:(pair + 1) * LANES] = (
                    jnp.where(low_half, lo, hi).astype(o_ref.dtype))

    def step(t, t_next, slot, c):
        static = isinstance(c, int)
        par = c % 2 if static else None
        ahead2 = c + 2
        if static and ahead2 >= nchunks:
            scores(t_next, ahead2 - nchunks, 1 - slot, par)
        else:
            scores(t, ahead2, slot, par)
        if static and c + 1 >= nchunks:
            softmax(1 - slot, 1 - par, first=True)
        else:
            softmax(slot, 1 - par, first=False)
        values(c, slot, par, first=static and c == 0)

    load_queries(0, 0)
    scores(0, 0, 0, 0)
    scores(0, 1, 0, 1)
    softmax(0, 0, first=True)

    def tile_body(t, carry):
        slot = t % 2
        t_next = (t + 1) % ntiles
        load_queries(t_next, 1 - slot)
        lead = min(2, nchunks - 2)
        for c in range(lead):
            step(t, t_next, slot, c)

        def pair_body(j, inner):
            for par in range(2):
                c = 2 * j + par
                scores(t, c + 2, slot, par)
                softmax(slot, 1 - par, first=False)
                values(c, slot, par, first=False)
            return inner

        lax.fori_loop(lead // 2, (nchunks - 2) // 2, pair_body, 0)
        for c in range(nchunks - 2, nchunks):
            step(t, t_next, slot, c)
        finalize(t, slot)
        return carry

    lax.fori_loop(0, ntiles, tile_body, 0)


def _flash_attention(q, k, v, slopes, lam_pack, subg, *, nbatch, seq, ngroups, nstack, alibi, lam_init,
                     nrows=1024):
    qw = (nstack // 2) * LANES
    tq = nrows // nstack
    tk = min(1024, seq // 2)
    kernel = functools.partial(_flash_kernel, tq=tq, tk=tk, seq=seq, nstack=nstack, alibi=alibi,
                               lam_init=lam_init)
    grid_spec = pltpu.PrefetchScalarGridSpec(
        num_scalar_prefetch=1,
        grid=(nbatch, ngroups),
        in_specs=[pl.BlockSpec((None, seq, qw), lambda b, g, s: (b, 0, g)),
                  pl.BlockSpec((None, seq, LANES), lambda b, g, s: (b, 0, g)),
                  pl.BlockSpec((None, seq, LANES), lambda b, g, s: (b, 0, g)),
                  pl.BlockSpec(lam_pack.shape, lambda b, g, s: (0, 0)),
                  pl.BlockSpec(subg.shape, lambda b, g, s: (0, 0))],
        out_specs=pl.BlockSpec((None, seq, qw), lambda b, g, s: (b, 0, g)),
        scratch_shapes=[pltpu.VMEM((2, nrows, LANES), BF16),
                        pltpu.VMEM((seq, 2 * LANES), BF16),
                        pltpu.VMEM((2, nrows, LANES), F32),
                        pltpu.VMEM((2, nrows, 2 * LANES), F32),
                        pltpu.VMEM((nrows, tk), F32), pltpu.VMEM((nrows, tk), F32),
                        pltpu.VMEM((nrows, tk), BF16), pltpu.VMEM((nrows, tk), BF16),
                        pltpu.VMEM((nrows, LANES), F32), pltpu.VMEM((nrows, LANES), F32)])
    return pl.pallas_call(
        kernel,
        grid_spec=grid_spec,
        out_shape=jax.ShapeDtypeStruct((nbatch, seq, ngroups * qw), BF16),
        compiler_params=_cparams(("arbitrary", "arbitrary")),
        name="flash_alibi" if alibi else "flash_gqa",
    )(slopes, q, k, v, lam_pack, subg)


def _natten_kernel(q_ref, k_ref, v_ref, bias_ref, o_ref, v_sc, s0_sc, s1_sc, *, seq):
    lane = lax.broadcasted_iota(jnp.int32, (1, LANES), 1)
    low_half = lane < HEAD_DIM
    ntiles = seq // NAT_TQ
    rows = seq // GRID_W
    s_bufs = (s0_sc, s1_sc)

    v_sc[:, 0:LANES] = v_ref[...]
    v_sc[:, LANES:2 * LANES] = jnp.ones((seq, LANES), BF16)

    def window(t):
        w0 = jnp.clip(t * NAT_QROWS - WIN_H // 2, 0, rows - NAT_KROWS)
        return pl.multiple_of(w0 * GRID_W, NAT_TQ)

    def scores(t, par):
        q = q_ref[pl.ds(pl.multiple_of(t * NAT_TQ, NAT_TQ), NAT_TQ), :]
        zero = jnp.zeros_like(q)
        q2 = jnp.concatenate([jnp.where(low_half, q, zero), jnp.where(low_half, zero, q)], axis=0)
        s = lax.dot_general(q2, k_ref[pl.ds(window(t), NAT_TK), :], (((1,), (1,)), ((), ())),
                            preferred_element_type=F32)
        cls = jnp.where(t == 0, 0, jnp.where(t == ntiles - 1, 2, 1))
        bias = bias_ref[cls].astype(F32).reshape(2 * NAT_TQ, NAT_TK)
        s_bufs[par][...] = s + bias

    def finish(t, par):
        s = s_bufs[par][...]
        m = jnp.max(s, axis=-1, keepdims=True)
        p = jnp.exp2(s - m).astype(BF16)
        acc = jnp.dot(p, v_sc[pl.ds(window(t), NAT_TK), :], preferred_element_type=F32)
        o = acc[:, 0:LANES] * (1.0 / acc[:, LANES:2 * LANES])
        out = jnp.where(low_half, o[0:NAT_TQ, :], o[NAT_TQ:, :])
        o_ref[pl.ds(pl.multiple_of(t * NAT_TQ, NAT_TQ), NAT_TQ), :] = out.astype(o_ref.dtype)

    scores(jnp.int32(0), 0)

    def pair_body(u, carry):
        t = 2 * u
        scores(t + 1, 1)
        finish(t, 0)
        scores((t + 2) % ntiles, 0)
        finish(t + 1, 1)
        return carry

    lax.fori_loop(0, ntiles // 2, pair_body, 0)


def _natten(q, k, v, bias_tab, *, nbatch, seq):
    npairs = C_HEADS // 2
    blk = pl.BlockSpec((None, seq, LANES), lambda p, b: (b, 0, p))
    return pl.pallas_call(
        functools.partial(_natten_kernel, seq=seq),
        grid=(npairs, nbatch),
        in_specs=[blk, blk, blk,
                  pl.BlockSpec((3, None, 2, NAT_TQ, NAT_TK), lambda p, b: (0, p, 0, 0, 0))],
        out_specs=blk,
        out_shape=jax.ShapeDtypeStruct((nbatch, seq, C_HEADS * HEAD_DIM), BF16),
        scratch_shapes=[pltpu.VMEM((seq, 2 * LANES), BF16),
                        pltpu.VMEM((2 * NAT_TQ, NAT_TK), F32), pltpu.VMEM((2 * NAT_TQ, NAT_TK), F32)],
        compiler_params=_cparams(("arbitrary", "arbitrary")),
        name="natten",
    )(q, k, v, bias_tab)


def _natten_bias_table(rpb, seq):
    rows = seq // GRID_W
    ntiles = rows // NAT_QROWS
    col = jnp.arange(GRID_W)
    cstart = jnp.clip(col - WIN_W // 2, 0, GRID_W - WIN_W)
    col_valid = (col[None, :] >= cstart[:, None]) & (col[None, :] < cstart[:, None] + WIN_W)
    dc_idx = jnp.clip(col[None, :] - col[:, None] + WIN_W - 1, 0, 2 * WIN_W - 2)
    rpb_cols = rpb[:, :, dc_idx]
    tabs = []
    for tile in (0, 1, ntiles - 1):
        r = tile * NAT_QROWS + jnp.arange(NAT_QROWS)
        w0 = int(np.clip(tile * NAT_QROWS - WIN_H // 2, 0, rows - NAT_KROWS))
        kr = w0 + jnp.arange(NAT_KROWS)
        rstart = jnp.clip(r - WIN_H // 2, 0, rows - WIN_H)
        row_valid = (kr[None, :] >= rstart[:, None]) & (kr[None, :] < rstart[:, None] + WIN_H)
        dr_idx = jnp.clip(kr[None, :] - r[:, None] + WIN_H - 1, 0, 2 * WIN_H - 2)
        pick = (dr_idx[:, :, None] == jnp.arange(2 * WIN_H - 1)[None, None, :]).astype(F32)
        b = jnp.einsum("qkd,hdcx->hqckx", pick, rpb_cols * LOG2E,
                       precision=lax.Precision.HIGHEST)
        valid = row_valid[:, None, :, None] & col_valid[None, :, None, :]
        b = jnp.where(valid[None], b, -jnp.inf)
        tabs.append(b.reshape(C_HEADS, NAT_TQ, NAT_TK))
    return jnp.stack(tabs).reshape(3, C_HEADS // 2, 2, NAT_TQ, NAT_TK).astype(BF16)


def _outproj_kernel(x_ref, gate_ref, *rest, nmix):
    mix_refs, w_refs, o_ref = rest[:nmix], rest[nmix:2 * nmix], rest[2 * nmix]
    y = None
    for m_ref, w_ref in zip(mix_refs, w_refs):
        d = jnp.dot(m_ref[...], w_ref[...], preferred_element_type=F32)
        y = d if y is None else y + d
    o_ref[...] = x_ref[...] + gate_ref[...] * y


def _outproj(x2, mod, mixes, ws, seq, tm=512):
    t = x2.shape[0]
    row = lambda i: (i, 0)
    const = lambda i: (0, 0)
    return pl.pallas_call(
        functools.partial(_outproj_kernel, nmix=len(mixes)),
        grid=(t // tm,),
        in_specs=[pl.BlockSpec((tm, D_MODEL), row), _mod_spec(2, tm, seq)]
                 + [pl.BlockSpec((tm, m.shape[1]), row) for m in mixes]
                 + [pl.BlockSpec(w.shape, const) for w in ws],
        out_specs=pl.BlockSpec((tm, D_MODEL), row),
        out_shape=jax.ShapeDtypeStruct((t, D_MODEL), F32),
        compiler_params=_cparams(("arbitrary",)),
        name="outproj_residual",
    )(x2, mod, *mixes, *ws)


def _ffn_kernel(x_ref, g1_ref, sh_ref, sc_ref, g2_ref, ng_ref, ma_ref, mb_ref, woa_ref, wob_ref,
                w1_ref, w3_ref, w2_ref, o_ref, x1_sc, h_sc, g_sc, *, nf, tf, nb):
    j = pl.program_id(1)
    tn = D_MODEL // nb

    @pl.when(j == 0)
    def _():
        y = (jnp.dot(ma_ref[...], woa_ref[...], preferred_element_type=F32)
             + jnp.dot(mb_ref[...], wob_ref[...], preferred_element_type=F32))
        x1 = x_ref[...] + g1_ref[...] * y
        for n in range(nb):
            x1_sc[n] = x1[:, n * tn:(n + 1) * tn]
        h_sc[...] = _norm_mod(x1, ng_ref[...], sh_ref[...], sc_ref[...]).astype(BF16)

    @pl.when(j < nf)
    def _():
        h = h_sc[...]
        a = jnp.dot(h, w1_ref[...], preferred_element_type=F32)
        b = jnp.dot(h, w3_ref[...], preferred_element_type=F32)
        g_sc[j] = (a * _sigmoid(a) * b).astype(BF16)

    @pl.when(j >= nf)
    def _():
        y = None
        for f in range(nf):
            d = jnp.dot(g_sc[f], w2_ref[f * tf:(f + 1) * tf, :], preferred_element_type=F32)
            y = d if y is None else y + d
        o_ref[...] = x1_sc[j - nf] + g2_ref[...] * y


def _ffn(x2, mod, norm_g, mixes, wos, w1, w3, w2, seq, tm=512):
    t = x2.shape[0]
    nf, _, tf = w1.shape
    nb, _, tn = w2.shape
    fcl = lambda j: jnp.minimum(j, nf - 1)
    ncl = lambda j: jnp.maximum(j - nf, 0)
    row = lambda i, j: (i, 0)
    const = lambda i, j: (0, 0)
    g2_spec = pl.BlockSpec((None, None, 1, tn), lambda i, j: ((i * tm) // seq, 5, 0, ncl(j)))
    return pl.pallas_call(
        functools.partial(_ffn_kernel, nf=nf, tf=tf, nb=nb),
        grid=(t // tm, nf + nb),
        in_specs=[pl.BlockSpec((tm, D_MODEL), row),
                  _mod_spec(2, tm, seq), _mod_spec(3, tm, seq), _mod_spec(4, tm, seq), g2_spec,
                  pl.BlockSpec((1, D_MODEL), const),
                  pl.BlockSpec((tm, mixes[0].shape[1]), row), pl.BlockSpec((tm, mixes[1].shape[1]), row),
                  pl.BlockSpec(wos[0].shape, const), pl.BlockSpec(wos[1].shape, const),
                  pl.BlockSpec((None, D_MODEL, tf), lambda i, j: (fcl(j), 0, 0)),
                  pl.BlockSpec((None, D_MODEL, tf), lambda i, j: (fcl(j), 0, 0)),
                  pl.BlockSpec((None, nf * tf, tn), lambda i, j: (ncl(j), 0, 0))],
        out_specs=pl.BlockSpec((tm, tn), lambda i, j: (i, ncl(j))),
        out_shape=jax.ShapeDtypeStruct((t, D_MODEL), F32),
        scratch_shapes=[pltpu.VMEM((nb, tm, tn), F32), pltpu.VMEM((tm, D_MODEL), BF16),
                        pltpu.VMEM((nf, tm, tf), BF16)],
        compiler_params=_cparams(("arbitrary", "arbitrary")),
        name="outproj_ffn_swiglu",
    )(x2, mod, mod, mod, mod, norm_g.reshape(1, D_MODEL), mixes[0], mixes[1], wos[0], wos[1], w1, w3, w2)


def _router_kernel(x_ref, sh_ref, sc_ref, g_ref, rw_ref, rb_ref, h_ref, meta_ref):
    h = _norm_mod(x_ref[...], g_ref[...], sh_ref[...], sc_ref[...])
    _store_pieces(h_ref, _pack_halves(h[:, 0:PACK_W], h[:, PACK_W:D_MODEL]))
    lane = lax.broadcasted_iota(jnp.int32, (1, LANES), 1).astype(F32)
    logits = jnp.dot(h, rw_ref[...], preferred_element_type=F32,
                     precision=lax.Precision.HIGHEST) + rb_ref[...]
    logits = jnp.where(lane < N_EXPERTS, logits, -jnp.inf)
    m1 = jnp.max(logits, axis=-1, keepdims=True)
    i1 = jnp.min(jnp.where(logits == m1, lane, float(LANES)), axis=-1, keepdims=True)
    rest = jnp.where(lane == i1, -jnp.inf, logits)
    m2 = jnp.max(rest, axis=-1, keepdims=True)
    i2 = jnp.min(jnp.where(rest == m2, lane, float(LANES)), axis=-1, keepdims=True)
    e = jnp.exp(m2 - m1)
    g1 = 1.0 / (1.0 + e)
    g2 = e * g1
    meta = jnp.where(lane == 0, i1, jnp.where(lane == 1, i2, jnp.where(lane == 2, g1,
                                                                       jnp.where(lane == 3, g2, 0.0))))
    meta_ref[...] = meta


def _router(x2, mod, norm_g, rw_pad, rb_pad, seq, tm=512):
    t = x2.shape[0]
    row = lambda i: (i, 0)
    const = lambda i: (0, 0)
    return pl.pallas_call(
        _router_kernel,
        grid=(t // tm,),
        in_specs=[pl.BlockSpec((tm, D_MODEL), row), _mod_spec(3, tm, seq), _mod_spec(4, tm, seq),
                  pl.BlockSpec((1, D_MODEL), const),
                  pl.BlockSpec((D_MODEL, LANES), const), pl.BlockSpec((1, LANES), const)],
        out_specs=[pl.BlockSpec((SC_PIECES, tm, SC_ROW_WORDS), lambda i: (0, i, 0)),
                   pl.BlockSpec((tm, LANES), row)],
        out_shape=[jax.ShapeDtypeStruct((SC_PIECES, t, SC_ROW_WORDS), jnp.int32),
                   jax.ShapeDtypeStruct((t, LANES), F32)],
        compiler_params=_cparams(("arbitrary",)),
        name="moe_router",
    )(x2, mod, mod, norm_g.reshape(1, D_MODEL), rw_pad, rb_pad)


def _moe_kernel(te_ref, nt_ref, x_ref, w1_ref, w3_ref, w2_ref, o_ref, x_sc, g_sc, *, nf, tf):
    i = pl.program_id(0)
    j = pl.program_id(1)
    active = i < nt_ref[0]

    @pl.when(jnp.logical_and(active, j == 0))
    def _():
        x_sc[...] = _unpack_halves(_load_pieces(x_ref)).astype(BF16)

    @pl.when(jnp.logical_and(active, j < nf))
    def _():
        x = x_sc[...]
        a = jnp.dot(x, w1_ref[...], preferred_element_type=F32)
        b = jnp.dot(x, w3_ref[...], preferred_element_type=F32)
        g_sc[j] = (a * _sigmoid(a) * b).astype(BF16)

    @pl.when(jnp.logical_and(active, j >= nf))
    def _():
        ya, yb = None, None
        for f in range(nf):
            g = g_sc[f]
            da = jnp.dot(g, w2_ref[f * tf:(f + 1) * tf, 0:SC_ROW_WORDS], preferred_element_type=F32)
            db = jnp.dot(g, w2_ref[f * tf:(f + 1) * tf, SC_ROW_WORDS:2 * SC_ROW_WORDS],
                         preferred_element_type=F32)
            ya = da if ya is None else ya + da
            yb = db if yb is None else yb + db
        o_ref[...] = _pack_halves(ya, yb)

    @pl.when(jnp.logical_and(jnp.logical_not(active), j >= nf))
    def _():
        o_ref[...] = jnp.zeros(o_ref.shape, o_ref.dtype)


def _moe_grouped(xs, tile_expert, num_tiles, w1, w3, w2, tg):
    p = xs.shape[1]
    nf, tf = w1.shape[1], w1.shape[3]
    tn = SC_ROW_WORDS
    nb = SC_PIECES
    fcl = lambda j: jnp.minimum(j, nf - 1)
    ncl = lambda j: jnp.maximum(j - nf, 0)
    grid_spec = pltpu.PrefetchScalarGridSpec(
        num_scalar_prefetch=2,
        grid=(p // tg, nf + nb),
        in_specs=[pl.BlockSpec((SC_PIECES, tg, SC_ROW_WORDS), lambda i, j, te, nt: (0, i, 0)),
                  pl.BlockSpec((None, None, D_MODEL, tf), lambda i, j, te, nt: (te[i], fcl(j), 0, 0)),
                  pl.BlockSpec((None, None, D_MODEL, tf), lambda i, j, te, nt: (te[i], fcl(j), 0, 0)),
                  pl.BlockSpec((None, None, nf * tf, 2 * tn), lambda i, j, te, nt: (te[i], ncl(j), 0, 0))],
        out_specs=pl.BlockSpec((None, tg, tn), lambda i, j, te, nt: (ncl(j), i, 0)),
        scratch_shapes=[pltpu.VMEM((tg, D_MODEL), BF16), pltpu.VMEM((nf, tg, tf), BF16)])
    return pl.pallas_call(
        functools.partial(_moe_kernel, nf=nf, tf=tf),
        grid_spec=grid_spec,
        out_shape=jax.ShapeDtypeStruct((SC_PIECES, p, SC_ROW_WORDS), jnp.int32),
        compiler_params=_cparams(("arbitrary", "arbitrary")),
        name="moe_grouped",
    )(tile_expert, num_tiles, xs, w1, w3, w2)


def _combine_kernel(x_ref, gate_ref, meta_ref, y1_ref, y2_ref, o_ref):
    meta = meta_ref[...]
    moe = (meta[:, 2:3] * _unpack_halves(_load_pieces(y1_ref))
           + meta[:, 3:4] * _unpack_halves(_load_pieces(y2_ref)))
    o_ref[...] = x_ref[...] + gate_ref[...] * moe


def _combine(x2, mod, meta, yg, seq, tm=512):
    t = x2.shape[0]
    nt = t // tm
    row = lambda i: (i, 0)
    return pl.pallas_call(
        _combine_kernel,
        grid=(nt,),
        in_specs=[pl.BlockSpec((tm, D_MODEL), row), _mod_spec(5, tm, seq),
                  pl.BlockSpec((tm, LANES), row),
                  pl.BlockSpec((SC_PIECES, tm, SC_ROW_WORDS), lambda i: (0, i, 0)),
                  pl.BlockSpec((SC_PIECES, tm, SC_ROW_WORDS), lambda i: (0, nt + i, 0))],
        out_specs=pl.BlockSpec((tm, D_MODEL), row),
        out_shape=jax.ShapeDtypeStruct((t, D_MODEL), F32),
        compiler_params=_cparams(("arbitrary",)),
        name="moe_combine",
    )(x2, mod, meta, yg, yg)


def _sc_gather(table, idx):
    n = idx.shape[0]
    nrows = table.shape[1]
    pieces = n * SC_PIECES
    idx_pieces = jnp.concatenate([idx + s * nrows for s in range(SC_PIECES)]).reshape(1, pieces)
    mesh = plsc.VectorSubcoreMesh(core_axis_name="core", subcore_axis_name="subcore")

    @pl.kernel(out_type=jax.ShapeDtypeStruct((pieces, SC_ROW_WORDS), table.dtype), mesh=mesh, scratch_types=[])
    def gather_kernel(table_hbm, idx_hbm, out_hbm):
        def body(idx_vmem, out_vmem):
            pltpu.sync_copy(table_hbm.at[idx_vmem.at[0]], out_vmem)

        pltpu.emit_pipeline(
            body,
            grid=(pieces // SC_WINDOW,),
            in_specs=[pl.BlockSpec((1, SC_WINDOW), lambda i: (0, i))],
            out_specs=[pl.BlockSpec((SC_WINDOW, SC_ROW_WORDS), lambda i: (i, 0))],
            core_axis_name=("core", "subcore"),
            dimension_semantics=(pltpu.PARALLEL,),
        )(idx_hbm, out_hbm)

    out = gather_kernel(table.reshape(SC_PIECES * nrows, SC_ROW_WORDS), idx_pieces)
    return out.reshape(SC_PIECES, n, SC_ROW_WORDS)


def _moe(x2, mod, norm_g, rw_pad, rb_pad, w1, w3, w2, seq, tg=1024):
    t = x2.shape[0]
    h, meta = _router(x2, mod, norm_g, rw_pad, rb_pad, seq)
    e_flat = meta[:, 0:2].astype(jnp.int32).reshape(-1)
    onehot = (e_flat[:, None] == jnp.arange(N_EXPERTS)[None, :]).astype(jnp.int32)
    csum = jnp.cumsum(onehot, axis=0)
    counts = csum[-1]
    rank = jnp.take_along_axis(csum, e_flat[:, None], axis=1)[:, 0] - 1
    padded = ((counts + tg - 1) // tg) * tg
    pend = jnp.cumsum(padded)
    pos = (pend - padded)[e_flat] + rank
    p_rows = 2 * t + N_EXPERTS * tg
    row_token = jnp.zeros((p_rows,), jnp.int32).at[pos].set(jnp.arange(2 * t, dtype=jnp.int32) // 2)
    tile_start = jnp.arange(p_rows // tg, dtype=jnp.int32) * tg
    tile_expert = jnp.minimum(jnp.sum((tile_start[:, None] >= pend[None, :]).astype(jnp.int32), axis=1),
                              N_EXPERTS - 1)
    num_tiles = (pend[-1] // tg).astype(jnp.int32).reshape(1)
    xs = _sc_gather(h, row_token)
    ys = _moe_grouped(xs, tile_expert, num_tiles, w1, w3, w2, tg)
    yg = _sc_gather(ys, jnp.concatenate([pos[0::2], pos[1::2]]))
    return _combine(x2, mod, meta, yg, seq)


def _rope_tables(seq):
    t = np.arange(seq)
    lane = np.arange(LANES)
    d = lane % HEAD_DIM
    pos = np.where((d // 32)[None, :] == 0, (t // GRID_W)[:, None], (t % GRID_W)[:, None]).astype(np.float32)
    inv = (ROPE_THETA ** (-np.arange(16, dtype=np.float32) / 16)).astype(np.float32)
    ang = pos * inv[(d % 16)][None, :]
    return jnp.asarray(np.cos(ang), F32), jnp.asarray(np.sin(ang), F32)


def _pair_gain(g):
    return jnp.concatenate([g, g]).astype(F32)


def _prepare(p, seq):
    even, odd = {}, {}
    w_in = p["w_in_even"][0]
    qa, ka, va, qb, kb, vb = jnp.split(w_in, [512, 640, 768, 1280, 1792], axis=1)
    dup = lambda w: jnp.concatenate([w[:, 0:64], w[:, 0:64], w[:, 64:128], w[:, 64:128]], axis=1)
    even["w_in"] = jnp.concatenate([qa, qb, dup(ka), dup(va), kb, vb], axis=1).astype(BF16)
    gains = jnp.zeros((8, LANES), F32)
    gains = gains.at[0].set(_pair_gain(p["qnorm_a"][0])).at[1].set(_pair_gain(p["knorm_a"][0]))
    gains = gains.at[2].set(_pair_gain(p["qnorm_b"][0])).at[3].set(_pair_gain(p["knorm_b"][0]))
    even["gains"] = gains
    qscale = SCALE * LOG2E
    even["groups"] = ((512, 0, True, qscale), (512, 2, False, qscale), (256, 1, True, 1.0),
                      (256, None, False, 1.0), (512, 3, False, 1.0), (512, None, False, 1.0))
    lam = jnp.zeros((8, LANES), F32)
    for r, name in enumerate(("lam_q1", "lam_k1", "lam_q2", "lam_k2")):
        lam = lam.at[r, 0:HEAD_DIM].set(p[name][0])
    even["lam"] = lam
    even["subg"] = p["subln_b"][0].reshape(1, LANES).astype(F32)
    wo = p["w_out_even"][0].astype(BF16)
    even["wo"] = (wo[0:512], wo[512:1024])
    even["slopes"] = jnp.asarray(LOG2E * 2.0 ** (-8.0 * (np.arange(B_HEADS) + 1.0) / B_HEADS), F32)
    for name in ("ffn_w1", "ffn_w3"):
        even[name] = p[name][0].astype(BF16).reshape(D_MODEL, FFN_NF, D_FF // FFN_NF).transpose(1, 0, 2)
    even["ffn_w2"] = p["ffn_w2"][0].astype(BF16).reshape(D_FF, FFN_NB, D_MODEL // FFN_NB).transpose(1, 0, 2)
    for name in ("ada_w", "ada_b", "norm_mix", "norm_ffn"):
        even[name] = p[name + "_even"][0]
        odd[name] = p[name + "_odd"][0]

    odd["w_qkv"] = p["w_qkv_odd"][0].astype(BF16)
    gains = jnp.zeros((8, LANES), F32)
    odd["gains"] = gains.at[0].set(_pair_gain(p["qnorm_c"][0])).at[1].set(_pair_gain(p["knorm_c"][0]))
    odd["groups"] = ((1024, 0, False, qscale), (1024, 1, False, 1.0), (1024, None, False, 1.0))
    odd["bias_tab"] = _natten_bias_table(p["rpb_c"][0], seq)
    odd["wo"] = (p["w_out_odd"][0].astype(BF16),)
    odd["rw"] = jnp.zeros((D_MODEL, LANES), F32).at[:, 0:N_EXPERTS].set(p["router_w"][0])
    odd["rb"] = jnp.zeros((1, LANES), F32).at[0, 0:N_EXPERTS].set(p["router_b"][0])
    for name in ("moe_w1", "moe_w3"):
        odd[name] = (p[name][0].astype(BF16).reshape(N_EXPERTS, D_MODEL, MOE_NF, D_FF_EXPERT // MOE_NF)
                     .transpose(0, 2, 1, 3))
    w2 = p["moe_w2"][0].astype(BF16).reshape(N_EXPERTS, D_FF_EXPERT, 2, SC_PIECES, SC_ROW_WORDS)
    odd["moe_w2"] = w2.transpose(0, 3, 1, 2, 4).reshape(N_EXPERTS, SC_PIECES, D_FF_EXPERT, 2 * SC_ROW_WORDS)
    cos_t, sn_t = _rope_tables(seq)
    gmat = jnp.asarray(np.kron(np.eye(2), np.full((HEAD_DIM, HEAD_DIM), 1.0 / HEAD_DIM)), BF16)
    return even, odd, (cos_t, sn_t, gmat)


def _even_layer(x2, c, ev, shared, nbatch, seq):
    cos_t, sn_t, gmat = shared
    mod = _ada_modulation(c, ev["ada_w"], ev["ada_b"])
    qa, qb, ka, va, kb, vb = _projection(x2, mod, ev["norm_mix"], ev["w_in"], cos_t, sn_t, ev["gains"], gmat,
                                         ev["groups"], seq)
    shp = lambda a: a.reshape(nbatch, seq, a.shape[1])
    lam_init = 0.8 - 0.6 * math.exp(-0.3 * 0)
    mix_a = _flash_attention(shp(qa), shp(ka), shp(va), ev["slopes"], ev["lam"], ev["subg"], nbatch=nbatch, seq=seq,
                             ngroups=A_KV_HEADS, nstack=4, alibi=False, lam_init=lam_init)
    mix_b = _flash_attention(shp(qb), shp(kb), shp(vb), ev["slopes"], ev["lam"], ev["subg"], nbatch=nbatch, seq=seq,
                             ngroups=B_HEADS, nstack=2, alibi=True, lam_init=lam_init)
    t = nbatch * seq
    return _ffn(x2, mod, ev["norm_ffn"], (mix_a.reshape(t, -1), mix_b.reshape(t, -1)), ev["wo"],
                ev["ffn_w1"], ev["ffn_w3"], ev["ffn_w2"], seq)


def _odd_layer(x2, c, od, shared, nbatch, seq):
    cos_t, sn_t, gmat = shared
    mod = _ada_modulation(c, od["ada_w"], od["ada_b"])
    q, k, v = _projection(x2, mod, od["norm_mix"], od["w_qkv"], cos_t, sn_t, od["gains"], gmat, od["groups"], seq)
    shp = lambda a: a.reshape(nbatch, seq, a.shape[1])
    mix = _natten(shp(q), shp(k), shp(v), od["bias_tab"], nbatch=nbatch, seq=seq)
    x2 = _outproj(x2, mod, (mix.reshape(nbatch * seq, -1),), od["wo"], seq)
    return _moe(x2, mod, od["norm_ffn"], od["rw"], od["rb"], od["moe_w1"], od["moe_w3"], od["moe_w2"], seq)


def _trunk(x, c, ev, od, shared):
    nbatch, seq, d = x.shape
    x2 = x.reshape(nbatch * seq, d)
    x2 = _even_layer(x2, c, ev, shared, nbatch, seq)
    x2 = _odd_layer(x2, c, od, shared, nbatch, seq)
    return x2.reshape(nbatch, seq, d)


def kernel(x_prompt, x_sample, c_prompt, c_sample, ada_w_even, ada_b_even, norm_mix_even, norm_ffn_even, w_in_even, qnorm_a, knorm_a, qnorm_b, knorm_b, lam_q1, lam_k1, lam_q2, lam_k2, subln_b, w_out_even, ffn_w1, ffn_w3, ffn_w2, ada_w_odd, ada_b_odd, norm_mix_odd, norm_ffn_odd, w_qkv_odd, qnorm_c, knorm_c, rpb_c, w_out_odd, router_w, router_b, moe_w1, moe_w3, moe_w2):
    params = dict(ada_w_even=ada_w_even, ada_b_even=ada_b_even, norm_mix_even=norm_mix_even,
                  norm_ffn_even=norm_ffn_even, w_in_even=w_in_even, qnorm_a=qnorm_a, knorm_a=knorm_a,
                  qnorm_b=qnorm_b, knorm_b=knorm_b, lam_q1=lam_q1, lam_k1=lam_k1, lam_q2=lam_q2, lam_k2=lam_k2,
                  subln_b=subln_b, w_out_even=w_out_even, ffn_w1=ffn_w1, ffn_w3=ffn_w3, ffn_w2=ffn_w2,
                  ada_w_odd=ada_w_odd, ada_b_odd=ada_b_odd, norm_mix_odd=norm_mix_odd, norm_ffn_odd=norm_ffn_odd,
                  w_qkv_odd=w_qkv_odd, qnorm_c=qnorm_c, knorm_c=knorm_c, rpb_c=rpb_c, w_out_odd=w_out_odd,
                  router_w=router_w, router_b=router_b, moe_w1=moe_w1, moe_w3=moe_w3, moe_w2=moe_w2)
    seq = x_prompt.shape[1]
    ev, od, shared = _prepare(params, seq)
    y_prompt = _trunk(x_prompt, c_prompt, ev, od, shared)
    y_sample = _trunk(x_sample, c_sample, ev, od, shared)
    return (y_prompt, y_sample)
```

```python
import functools
import math

import numpy as np
import jax
import jax.numpy as jnp
from jax import lax
from jax.experimental import pallas as pl
from jax.experimental.pallas import tpu as pltpu
from jax.experimental.pallas import tpu_sc as plsc

F32 = jnp.float32
BF16 = jnp.bfloat16

D_MODEL = 1024
HEAD_DIM = 64
LANES = 128
SCALE = HEAD_DIM ** -0.5
LOG2E = 1.4426950408889634
GRID_W = 64
EPS = 1e-6
ROPE_THETA = 10000.0
A_Q_HEADS = 8
A_KV_HEADS = 2
B_HEADS = 4
C_HEADS = 16
WIN_H = 8
WIN_W = 16
N_EXPERTS = 8
D_FF = 2816
D_FF_EXPERT = 3584
VMEM_LIMIT = 56 * 1024 * 1024

NAT_QROWS = 4
NAT_KROWS = 12
NAT_TQ = NAT_QROWS * GRID_W
NAT_TK = NAT_KROWS * GRID_W
PACK_W = D_MODEL // 2
SC_WINDOW = 128
SC_ROW_WORDS = 256
SC_PIECES = PACK_W // SC_ROW_WORDS
FFN_CHUNK = 1408


def _cparams(sem):
    return pltpu.CompilerParams(dimension_semantics=sem, vmem_limit_bytes=VMEM_LIMIT)


def _norm_mod(x, g, shift, scale):
    ms = jnp.mean(x * x, axis=-1, keepdims=True)
    y = x * lax.rsqrt(ms + EPS) * g
    return y * (1.0 + scale) + shift


def _head_norm(x, gain, gmat):
    ms = jnp.dot((x * x).astype(BF16), gmat, preferred_element_type=F32)
    return x * lax.rsqrt(ms + EPS) * gain


def _rope(x, cos, sn, first_quarter):
    up = pltpu.roll(x, LANES - 16, 1)
    down = pltpu.roll(x, 16, 1)
    return x * cos + sn * jnp.where(first_quarter, -up, down)


def _sigmoid(a):
    return 1.0 / (1.0 + jnp.exp(-a))


def _pack_halves(a, b):
    hi = lax.bitcast_convert_type(a.astype(BF16).astype(F32), jnp.int32)
    lo = lax.bitcast_convert_type(b.astype(BF16).astype(F32), jnp.int32)
    return hi | lax.shift_right_logical(lo, jnp.full_like(lo, 16))


def _unpack_halves(w):
    hi = lax.bitcast_convert_type(w & jnp.int32(-65536), F32)
    lo = lax.bitcast_convert_type(lax.shift_left(w, jnp.full_like(w, 16)), F32)
    return jnp.concatenate([hi, lo], axis=1)


def _load_pieces(ref):
    return jnp.concatenate([ref[s] for s in range(SC_PIECES)], axis=1)


def _ada_kernel(c_ref, w_ref, b_ref, o_ref):
    c = c_ref[...]
    s = c * _sigmoid(c)
    o_ref[...] = jnp.dot(s, w_ref[...], preferred_element_type=F32,
                         precision=lax.Precision.HIGHEST) + b_ref[...]


def _ada_modulation(c, w, b):
    nb, d = c.shape
    n = w.shape[1]
    tn = 512
    mod = pl.pallas_call(
        _ada_kernel,
        grid=(n // tn,),
        in_specs=[pl.BlockSpec((nb, d), lambda j: (0, 0)),
                  pl.BlockSpec((d, tn), lambda j: (0, j)),
                  pl.BlockSpec((1, tn), lambda j: (0, j))],
        out_specs=pl.BlockSpec((nb, tn), lambda j: (0, j)),
        out_shape=jax.ShapeDtypeStruct((nb, n), F32),
        compiler_params=_cparams(("arbitrary",)),
        name="ada_mod",
    )(c, w, b.reshape(1, n))
    return mod.reshape(nb, 6, 1, d)


def _mod_spec(k, tm, seq):
    return pl.BlockSpec((None, None, 1, D_MODEL), lambda i, *_: ((i * tm) // seq, k, 0, 0))


def _proj_kernel(x_ref, sh_ref, sc_ref, g_ref, w_ref, cos_ref, sn_ref, gains_ref, gmat_ref, *o_refs,
                 groups):
    h = _norm_mod(x_ref[...], g_ref[...], sh_ref[...], sc_ref[...])
    y = jnp.dot(h.astype(BF16), w_ref[...], preferred_element_type=F32)
    lane = lax.broadcasted_iota(jnp.int32, (1, LANES), 1)
    first_quarter = (lane % 32) < 16
    gmat = gmat_ref[...]
    off = 0
    for o_ref, (width, gain_row, rope, mult) in zip(o_refs, groups):
        if gain_row is None:
            o_ref[...] = y[:, off:off + width].astype(o_ref.dtype)
        else:
            gain = gains_ref[gain_row:gain_row + 1, :]
            for t in range(width // LANES):
                z = _head_norm(y[:, off + t * LANES: off + (t + 1) * LANES], gain, gmat)
                if rope:
                    z = _rope(z, cos_ref[...], sn_ref[...], first_quarter)
                if mult != 1.0:
                    z = z * mult
                o_ref[:, t * LANES:(t + 1) * LANES] = z.astype(o_ref.dtype)
        off += width


def _projection(x2, mod, norm_g, w, cos_t, sn_t, gains, gmat, groups, seq, tm=512):
    t = x2.shape[0]
    n = w.shape[1]
    nseq = seq // tm
    row = lambda i: (i, 0)
    const = lambda i: (0, 0)
    tab = lambda i: (i % nseq, 0)
    return pl.pallas_call(
        functools.partial(_proj_kernel, groups=groups),
        grid=(t // tm,),
        in_specs=[pl.BlockSpec((tm, D_MODEL), row),
                  _mod_spec(0, tm, seq), _mod_spec(1, tm, seq),
                  pl.BlockSpec((1, D_MODEL), const),
                  pl.BlockSpec((D_MODEL, n), const),
                  pl.BlockSpec((tm, LANES), tab), pl.BlockSpec((tm, LANES), tab),
                  pl.BlockSpec(gains.shape, const),
                  pl.BlockSpec((LANES, LANES), const)],
        out_specs=[pl.BlockSpec((tm, g[0]), row) for g in groups],
        out_shape=[jax.ShapeDtypeStruct((t, g[0]), BF16) for g in groups],
        compiler_params=_cparams(("arbitrary",)),
        name="norm_mod_proj",
    )(x2, mod, mod, norm_g.reshape(1, D_MODEL), w, cos_t, sn_t, gains, gmat)


def _flash_kernel(slope_ref, q_ref, k_ref, v_ref, lam_ref, subg_ref, o_ref,
                  q_sc, v_sc, m_sc, acc_sc, s0_sc, s1_sc, p0_sc, p1_sc, a0_sc, a1_sc,
                  *, tq, tk, seq, nstack, alibi, lam_init):
    g = pl.program_id(1)
    lane = lax.broadcasted_iota(jnp.int32, (1, LANES), 1)
    low_half = lane < HEAD_DIM
    s_bufs, p_bufs, a_bufs = (s0_sc, s1_sc), (p0_sc, p1_sc), (a0_sc, a1_sc)
    nchunks = seq // tk
    ntiles = seq // tq
    nrows = nstack * tq

    v_sc[:, 0:LANES] = v_ref[...]
    v_sc[:, LANES:2 * LANES] = jnp.ones((seq, LANES), BF16)

    if alibi:
        rc = (lax.broadcasted_iota(jnp.int32, (tq, tk), 0)
              - lax.broadcasted_iota(jnp.int32, (tq, tk), 1)).astype(F32)
        neg_slope = -slope_ref[g]
        lp = lam_ref[...]
        l1 = jnp.sum(lp[0:1, :] * lp[1:2, :], axis=-1, keepdims=True)
        l2 = jnp.sum(lp[2:3, :] * lp[3:4, :], axis=-1, keepdims=True)
        lam = jnp.exp(l1) - jnp.exp(l2) + lam_init

    def tile_rows(t):
        return pl.ds(t * tq, tq) if isinstance(t, int) else pl.ds(pl.multiple_of(t * tq, tq), tq)

    def chunk_rows(c):
        return pl.ds(c * tk, tk) if isinstance(c, int) else pl.ds(pl.multiple_of(c * tk, tk), tk)

    def load_queries(t, slot):
        for u in range(nstack):
            src = q_ref[tile_rows(t), (u // 2) * LANES:(u // 2 + 1) * LANES]
            keep = low_half if u % 2 == 0 else jnp.logical_not(low_half)
            q_sc[slot, u * tq:(u + 1) * tq, :] = jnp.where(keep, src, jnp.zeros_like(src))

    def scores(t, c, slot, par):
        s = lax.dot_general(q_sc[slot], k_ref[chunk_rows(c), :], (((1,), (1,)), ((), ())),
                            preferred_element_type=F32)
        if alibi:
            base = (t * tq - c * tk).astype(F32) if not (isinstance(t, int) and isinstance(c, int)) \
                else float(t * tq - c * tk)
            bias = neg_slope * jnp.abs(rc + base)
            s = s + jnp.concatenate([bias] * nstack, axis=0)
        s_bufs[par][...] = s

    def softmax(slot, par, first):
        s = s_bufs[par][...]
        m_cur = jnp.max(s, axis=-1, keepdims=True)
        if first:
            m_new = jnp.broadcast_to(m_cur, (nrows, LANES))
        else:
            m_old = m_sc[slot]
            m_new = jnp.maximum(m_old, m_cur)
            a_bufs[par][...] = jnp.exp2(m_old - m_new)
        p_bufs[par][...] = jnp.exp2(s - pltpu.repeat(m_new, tk // LANES, 1)).astype(BF16)
        m_sc[slot] = m_new

    def values(c, slot, par, first):
        d = jnp.dot(p_bufs[par][...], v_sc[chunk_rows(c), :], preferred_element_type=F32)
        if first:
            acc_sc[slot] = d
        else:
            acc_sc[slot] = pltpu.repeat(a_bufs[par][...], 2, 1) * acc_sc[slot] + d

    def finalize(t, slot):
        acc = acc_sc[slot]
        o = acc[:, 0:LANES] * (1.0 / acc[:, LANES:2 * LANES])
        if alibi:
            ob = o[0:tq, :] - lam * o[tq:2 * tq, :]
            ms = jnp.mean(ob * ob, axis=-1, keepdims=True)
            ob = ob * lax.rsqrt(ms + EPS) * subg_ref[...] * (1.0 - lam_init)
            o_ref[tile_rows(t), :] = ob.astype(o_ref.dtype)
        else:
            for pair in range(nstack // 2):
                lo = o[(2 * pair) * tq:(2 * pair + 1) * tq, :]
                hi = o[(2 * pair + 1) * tq:(2 * pair + 2) * tq, :]
                o_ref[tile_rows(t), pair * LANES:(pair + 1) * LANES] = (
                    jnp.where(low_half, lo, hi).astype(o_ref.dtype))

    def step(t, t_next, slot, c):
        static = isinstance(c, int)
        par = c % 2 if static else None
        ahead2 = c + 2
        if static and ahead2 >= nchunks:
            scores(t_next, ahead2 - nchunks, 1 - slot, par)
        else:
            scores(t, ahead2, slot, par)
        if static and c + 1 >= nchunks:
            softmax(1 - slot, 1 - par, first=True)
        else:
            softmax(slot, 1 - par, first=False)
        values(c, slot, par, first=static and c == 0)

    load_queries(0, 0)
    scores(0, 0, 0, 0)
    scores(0, 1, 0, 1)
    softmax(0, 0, first=True)

    def tile_body(t, carry):
        slot = t % 2
        t_next = (t + 1) % ntiles
        load_queries(t_next, 1 - slot)
        lead = min(2, nchunks - 2)
        for c in range(lead):
            step(t, t_next, slot, c)

        def pair_body(j, inner):
            for par in range(2):
                c = 2 * j + par
                scores(t, c + 2, slot, par)
                softmax(slot, 1 - par, first=False)
                values(c, slot, par, first=False)
            return inner

        lax.fori_loop(lead // 2, (nchunks - 2) // 2, pair_body, 0)
        for c in range(nchunks - 2, nchunks):
            step(t, t_next, slot, c)
        finalize(t, slot)
        return carry

    lax.fori_loop(0, ntiles, tile_body, 0)


def _flash_attention(q, k, v, slopes, lam_pack, subg, *, nbatch, seq, ngroups, nstack, alibi, lam_init,
                     nrows=1024):
    qw = (nstack // 2) * LANES
    tq = nrows // nstack
    tk = min(1024, seq // 2)
    kernel = functools.partial(_flash_kernel, tq=tq, tk=tk, seq=seq, nstack=nstack, alibi=alibi,
                               lam_init=lam_init)
    grid_spec = pltpu.PrefetchScalarGridSpec(
        num_scalar_prefetch=1,
        grid=(nbatch, ngroups),
        in_specs=[pl.BlockSpec((None, seq, qw), lambda b, g, s: (b, 0, g)),
                  pl.BlockSpec((None, seq, LANES), lambda b, g, s: (b, 0, g)),
                  pl.BlockSpec((None, seq, LANES), lambda b, g, s: (b, 0, g)),
                  pl.BlockSpec(lam_pack.shape, lambda b, g, s: (0, 0)),
                  pl.BlockSpec(subg.shape, lambda b, g, s: (0, 0))],
        out_specs=pl.BlockSpec((None, seq, qw), lambda b, g, s: (b, 0, g)),
        scratch_shapes=[pltpu.VMEM((2, nrows, LANES), BF16),
                        pltpu.VMEM((seq, 2 * LANES), BF16),
                        pltpu.VMEM((2, nrows, LANES), F32),
                        pltpu.VMEM((2, nrows, 2 * LANES), F32),
                        pltpu.VMEM((nrows, tk), F32), pltpu.VMEM((nrows, tk), F32),
                        pltpu.VMEM((nrows, tk), BF16), pltpu.VMEM((nrows, tk), BF16),
                        pltpu.VMEM((nrows, LANES), F32), pltpu.VMEM((nrows, LANES), F32)])
    return pl.pallas_call(
        kernel,
        grid_spec=grid_spec,
        out_shape=jax.ShapeDtypeStruct((nbatch, seq, ngroups * qw), BF16),
        compiler_params=_cparams(("arbitrary", "arbitrary")),
        name="flash_alibi" if alibi else "flash_gqa",
    )(slopes, q, k, v, lam_pack, subg)


def _natten_kernel(q_ref, k_ref, v_ref, bias_ref, o_ref, v_sc, s0_sc, s1_sc, *, seq):
    lane = lax.broadcasted_iota(jnp.int32, (1, LANES), 1)
    low_half = lane < HEAD_DIM
    ntiles = seq // NAT_TQ
    rows = seq // GRID_W
    s_bufs = (s0_sc, s1_sc)

    v_sc[:, 0:LANES] = v_ref[...]
    v_sc[:, LANES:2 * LANES] = jnp.ones((seq, LANES), BF16)

    def window(t):
        w0 = jnp.clip(t * NAT_QROWS - WIN_H // 2, 0, rows - NAT_KROWS)
        return pl.multiple_of(w0 * GRID_W, NAT_TQ)

    def scores(t, par):
        q = q_ref[pl.ds(pl.multiple_of(t * NAT_TQ, NAT_TQ), NAT_TQ), :]
        zero = jnp.zeros_like(q)
        q2 = jnp.concatenate([jnp.where(low_half, q, zero), jnp.where(low_half, zero, q)], axis=0)
        s = lax.dot_general(q2, k_ref[pl.ds(window(t), NAT_TK), :], (((1,), (1,)), ((), ())),
                            preferred_element_type=F32)
        cls = jnp.where(t == 0, 0, jnp.where(t == ntiles - 1, 2, 1))
        bias = bias_ref[cls].astype(F32).reshape(2 * NAT_TQ, NAT_TK)
        s_bufs[par][...] = s + bias

    def finish(t, par):
        s = s_bufs[par][...]
        m = jnp.max(s, axis=-1, keepdims=True)
        p = jnp.exp2(s - m).astype(BF16)
        acc = jnp.dot(p, v_sc[pl.ds(window(t), NAT_TK), :], preferred_element_type=F32)
        o = acc[:, 0:LANES] * (1.0 / acc[:, LANES:2 * LANES])
        out = jnp.where(low_half, o[0:NAT_TQ, :], o[NAT_TQ:, :])
        o_ref[pl.ds(pl.multiple_of(t * NAT_TQ, NAT_TQ), NAT_TQ), :] = out.astype(o_ref.dtype)

    scores(jnp.int32(0), 0)

    def pair_body(u, carry):
        t = 2 * u
        scores(t + 1, 1)
        finish(t, 0)
        scores((t + 2) % ntiles, 0)
        finish(t + 1, 1)
        return carry

    lax.fori_loop(0, ntiles // 2, pair_body, 0)


def _natten(q, k, v, bias_tab, *, nbatch, seq):
    npairs = C_HEADS // 2
    blk = pl.BlockSpec((None, seq, LANES), lambda p, b: (b, 0, p))
    return pl.pallas_call(
        functools.partial(_natten_kernel, seq=seq),
        grid=(npairs, nbatch),
        in_specs=[blk, blk, blk,
                  pl.BlockSpec((3, None, 2, NAT_TQ, NAT_TK), lambda p, b: (0, p, 0, 0, 0))],
        out_specs=blk,
        out_shape=jax.ShapeDtypeStruct((nbatch, seq, C_HEADS * HEAD_DIM), BF16),
        scratch_shapes=[pltpu.VMEM((seq, 2 * LANES), BF16),
                        pltpu.VMEM((2 * NAT_TQ, NAT_TK), F32), pltpu.VMEM((2 * NAT_TQ, NAT_TK), F32)],
        compiler_params=_cparams(("arbitrary", "arbitrary")),
        name="natten",
    )(q, k, v, bias_tab)


def _natten_bias_table(rpb, seq):
    rows = seq // GRID_W
    ntiles = rows // NAT_QROWS
    col = jnp.arange(GRID_W)
    cstart = jnp.clip(col - WIN_W // 2, 0, GRID_W - WIN_W)
    col_valid = (col[None, :] >= cstart[:, None]) & (col[None, :] < cstart[:, None] + WIN_W)
    dc_idx = jnp.clip(col[None, :] - col[:, None] + WIN_W - 1, 0, 2 * WIN_W - 2)
    rpb_cols = rpb[:, :, dc_idx]
    tabs = []
    for tile in (0, 1, ntiles - 1):
        r = tile * NAT_QROWS + jnp.arange(NAT_QROWS)
        w0 = int(np.clip(tile * NAT_QROWS - WIN_H // 2, 0, rows - NAT_KROWS))
        kr = w0 + jnp.arange(NAT_KROWS)
        rstart = jnp.clip(r - WIN_H // 2, 0, rows - WIN_H)
        row_valid = (kr[None, :] >= rstart[:, None]) & (kr[None, :] < rstart[:, None] + WIN_H)
        dr_idx = jnp.clip(kr[None, :] - r[:, None] + WIN_H - 1, 0, 2 * WIN_H - 2)
        pick = (dr_idx[:, :, None] == jnp.arange(2 * WIN_H - 1)[None, None, :]).astype(F32)
        b = jnp.einsum("qkd,hdcx->hqckx", pick, rpb_cols * LOG2E,
                       precision=lax.Precision.HIGHEST)
        valid = row_valid[:, None, :, None] & col_valid[None, :, None, :]
        b = jnp.where(valid[None], b, -jnp.inf)
        tabs.append(b.reshape(C_HEADS, NAT_TQ, NAT_TK))
    return jnp.stack(tabs).reshape(3, C_HEADS // 2, 2, NAT_TQ, NAT_TK).astype(BF16)


def _ffn_kernel(x_ref, g1_ref, sh_ref, sc_ref, g2_ref, ng_ref, ma_ref, mb_ref, woa_ref, wob_ref,
                w1_ref, w3_ref, w2_ref, o_ref):
    y = (jnp.dot(ma_ref[...], woa_ref[...], preferred_element_type=F32)
         + jnp.dot(mb_ref[...], wob_ref[...], preferred_element_type=F32))
    x1 = x_ref[...] + g1_ref[...] * y
    h = _norm_mod(x1, ng_ref[...], sh_ref[...], sc_ref[...]).astype(BF16)
    y = None
    for f in range(D_FF // FFN_CHUNK):
        cols = slice(f * FFN_CHUNK, (f + 1) * FFN_CHUNK)
        a = jnp.dot(h, w1_ref[:, cols], preferred_element_type=F32)
        b = jnp.dot(h, w3_ref[:, cols], preferred_element_type=F32)
        d = jnp.dot((a * _sigmoid(a) * b).astype(BF16), w2_ref[cols, :], preferred_element_type=F32)
        y = d if y is None else y + d
    o_ref[...] = x1 + g2_ref[...] * y


def _ffn(x2, mod, norm_g, mixes, wos, w1, w3, w2, seq, tm=512):
    t = x2.shape[0]
    row = lambda i: (i, 0)
    resident = lambda a: pl.BlockSpec(a.shape, lambda i: (0, 0), pipeline_mode=pl.Buffered(1))
    return pl.pallas_call(
        _ffn_kernel,
        grid=(t // tm,),
        in_specs=[pl.BlockSpec((tm, D_MODEL), row),
                  _mod_spec(2, tm, seq), _mod_spec(3, tm, seq), _mod_spec(4, tm, seq), _mod_spec(5, tm, seq),
                  pl.BlockSpec((1, D_MODEL), lambda i: (0, 0)),
                  pl.BlockSpec((tm, mixes[0].shape[1]), row), pl.BlockSpec((tm, mixes[1].shape[1]), row),
                  resident(wos[0]), resident(wos[1]), resident(w1), resident(w3), resident(w2)],
        out_specs=pl.BlockSpec((tm, D_MODEL), row),
        out_shape=jax.ShapeDtypeStruct((t, D_MODEL), F32),
        compiler_params=_cparams(("arbitrary",)),
        name="outproj_ffn_swiglu",
    )(x2, mod, mod, mod, mod, norm_g.reshape(1, D_MODEL), mixes[0], mixes[1], wos[0], wos[1], w1, w3, w2)


def _router_kernel(x_ref, g1_ref, sh_ref, sc_ref, g_ref, mix_ref, wo_ref, rw_ref, rb_ref,
                   x1_ref, h_ref, meta_ref):
    lane = lax.broadcasted_iota(jnp.int32, (1, LANES), 1).astype(F32)
    nsub = 4
    rb = x_ref.shape[0] // nsub
    for r in range(nsub):
        rows = slice(r * rb, (r + 1) * rb)
        x1 = x_ref[rows, :] + g1_ref[...] * jnp.dot(mix_ref[rows, :], wo_ref[...], preferred_element_type=F32)
        x1_ref[rows, :] = x1
        h = _norm_mod(x1, g_ref[...], sh_ref[...], sc_ref[...])
        packed = _pack_halves(h[:, 0:PACK_W], h[:, PACK_W:D_MODEL])
        for s in range(SC_PIECES):
            h_ref[s, rows, :] = packed[:, s * SC_ROW_WORDS:(s + 1) * SC_ROW_WORDS]
        logits = jnp.dot(h, rw_ref[...], preferred_element_type=F32,
                         precision=lax.Precision.HIGHEST) + rb_ref[...]
        logits = jnp.where(lane < N_EXPERTS, logits, -jnp.inf)
        m1 = jnp.max(logits, axis=-1, keepdims=True)
        i1 = jnp.min(jnp.where(logits == m1, lane, float(LANES)), axis=-1, keepdims=True)
        rest = jnp.where(lane == i1, -jnp.inf, logits)
        m2 = jnp.max(rest, axis=-1, keepdims=True)
        i2 = jnp.min(jnp.where(rest == m2, lane, float(LANES)), axis=-1, keepdims=True)
        e = jnp.exp(m2 - m1)
        g1 = 1.0 / (1.0 + e)
        g2 = e * g1
        meta_ref[rows, :] = jnp.where(lane == 0, i1, jnp.where(lane == 1, i2, jnp.where(
            lane == 2, g1, jnp.where(lane == 3, g2, 0.0))))


def _router(x2, mod, norm_g, mix, wo, rw_pad, rb_pad, seq, tm=512):
    t = x2.shape[0]
    row = lambda i: (i, 0)
    const = lambda i: (0, 0)
    return pl.pallas_call(
        _router_kernel,
        grid=(t // tm,),
        in_specs=[pl.BlockSpec((tm, D_MODEL), row),
                  _mod_spec(2, tm, seq), _mod_spec(3, tm, seq), _mod_spec(4, tm, seq),
                  pl.BlockSpec((1, D_MODEL), const),
                  pl.BlockSpec((tm, D_MODEL), row), pl.BlockSpec((D_MODEL, D_MODEL), const),
                  pl.BlockSpec((D_MODEL, LANES), const), pl.BlockSpec((1, LANES), const)],
        out_specs=[pl.BlockSpec((tm, D_MODEL), row),
                   pl.BlockSpec((SC_PIECES, tm, SC_ROW_WORDS), lambda i: (0, i, 0)),
                   pl.BlockSpec((tm, LANES), row)],
        out_shape=[jax.ShapeDtypeStruct((t, D_MODEL), F32),
                   jax.ShapeDtypeStruct((SC_PIECES, t, SC_ROW_WORDS), jnp.int32),
                   jax.ShapeDtypeStruct((t, LANES), F32)],
        compiler_params=_cparams(("arbitrary",)),
        name="outproj_moe_router",
    )(x2, mod, mod, mod, norm_g.reshape(1, D_MODEL), mix, wo, rw_pad, rb_pad)


def _moe_kernel(te_ref, nt_ref, x_ref, w1_ref, w3_ref, w2a_ref, w2b_ref, o_ref, x_sc, g_sc, *, nf, tf):
    i = pl.program_id(0)
    j = pl.program_id(1)
    active = i < nt_ref[0]

    @pl.when(jnp.logical_and(active, j == 0))
    def _():
        x_sc[...] = _unpack_halves(_load_pieces(x_ref)).astype(BF16)

    @pl.when(jnp.logical_and(active, j < nf))
    def _():
        x = x_sc[...]
        a = jnp.dot(x, w1_ref[...], preferred_element_type=F32)
        b = jnp.dot(x, w3_ref[...], preferred_element_type=F32)
        g_sc[j] = (a * _sigmoid(a) * b).astype(BF16)

    @pl.when(jnp.logical_and(active, j >= nf))
    def _():
        ya, yb = None, None
        for f in range(nf):
            g = g_sc[f]
            da = jnp.dot(g, w2a_ref[f * tf:(f + 1) * tf, :], preferred_element_type=F32)
            db = jnp.dot(g, w2b_ref[f * tf:(f + 1) * tf, :], preferred_element_type=F32)
            ya = da if ya is None else ya + da
            yb = db if yb is None else yb + db
        o_ref[...] = _pack_halves(ya, yb)

    @pl.when(jnp.logical_and(jnp.logical_not(active), j >= nf))
    def _():
        o_ref[...] = jnp.zeros(o_ref.shape, o_ref.dtype)


def _moe_grouped(xs, tile_expert, num_tiles, w1, w3, w2, tg, tf=512):
    p = xs.shape[1]
    nf = w1.shape[2] // tf
    tn = SC_ROW_WORDS
    nb = SC_PIECES
    fcl = lambda j: jnp.minimum(j, nf - 1)
    ncl = lambda j: jnp.maximum(j - nf, 0)
    grid_spec = pltpu.PrefetchScalarGridSpec(
        num_scalar_prefetch=2,
        grid=(p // tg, nf + nb),
        in_specs=[pl.BlockSpec((SC_PIECES, tg, SC_ROW_WORDS), lambda i, j, te, nt: (0, i, 0)),
                  pl.BlockSpec((None, D_MODEL, tf), lambda i, j, te, nt: (te[i], 0, fcl(j))),
                  pl.BlockSpec((None, D_MODEL, tf), lambda i, j, te, nt: (te[i], 0, fcl(j))),
                  pl.BlockSpec((None, nf * tf, tn), lambda i, j, te, nt: (te[i], 0, ncl(j))),
                  pl.BlockSpec((None, nf * tf, tn), lambda i, j, te, nt: (te[i], 0, nb + ncl(j)))],
        out_specs=pl.BlockSpec((None, tg, tn), lambda i, j, te, nt: (ncl(j), i, 0)),
        scratch_shapes=[pltpu.VMEM((tg, D_MODEL), BF16), pltpu.VMEM((nf, tg, tf), BF16)])
    return pl.pallas_call(
        functools.partial(_moe_kernel, nf=nf, tf=tf),
        grid_spec=grid_spec,
        out_shape=jax.ShapeDtypeStruct((SC_PIECES, p, SC_ROW_WORDS), jnp.int32),
        compiler_params=_cparams(("arbitrary", "arbitrary")),
        name="moe_grouped",
    )(tile_expert, num_tiles, xs, w1, w3, w2, w2)


def _combine_kernel(x_ref, gate_ref, meta_ref, y1_ref, y2_ref, o_ref):
    meta = meta_ref[...]
    moe = (meta[:, 2:3] * _unpack_halves(_load_pieces(y1_ref))
           + meta[:, 3:4] * _unpack_halves(_load_pieces(y2_ref)))
    o_ref[...] = x_ref[...] + gate_ref[...] * moe


def _combine(x2, mod, meta, yg, seq, tm=512):
    t = x2.shape[0]
    nt = t // tm
    row = lambda i: (i, 0)
    return pl.pallas_call(
        _combine_kernel,
        grid=(nt,),
        in_specs=[pl.BlockSpec((tm, D_MODEL), row), _mod_spec(5, tm, seq),
                  pl.BlockSpec((tm, LANES), row),
                  pl.BlockSpec((SC_PIECES, tm, SC_ROW_WORDS), lambda i: (0, i, 0)),
                  pl.BlockSpec((SC_PIECES, tm, SC_ROW_WORDS), lambda i: (0, nt + i, 0))],
        out_specs=pl.BlockSpec((tm, D_MODEL), row),
        out_shape=jax.ShapeDtypeStruct((t, D_MODEL), F32),
        compiler_params=_cparams(("arbitrary",)),
        name="moe_combine",
    )(x2, mod, meta, yg, yg)


def _sc_gather(table, idx):
    n = idx.shape[0]
    nrows = table.shape[1]
    pieces = n * SC_PIECES
    idx_pieces = jnp.concatenate([idx + s * nrows for s in range(SC_PIECES)]).reshape(1, pieces)
    mesh = plsc.VectorSubcoreMesh(core_axis_name="core", subcore_axis_name="subcore")

    @pl.kernel(out_type=jax.ShapeDtypeStruct((pieces, SC_ROW_WORDS), table.dtype), mesh=mesh, scratch_types=[])
    def gather_kernel(table_hbm, idx_hbm, out_hbm):
        def body(idx_vmem, out_vmem):
            pltpu.sync_copy(table_hbm.at[idx_vmem.at[0]], out_vmem)

        pltpu.emit_pipeline(
            body,
            grid=(pieces // SC_WINDOW,),
            in_specs=[pl.BlockSpec((1, SC_WINDOW), lambda i: (0, i))],
            out_specs=[pl.BlockSpec((SC_WINDOW, SC_ROW_WORDS), lambda i: (i, 0))],
            core_axis_name=("core", "subcore"),
            dimension_semantics=(pltpu.PARALLEL,),
        )(idx_hbm, out_hbm)

    out = gather_kernel(table.reshape(SC_PIECES * nrows, SC_ROW_WORDS), idx_pieces)
    return out.reshape(SC_PIECES, n, SC_ROW_WORDS)


def _moe(x2, mod, norm_g, mix, wo, rw_pad, rb_pad, w1, w3, w2, seq, tg=1024):
    t = x2.shape[0]
    x2, h, meta = _router(x2, mod, norm_g, mix, wo, rw_pad, rb_pad, seq)
    e_flat = meta[:, 0:2].astype(jnp.int32).reshape(-1)
    onehot = (e_flat[:, None] == jnp.arange(N_EXPERTS)[None, :]).astype(jnp.int32)
    csum = jnp.cumsum(onehot, axis=0)
    counts = csum[-1]
    rank = jnp.take_along_axis(csum, e_flat[:, None], axis=1)[:, 0] - 1
    padded = ((counts + tg - 1) // tg) * tg
    pend = jnp.cumsum(padded)
    pos = (pend - padded)[e_flat] + rank
    p_rows = 2 * t + N_EXPERTS * tg
    row_token = jnp.zeros((p_rows,), jnp.int32).at[pos].set(jnp.arange(2 * t, dtype=jnp.int32) // 2)
    tile_start = jnp.arange(p_rows // tg, dtype=jnp.int32) * tg
    tile_expert = jnp.minimum(jnp.sum((tile_start[:, None] >= pend[None, :]).astype(jnp.int32), axis=1),
                              N_EXPERTS - 1)
    num_tiles = (pend[-1] // tg).astype(jnp.int32).reshape(1)
    xs = _sc_gather(h, row_token)
    ys = _moe_grouped(xs, tile_expert, num_tiles, w1, w3, w2, tg)
    yg = _sc_gather(ys, jnp.concatenate([pos[0::2], pos[1::2]]))
    return _combine(x2, mod, meta, yg, seq)


def _rope_tables(seq):
    t = np.arange(seq)
    lane = np.arange(LANES)
    d = lane % HEAD_DIM
    pos = np.where((d // 32)[None, :] == 0, (t // GRID_W)[:, None], (t % GRID_W)[:, None]).astype(np.float32)
    inv = (ROPE_THETA ** (-np.arange(16, dtype=np.float32) / 16)).astype(np.float32)
    ang = pos * inv[(d % 16)][None, :]
    return jnp.asarray(np.cos(ang), F32), jnp.asarray(np.sin(ang), F32)


def _pair_gain(g):
    return jnp.concatenate([g, g]).astype(F32)


def _prepare(p, seq):
    even, odd = {}, {}
    w_in = p["w_in_even"][0]
    qa, ka, va, qb, kb, vb = jnp.split(w_in, [512, 640, 768, 1280, 1792], axis=1)
    dup = lambda w: jnp.concatenate([w[:, 0:64], w[:, 0:64], w[:, 64:128], w[:, 64:128]], axis=1)
    even["w_in"] = jnp.concatenate([qa, qb, dup(ka), dup(va), kb, vb], axis=1).astype(BF16)
    gains = jnp.zeros((8, LANES), F32)
    gains = gains.at[0].set(_pair_gain(p["qnorm_a"][0])).at[1].set(_pair_gain(p["knorm_a"][0]))
    gains = gains.at[2].set(_pair_gain(p["qnorm_b"][0])).at[3].set(_pair_gain(p["knorm_b"][0]))
    even["gains"] = gains
    qscale = SCALE * LOG2E
    even["groups"] = ((512, 0, True, qscale), (512, 2, False, qscale), (256, 1, True, 1.0),
                      (256, None, False, 1.0), (512, 3, False, 1.0), (512, None, False, 1.0))
    lam = jnp.zeros((8, LANES), F32)
    for r, name in enumerate(("lam_q1", "lam_k1", "lam_q2", "lam_k2")):
        lam = lam.at[r, 0:HEAD_DIM].set(p[name][0])
    even["lam"] = lam
    even["subg"] = p["subln_b"][0].reshape(1, LANES).astype(F32)
    wo = p["w_out_even"][0].astype(BF16)
    even["wo"] = (wo[0:512], wo[512:1024])
    even["slopes"] = jnp.asarray(LOG2E * 2.0 ** (-8.0 * (np.arange(B_HEADS) + 1.0) / B_HEADS), F32)
    for name in ("ffn_w1", "ffn_w3", "ffn_w2"):
        even[name] = p[name][0].astype(BF16)
    for name in ("ada_w", "ada_b", "norm_mix", "norm_ffn"):
        even[name] = p[name + "_even"][0]
        odd[name] = p[name + "_odd"][0]

    odd["w_qkv"] = p["w_qkv_odd"][0].astype(BF16)
    gains = jnp.zeros((8, LANES), F32)
    odd["gains"] = gains.at[0].set(_pair_gain(p["qnorm_c"][0])).at[1].set(_pair_gain(p["knorm_c"][0]))
    odd["groups"] = ((1024, 0, False, qscale), (1024, 1, False, 1.0), (1024, None, False, 1.0))
    odd["bias_tab"] = _natten_bias_table(p["rpb_c"][0], seq)
    odd["wo"] = (p["w_out_odd"][0].astype(BF16),)
    odd["rw"] = jnp.zeros((D_MODEL, LANES), F32).at[:, 0:N_EXPERTS].set(p["router_w"][0])
    odd["rb"] = jnp.zeros((1, LANES), F32).at[0, 0:N_EXPERTS].set(p["router_b"][0])
    for name in ("moe_w1", "moe_w3", "moe_w2"):
        odd[name] = p[name][0].astype(BF16)
    cos_t, sn_t = _rope_tables(seq)
    gmat = jnp.asarray(np.kron(np.eye(2), np.full((HEAD_DIM, HEAD_DIM), 1.0 / HEAD_DIM)), BF16)
    return even, odd, (cos_t, sn_t, gmat)


def _even_layer(x2, c, ev, shared, nbatch, seq):
    cos_t, sn_t, gmat = shared
    mod = _ada_modulation(c, ev["ada_w"], ev["ada_b"])
    qa, qb, ka, va, kb, vb = _projection(x2, mod, ev["norm_mix"], ev["w_in"], cos_t, sn_t, ev["gains"], gmat,
                                         ev["groups"], seq)
    shp = lambda a: a.reshape(nbatch, seq, a.shape[1])
    lam_init = 0.8 - 0.6 * math.exp(-0.3 * 0)
    mix_a = _flash_attention(shp(qa), shp(ka), shp(va), ev["slopes"], ev["lam"], ev["subg"], nbatch=nbatch, seq=seq,
                             ngroups=A_KV_HEADS, nstack=4, alibi=False, lam_init=lam_init)
    mix_b = _flash_attention(shp(qb), shp(kb), shp(vb), ev["slopes"], ev["lam"], ev["subg"], nbatch=nbatch, seq=seq,
                             ngroups=B_HEADS, nstack=2, alibi=True, lam_init=lam_init)
    t = nbatch * seq
    return _ffn(x2, mod, ev["norm_ffn"], (mix_a.reshape(t, -1), mix_b.reshape(t, -1)), ev["wo"],
                ev["ffn_w1"], ev["ffn_w3"], ev["ffn_w2"], seq)


def _odd_layer(x2, c, od, shared, nbatch, seq):
    cos_t, sn_t, gmat = shared
    mod = _ada_modulation(c, od["ada_w"], od["ada_b"])
    q, k, v = _projection(x2, mod, od["norm_mix"], od["w_qkv"], cos_t, sn_t, od["gains"], gmat, od["groups"], seq)
    shp = lambda a: a.reshape(nbatch, seq, a.shape[1])
    mix = _natten(shp(q), shp(k), shp(v), od["bias_tab"], nbatch=nbatch, seq=seq)
    return _moe(x2, mod, od["norm_ffn"], mix.reshape(nbatch * seq, -1), od["wo"][0], od["rw"], od["rb"],
                od["moe_w1"], od["moe_w3"], od["moe_w2"], seq)


def _trunk(x, c, ev, od, shared):
    nbatch, seq, d = x.shape
    x2 = x.reshape(nbatch * seq, d)
    x2 = _even_layer(x2, c, ev, shared, nbatch, seq)
    x2 = _odd_layer(x2, c, od, shared, nbatch, seq)
    return x2.reshape(nbatch, seq, d)


def kernel(x_prompt, x_sample, c_prompt, c_sample, ada_w_even, ada_b_even, norm_mix_even, norm_ffn_even, w_in_even, qnorm_a, knorm_a, qnorm_b, knorm_b, lam_q1, lam_k1, lam_q2, lam_k2, subln_b, w_out_even, ffn_w1, ffn_w3, ffn_w2, ada_w_odd, ada_b_odd, norm_mix_odd, norm_ffn_odd, w_qkv_odd, qnorm_c, knorm_c, rpb_c, w_out_odd, router_w, router_b, moe_w1, moe_w3, moe_w2):
    params = dict(ada_w_even=ada_w_even, ada_b_even=ada_b_even, norm_mix_even=norm_mix_even,
                  norm_ffn_even=norm_ffn_even, w_in_even=w_in_even, qnorm_a=qnorm_a, knorm_a=knorm_a,
                  qnorm_b=qnorm_b, knorm_b=knorm_b, lam_q1=lam_q1, lam_k1=lam_k1, lam_q2=lam_q2, lam_k2=lam_k2,
                  subln_b=subln_b, w_out_even=w_out_even, ffn_w1=ffn_w1, ffn_w3=ffn_w3, ffn_w2=ffn_w2,
                  ada_w_odd=ada_w_odd, ada_b_odd=ada_b_odd, norm_mix_odd=norm_mix_odd, norm_ffn_odd=norm_ffn_odd,
                  w_qkv_odd=w_qkv_odd, qnorm_c=qnorm_c, knorm_c=knorm_c, rpb_c=rpb_c, w_out_odd=w_out_odd,
                  router_w=router_w, router_b=router_b, moe_w1=moe_w1, moe_w3=moe_w3, moe_w2=moe_w2)
    seq = x_prompt.shape[1]
    ev, od, shared = _prepare(params, seq)
    y_prompt = _trunk(x_prompt, c_prompt, ev, od, shared)
    y_sample = _trunk(x_sample, c_sample, ev, od, shared)
    return (y_prompt, y_sample)
```

```python
import functools
import math

import numpy as np
import jax
import jax.numpy as jnp
from jax import lax
from jax.experimental import pallas as pl
from jax.experimental.pallas import tpu as pltpu
from jax.experimental.pallas import tpu_sc as plsc

F32 = jnp.float32
BF16 = jnp.bfloat16

D_MODEL = 1024
HEAD_DIM = 64
LANES = 128
SCALE = HEAD_DIM ** -0.5
LOG2E = 1.4426950408889634
GRID_W = 64
EPS = 1e-6
ROPE_THETA = 10000.0
A_Q_HEADS = 8
A_KV_HEADS = 2
B_HEADS = 4
C_HEADS = 16
WIN_H = 8
WIN_W = 16
N_EXPERTS = 8
D_FF = 2816
D_FF_EXPERT = 3584
VMEM_LIMIT = 56 * 1024 * 1024

NAT_QROWS = 4
NAT_KROWS = 12
NAT_TQ = NAT_QROWS * GRID_W
NAT_TK = NAT_KROWS * GRID_W
PACK_W = D_MODEL // 2
SC_WINDOW = 128
SC_ROW_WORDS = 256
SC_PIECES = PACK_W // SC_ROW_WORDS
FFN_CHUNK = 1408


def _cparams(sem):
    return pltpu.CompilerParams(dimension_semantics=sem, vmem_limit_bytes=VMEM_LIMIT)


def _norm_mod(x, g, shift, scale):
    ms = jnp.mean(x * x, axis=-1, keepdims=True)
    y = x * lax.rsqrt(ms + EPS) * g
    return y * (1.0 + scale) + shift


def _head_norm(x, gain, gmat):
    ms = jnp.dot((x * x).astype(BF16), gmat, preferred_element_type=F32)
    return x * lax.rsqrt(ms + EPS) * gain


def _rope(x, cos, sn, first_quarter):
    up = pltpu.roll(x, LANES - 16, 1)
    down = pltpu.roll(x, 16, 1)
    return x * cos + sn * jnp.where(first_quarter, -up, down)


def _sigmoid(a):
    return 1.0 / (1.0 + jnp.exp(-a))


def _pack_halves(a, b):
    hi = lax.bitcast_convert_type(a.astype(BF16).astype(F32), jnp.int32)
    lo = lax.bitcast_convert_type(b.astype(BF16).astype(F32), jnp.int32)
    return hi | lax.shift_right_logical(lo, jnp.full_like(lo, 16))


def _unpack_halves(w):
    hi = lax.bitcast_convert_type(w & jnp.int32(-65536), F32)
    lo = lax.bitcast_convert_type(lax.shift_left(w, jnp.full_like(w, 16)), F32)
    return jnp.concatenate([hi, lo], axis=1)


def _load_pieces(ref):
    return jnp.concatenate([ref[s] for s in range(SC_PIECES)], axis=1)


def _ada_kernel(c_ref, w_ref, b_ref, o_ref):
    c = c_ref[...]
    s = c * _sigmoid(c)
    o_ref[...] = jnp.dot(s, w_ref[...], preferred_element_type=F32,
                         precision=lax.Precision.HIGHEST) + b_ref[...]


def _ada_modulation(c, w, b):
    nb, d = c.shape
    n = w.shape[1]
    tn = 512
    mod = pl.pallas_call(
        _ada_kernel,
        grid=(n // tn,),
        in_specs=[pl.BlockSpec((nb, d), lambda j: (0, 0)),
                  pl.BlockSpec((d, tn), lambda j: (0, j)),
                  pl.BlockSpec((1, tn), lambda j: (0, j))],
        out_specs=pl.BlockSpec((nb, tn), lambda j: (0, j)),
        out_shape=jax.ShapeDtypeStruct((nb, n), F32),
        compiler_params=_cparams(("arbitrary",)),
        name="ada_mod",
    )(c, w, b.reshape(1, n))
    return mod.reshape(nb, 6, 1, d)


def _mod_spec(k, tm, seq):
    return pl.BlockSpec((None, None, 1, D_MODEL), lambda i, *_: ((i * tm) // seq, k, 0, 0))


def _proj_kernel(x_ref, sh_ref, sc_ref, g_ref, w_ref, cos_ref, sn_ref, gains_ref, gmat_ref, *o_refs,
                 groups):
    h = _norm_mod(x_ref[...], g_ref[...], sh_ref[...], sc_ref[...])
    y = jnp.dot(h.astype(BF16), w_ref[...], preferred_element_type=F32)
    lane = lax.broadcasted_iota(jnp.int32, (1, LANES), 1)
    first_quarter = (lane % 32) < 16
    gmat = gmat_ref[...]
    off = 0
    for o_ref, (width, gain_row, rope, mult) in zip(o_refs, groups):
        if gain_row is None:
            o_ref[...] = y[:, off:off + width].astype(o_ref.dtype)
        else:
            gain = gains_ref[gain_row:gain_row + 1, :]
            for t in range(width // LANES):
                z = _head_norm(y[:, off + t * LANES: off + (t + 1) * LANES], gain, gmat)
                if rope:
                    z = _rope(z, cos_ref[...], sn_ref[...], first_quarter)
                if mult != 1.0:
                    z = z * mult
                o_ref[:, t * LANES:(t + 1) * LANES] = z.astype(o_ref.dtype)
        off += width


def _projection(x2, mod, norm_g, w, cos_t, sn_t, gains, gmat, groups, seq, tm=512):
    t = x2.shape[0]
    n = w.shape[1]
    nseq = seq // tm
    row = lambda i: (i, 0)
    const = lambda i: (0, 0)
    tab = lambda i: (i % nseq, 0)
    return pl.pallas_call(
        functools.partial(_proj_kernel, groups=groups),
        grid=(t // tm,),
        in_specs=[pl.BlockSpec((tm, D_MODEL), row),
                  _mod_spec(0, tm, seq), _mod_spec(1, tm, seq),
                  pl.BlockSpec((1, D_MODEL), const),
                  pl.BlockSpec((D_MODEL, n), const),
                  pl.BlockSpec((tm, LANES), tab), pl.BlockSpec((tm, LANES), tab),
                  pl.BlockSpec(gains.shape, const),
                  pl.BlockSpec((LANES, LANES), const)],
        out_specs=[pl.BlockSpec((tm, g[0]), row) for g in groups],
        out_shape=[jax.ShapeDtypeStruct((t, g[0]), BF16) for g in groups],
        compiler_params=_cparams(("arbitrary",)),
        name="norm_mod_proj",
    )(x2, mod, mod, norm_g.reshape(1, D_MODEL), w, cos_t, sn_t, gains, gmat)


def _flash_kernel(slope_ref, q_ref, k_ref, v_ref, lam_ref, subg_ref, o_ref,
                  q_sc, v_sc, m_sc, acc_sc, s0_sc, s1_sc, p0_sc, p1_sc, a0_sc, a1_sc,
                  *, tq, tk, seq, nstack, alibi, lam_init):
    g = pl.program_id(1)
    lane = lax.broadcasted_iota(jnp.int32, (1, LANES), 1)
    low_half = lane < HEAD_DIM
    s_bufs, p_bufs, a_bufs = (s0_sc, s1_sc), (p0_sc, p1_sc), (a0_sc, a1_sc)
    nchunks = seq // tk
    ntiles = seq // tq
    nrows = nstack * tq

    v_sc[:, 0:LANES] = v_ref[...]
    v_sc[:, LANES:2 * LANES] = jnp.ones((seq, LANES), BF16)

    if alibi:
        rc = (lax.broadcasted_iota(jnp.int32, (tq, tk), 0)
              - lax.broadcasted_iota(jnp.int32, (tq, tk), 1)).astype(F32)
        neg_slope = -slope_ref[g]
        lp = lam_ref[...]
        l1 = jnp.sum(lp[0:1, :] * lp[1:2, :], axis=-1, keepdims=True)
        l2 = jnp.sum(lp[2:3, :] * lp[3:4, :], axis=-1, keepdims=True)
        lam = jnp.exp(l1) - jnp.exp(l2) + lam_init

    def tile_rows(t):
        return pl.ds(t * tq, tq) if isinstance(t, int) else pl.ds(pl.multiple_of(t * tq, tq), tq)

    def chunk_rows(c):
        return pl.ds(c * tk, tk) if isinstance(c, int) else pl.ds(pl.multiple_of(c * tk, tk), tk)

    def load_queries(t, slot):
        for u in range(nstack):
            src = q_ref[tile_rows(t), (u // 2) * LANES:(u // 2 + 1) * LANES]
            keep = low_half if u % 2 == 0 else jnp.logical_not(low_half)
            q_sc[slot, u * tq:(u + 1) * tq, :] = jnp.where(keep, src, jnp.zeros_like(src))

    def scores(t, c, slot, par):
        s = lax.dot_general(q_sc[slot], k_ref[chunk_rows(c), :], (((1,), (1,)), ((), ())),
                            preferred_element_type=F32)
        if alibi:
            base = (t * tq - c * tk).astype(F32) if not (isinstance(t, int) and isinstance(c, int)) \
                else float(t * tq - c * tk)
            bias = neg_slope * jnp.abs(rc + base)
            s = s + jnp.concatenate([bias] * nstack, axis=0)
        s_bufs[par][...] = s

    def softmax(slot, par, first):
        s = s_bufs[par][...]
        m_cur = jnp.max(s, axis=-1, keepdims=True)
        if first:
            m_new = jnp.broadcast_to(m_cur, (nrows, LANES))
        else:
            m_old = m_sc[slot]
            m_new = jnp.maximum(m_old, m_cur)
            a_bufs[par][...] = jnp.exp2(m_old - m_new)
        p_bufs[par][...] = jnp.exp2(s - pltpu.repeat(m_new, tk // LANES, 1)).astype(BF16)
        m_sc[slot] = m_new

    def values(c, slot, par, first):
        d = jnp.dot(p_bufs[par][...], v_sc[chunk_rows(c), :], preferred_element_type=F32)
        if first:
            acc_sc[slot] = d
        else:
            acc_sc[slot] = pltpu.repeat(a_bufs[par][...], 2, 1) * acc_sc[slot] + d

    def finalize(t, slot):
        acc = acc_sc[slot]
        o = acc[:, 0:LANES] * (1.0 / acc[:, LANES:2 * LANES])
        if alibi:
            ob = o[0:tq, :] - lam * o[tq:2 * tq, :]
            ms = jnp.mean(ob * ob, axis=-1, keepdims=True)
            ob = ob * lax.rsqrt(ms + EPS) * subg_ref[...] * (1.0 - lam_init)
            o_ref[tile_rows(t), :] = ob.astype(o_ref.dtype)
        else:
            for pair in range(nstack // 2):
                lo = o[(2 * pair) * tq:(2 * pair + 1) * tq, :]
                hi = o[(2 * pair + 1) * tq:(2 * pair + 2) * tq, :]
                o_ref[tile_rows(t), pair * LANES:(pair + 1) * LANES] = (
                    jnp.where(low_half, lo, hi).astype(o_ref.dtype))

    def step(t, t_next, slot, c):
        static = isinstance(c, int)
        par = c % 2 if static else None
        ahead2 = c + 2
        if static and ahead2 >= nchunks:
            scores(t_next, ahead2 - nchunks, 1 - slot, par)
        else:
            scores(t, ahead2, slot, par)
        if static and c + 1 >= nchunks:
            softmax(1 - slot, 1 - par, first=True)
        else:
            softmax(slot, 1 - par, first=False)
        values(c, slot, par, first=static and c == 0)

    load_queries(0, 0)
    scores(0, 0, 0, 0)
    scores(0, 1, 0, 1)
    softmax(0, 0, first=True)

    def tile_body(t, carry):
        slot = t % 2
        t_next = (t + 1) % ntiles
        load_queries(t_next, 1 - slot)
        lead = min(2, nchunks - 2)
        for c in range(lead):
            step(t, t_next, slot, c)

        def pair_body(j, inner):
            for par in range(2):
                c = 2 * j + par
                scores(t, c + 2, slot, par)
                softmax(slot, 1 - par, first=False)
                values(c, slot, par, first=False)
            return inner

        lax.fori_loop(lead // 2, (nchunks - 2) // 2, pair_body, 0)
        for c in range(nchunks - 2, nchunks):
            step(t, t_next, slot, c)
        finalize(t, slot)
        return carry

    lax.fori_loop(0, ntiles, tile_body, 0)


def _flash_attention(q, k, v, slopes, lam_pack, subg, *, nbatch, seq, ngroups, nstack, alibi, lam_init,
                     nrows=1024):
    qw = (nstack // 2) * LANES
    tq = nrows // nstack
    tk = min(1024, seq // 2)
    kernel = functools.partial(_flash_kernel, tq=tq, tk=tk, seq=seq, nstack=nstack, alibi=alibi,
                               lam_init=lam_init)
    grid_spec = pltpu.PrefetchScalarGridSpec(
        num_scalar_prefetch=1,
        grid=(nbatch, ngroups),
        in_specs=[pl.BlockSpec((None, seq, qw), lambda b, g, s: (b, 0, g)),
                  pl.BlockSpec((None, seq, LANES), lambda b, g, s: (b, 0, g)),
                  pl.BlockSpec((None, seq, LANES), lambda b, g, s: (b, 0, g)),
                  pl.BlockSpec(lam_pack.shape, lambda b, g, s: (0, 0)),
                  pl.BlockSpec(subg.shape, lambda b, g, s: (0, 0))],
        out_specs=pl.BlockSpec((None, seq, qw), lambda b, g, s: (b, 0, g)),
        scratch_shapes=[pltpu.VMEM((2, nrows, LANES), BF16),
                        pltpu.VMEM((seq, 2 * LANES), BF16),
                        pltpu.VMEM((2, nrows, LANES), F32),
                        pltpu.VMEM((2, nrows, 2 * LANES), F32),
                        pltpu.VMEM((nrows, tk), F32), pltpu.VMEM((nrows, tk), F32),
                        pltpu.VMEM((nrows, tk), BF16), pltpu.VMEM((nrows, tk), BF16),
                        pltpu.VMEM((nrows, LANES), F32), pltpu.VMEM((nrows, LANES), F32)])
    return pl.pallas_call(
        kernel,
        grid_spec=grid_spec,
        out_shape=jax.ShapeDtypeStruct((nbatch, seq, ngroups * qw), BF16),
        compiler_params=_cparams(("arbitrary", "arbitrary")),
        name="flash_alibi" if alibi else "flash_gqa",
    )(slopes, q, k, v, lam_pack, subg)


def _natten_kernel(q_ref, k_ref, v_ref, bias_ref, o_ref, v_sc, s0_sc, s1_sc, *, seq):
    lane = lax.broadcasted_iota(jnp.int32, (1, LANES), 1)
    low_half = lane < HEAD_DIM
    ntiles = seq // NAT_TQ
    rows = seq // GRID_W
    s_bufs = (s0_sc, s1_sc)

    v_sc[:, 0:LANES] = v_ref[...]
    v_sc[:, LANES:2 * LANES] = jnp.ones((seq, LANES), BF16)

    def window(t):
        w0 = jnp.clip(t * NAT_QROWS - WIN_H // 2, 0, rows - NAT_KROWS)
        return pl.multiple_of(w0 * GRID_W, NAT_TQ)

    def scores(t, par):
        q = q_ref[pl.ds(pl.multiple_of(t * NAT_TQ, NAT_TQ), NAT_TQ), :]
        zero = jnp.zeros_like(q)
        q2 = jnp.concatenate([jnp.where(low_half, q, zero), jnp.where(low_half, zero, q)], axis=0)
        s = lax.dot_general(q2, k_ref[pl.ds(window(t), NAT_TK), :], (((1,), (1,)), ((), ())),
                            preferred_element_type=F32)
        cls = jnp.where(t == 0, 0, jnp.where(t == ntiles - 1, 2, 1))
        bias = bias_ref[cls].astype(F32).reshape(2 * NAT_TQ, NAT_TK)
        s_bufs[par][...] = s + bias

    def finish(t, par):
        s = s_bufs[par][...]
        m = jnp.max(s, axis=-1, keepdims=True)
        p = jnp.exp2(s - m).astype(BF16)
        acc = jnp.dot(p, v_sc[pl.ds(window(t), NAT_TK), :], preferred_element_type=F32)
        o = acc[:, 0:LANES] * (1.0 / acc[:, LANES:2 * LANES])
        out = jnp.where(low_half, o[0:NAT_TQ, :], o[NAT_TQ:, :])
        o_ref[pl.ds(pl.multiple_of(t * NAT_TQ, NAT_TQ), NAT_TQ), :] = out.astype(o_ref.dtype)

    scores(jnp.int32(0), 0)

    def pair_body(u, carry):
        t = 2 * u
        scores(t + 1, 1)
        finish(t, 0)
        scores((t + 2) % ntiles, 0)
        finish(t + 1, 1)
        return carry

    lax.fori_loop(0, ntiles // 2, pair_body, 0)


def _natten(q, k, v, bias_tab, *, nbatch, seq):
    npairs = C_HEADS // 2
    blk = pl.BlockSpec((None, seq, LANES), lambda p, b: (b, 0, p))
    return pl.pallas_call(
        functools.partial(_natten_kernel, seq=seq),
        grid=(npairs, nbatch),
        in_specs=[blk, blk, blk,
                  pl.BlockSpec((3, None, 2, NAT_TQ, NAT_TK), lambda p, b: (0, p, 0, 0, 0))],
        out_specs=blk,
        out_shape=jax.ShapeDtypeStruct((nbatch, seq, C_HEADS * HEAD_DIM), BF16),
        scratch_shapes=[pltpu.VMEM((seq, 2 * LANES), BF16),
                        pltpu.VMEM((2 * NAT_TQ, NAT_TK), F32), pltpu.VMEM((2 * NAT_TQ, NAT_TK), F32)],
        compiler_params=_cparams(("arbitrary", "arbitrary")),
        name="natten",
    )(q, k, v, bias_tab)


def _natten_bias_table(rpb, seq):
    rows = seq // GRID_W
    ntiles = rows // NAT_QROWS
    col = jnp.arange(GRID_W)
    cstart = jnp.clip(col - WIN_W // 2, 0, GRID_W - WIN_W)
    col_valid = (col[None, :] >= cstart[:, None]) & (col[None, :] < cstart[:, None] + WIN_W)
    dc_idx = jnp.clip(col[None, :] - col[:, None] + WIN_W - 1, 0, 2 * WIN_W - 2)
    rpb_cols = rpb[:, :, dc_idx]
    tabs = []
    for tile in (0, 1, ntiles - 1):
        r = tile * NAT_QROWS + jnp.arange(NAT_QROWS)
        w0 = int(np.clip(tile * NAT_QROWS - WIN_H // 2, 0, rows - NAT_KROWS))
        kr = w0 + jnp.arange(NAT_KROWS)
        rstart = jnp.clip(r - WIN_H // 2, 0, rows - WIN_H)
        row_valid = (kr[None, :] >= rstart[:, None]) & (kr[None, :] < rstart[:, None] + WIN_H)
        dr_idx = jnp.clip(kr[None, :] - r[:, None] + WIN_H - 1, 0, 2 * WIN_H - 2)
        pick = (dr_idx[:, :, None] == jnp.arange(2 * WIN_H - 1)[None, None, :]).astype(F32)
        b = jnp.einsum("qkd,hdcx->hqckx", pick, rpb_cols * LOG2E,
                       precision=lax.Precision.HIGHEST)
        valid = row_valid[:, None, :, None] & col_valid[None, :, None, :]
        b = jnp.where(valid[None], b, -jnp.inf)
        tabs.append(b.reshape(C_HEADS, NAT_TQ, NAT_TK))
    return jnp.stack(tabs).reshape(3, C_HEADS // 2, 2, NAT_TQ, NAT_TK).astype(BF16)


def _ffn_kernel(x_ref, g1_ref, sh_ref, sc_ref, g2_ref, ng_ref, ma_ref, mb_ref, woa_ref, wob_ref,
                w1_ref, w3_ref, w2_ref, o_ref):
    y = (jnp.dot(ma_ref[...], woa_ref[...], preferred_element_type=F32)
         + jnp.dot(mb_ref[...], wob_ref[...], preferred_element_type=F32))
    x1 = x_ref[...] + g1_ref[...] * y
    h = _norm_mod(x1, ng_ref[...], sh_ref[...], sc_ref[...]).astype(BF16)
    y = None
    for f in range(D_FF // FFN_CHUNK):
        cols = slice(f * FFN_CHUNK, (f + 1) * FFN_CHUNK)
        a = jnp.dot(h, w1_ref[:, cols], preferred_element_type=F32)
        b = jnp.dot(h, w3_ref[:, cols], preferred_element_type=F32)
        d = jnp.dot((a * _sigmoid(a) * b).astype(BF16), w2_ref[cols, :], preferred_element_type=F32)
        y = d if y is None else y + d
    o_ref[...] = x1 + g2_ref[...] * y


def _ffn(x2, mod, norm_g, mixes, wos, w1, w3, w2, seq, tm=512):
    t = x2.shape[0]
    row = lambda i: (i, 0)
    resident = lambda a: pl.BlockSpec(a.shape, lambda i: (0, 0), pipeline_mode=pl.Buffered(1))
    return pl.pallas_call(
        _ffn_kernel,
        grid=(t // tm,),
        in_specs=[pl.BlockSpec((tm, D_MODEL), row),
                  _mod_spec(2, tm, seq), _mod_spec(3, tm, seq), _mod_spec(4, tm, seq), _mod_spec(5, tm, seq),
                  pl.BlockSpec((1, D_MODEL), lambda i: (0, 0)),
                  pl.BlockSpec((tm, mixes[0].shape[1]), row), pl.BlockSpec((tm, mixes[1].shape[1]), row),
                  resident(wos[0]), resident(wos[1]), resident(w1), resident(w3), resident(w2)],
        out_specs=pl.BlockSpec((tm, D_MODEL), row),
        out_shape=jax.ShapeDtypeStruct((t, D_MODEL), F32),
        compiler_params=_cparams(("arbitrary",)),
        name="outproj_ffn_swiglu",
    )(x2, mod, mod, mod, mod, norm_g.reshape(1, D_MODEL), mixes[0], mixes[1], wos[0], wos[1], w1, w3, w2)


def _router_kernel(x_ref, g1_ref, sh_ref, sc_ref, g_ref, mix_ref, wo_ref, rw_ref, rb_ref,
                   x1_ref, h_ref, meta_ref):
    lane = lax.broadcasted_iota(jnp.int32, (1, LANES), 1).astype(F32)
    nsub = 4
    rb = x_ref.shape[0] // nsub
    for r in range(nsub):
        rows = slice(r * rb, (r + 1) * rb)
        x1 = x_ref[rows, :] + g1_ref[...] * jnp.dot(mix_ref[rows, :], wo_ref[...], preferred_element_type=F32)
        x1_ref[rows, :] = x1
        h = _norm_mod(x1, g_ref[...], sh_ref[...], sc_ref[...])
        packed = _pack_halves(h[:, 0:PACK_W], h[:, PACK_W:D_MODEL])
        for s in range(SC_PIECES):
            h_ref[s, rows, :] = packed[:, s * SC_ROW_WORDS:(s + 1) * SC_ROW_WORDS]
        logits = jnp.dot(h, rw_ref[...], preferred_element_type=F32,
                         precision=lax.Precision.HIGHEST) + rb_ref[...]
        logits = jnp.where(lane < N_EXPERTS, logits, -jnp.inf)
        m1 = jnp.max(logits, axis=-1, keepdims=True)
        i1 = jnp.min(jnp.where(logits == m1, lane, float(LANES)), axis=-1, keepdims=True)
        rest = jnp.where(lane == i1, -jnp.inf, logits)
        m2 = jnp.max(rest, axis=-1, keepdims=True)
        i2 = jnp.min(jnp.where(rest == m2, lane, float(LANES)), axis=-1, keepdims=True)
        e = jnp.exp(m2 - m1)
        g1 = 1.0 / (1.0 + e)
        g2 = e * g1
        meta_ref[rows, :] = jnp.where(lane == 0, i1, jnp.where(lane == 1, i2, jnp.where(
            lane == 2, g1, jnp.where(lane == 3, g2, 0.0))))


def _router(x2, mod, norm_g, mix, wo, rw_pad, rb_pad, seq, tm=512):
    t = x2.shape[0]
    row = lambda i: (i, 0)
    const = lambda i: (0, 0)
    return pl.pallas_call(
        _router_kernel,
        grid=(t // tm,),
        in_specs=[pl.BlockSpec((tm, D_MODEL), row),
                  _mod_spec(2, tm, seq), _mod_spec(3, tm, seq), _mod_spec(4, tm, seq),
                  pl.BlockSpec((1, D_MODEL), const),
                  pl.BlockSpec((tm, D_MODEL), row), pl.BlockSpec((D_MODEL, D_MODEL), const),
                  pl.BlockSpec((D_MODEL, LANES), const), pl.BlockSpec((1, LANES), const)],
        out_specs=[pl.BlockSpec((tm, D_MODEL), row),
                   pl.BlockSpec((SC_PIECES, tm, SC_ROW_WORDS), lambda i: (0, i, 0)),
                   pl.BlockSpec((tm, LANES), row)],
        out_shape=[jax.ShapeDtypeStruct((t, D_MODEL), F32),
                   jax.ShapeDtypeStruct((SC_PIECES, t, SC_ROW_WORDS), jnp.int32),
                   jax.ShapeDtypeStruct((t, LANES), F32)],
        compiler_params=_cparams(("arbitrary",)),
        name="outproj_moe_router",
    )(x2, mod, mod, mod, norm_g.reshape(1, D_MODEL), mix, wo, rw_pad, rb_pad)


def _moe_kernel(te_ref, nt_ref, x_ref, w1_ref, w3_ref, w2a_ref, w2b_ref, o_ref, x_sc, g_sc, *, nf, tf):
    i = pl.program_id(0)
    j = pl.program_id(1)
    active = i < nt_ref[0]

    @pl.when(jnp.logical_and(active, j == 0))
    def _():
        x_sc[...] = _unpack_halves(_load_pieces(x_ref)).astype(BF16)

    @pl.when(jnp.logical_and(active, j < nf))
    def _():
        x = x_sc[...]
        a = jnp.dot(x, w1_ref[...], preferred_element_type=F32)
        b = jnp.dot(x, w3_ref[...], preferred_element_type=F32)
        g_sc[j] = (a * _sigmoid(a) * b).astype(BF16)

    @pl.when(jnp.logical_and(active, j >= nf))
    def _():
        ya, yb = None, None
        for f in range(nf):
            g = g_sc[f]
            da = jnp.dot(g, w2a_ref[f * tf:(f + 1) * tf, :], preferred_element_type=F32)
            db = jnp.dot(g, w2b_ref[f * tf:(f + 1) * tf, :], preferred_element_type=F32)
            ya = da if ya is None else ya + da
            yb = db if yb is None else yb + db
        o_ref[...] = _pack_halves(ya, yb)

    @pl.when(jnp.logical_and(jnp.logical_not(active), j >= nf))
    def _():
        o_ref[...] = jnp.zeros(o_ref.shape, o_ref.dtype)


def _moe_grouped(xs, tile_expert, num_tiles, w1, w3, w2, tg, tf=512):
    p = xs.shape[1]
    nf = w1.shape[2] // tf
    tn = SC_ROW_WORDS
    nb = SC_PIECES
    fcl = lambda j: jnp.minimum(j, nf - 1)
    ncl = lambda j: jnp.maximum(j - nf, 0)

    def w2_map(col0, switch):
        def index(i, j, te, nt):
            early = j < switch
            expert = jnp.where(early, te[jnp.maximum(i - 1, 0)], te[i])
            return (expert, 0, jnp.where(early, col0 + nb - 1, col0 + ncl(j)))
        return index

    grid_spec = pltpu.PrefetchScalarGridSpec(
        num_scalar_prefetch=2,
        grid=(p // tg, nf + nb),
        in_specs=[pl.BlockSpec((SC_PIECES, tg, SC_ROW_WORDS), lambda i, j, te, nt: (0, i, 0)),
                  pl.BlockSpec((None, D_MODEL, tf), lambda i, j, te, nt: (te[i], 0, fcl(j))),
                  pl.BlockSpec((None, D_MODEL, tf), lambda i, j, te, nt: (te[i], 0, fcl(j))),
                  pl.BlockSpec((None, nf * tf, tn), w2_map(0, nf // 2)),
                  pl.BlockSpec((None, nf * tf, tn), w2_map(nb, nf // 2 + 2))],
        out_specs=pl.BlockSpec((None, tg, tn), lambda i, j, te, nt: (ncl(j), i, 0)),
        scratch_shapes=[pltpu.VMEM((tg, D_MODEL), BF16), pltpu.VMEM((nf, tg, tf), BF16)])
    return pl.pallas_call(
        functools.partial(_moe_kernel, nf=nf, tf=tf),
        grid_spec=grid_spec,
        out_shape=jax.ShapeDtypeStruct((SC_PIECES, p, SC_ROW_WORDS), jnp.int32),
        compiler_params=_cparams(("arbitrary", "arbitrary")),
        name="moe_grouped",
    )(tile_expert, num_tiles, xs, w1, w3, w2, w2)


def _combine_kernel(x_ref, gate_ref, meta_ref, y1_ref, y2_ref, o_ref):
    meta = meta_ref[...]
    moe = (meta[:, 2:3] * _unpack_halves(_load_pieces(y1_ref))
           + meta[:, 3:4] * _unpack_halves(_load_pieces(y2_ref)))
    o_ref[...] = x_ref[...] + gate_ref[...] * moe


def _combine(x2, mod, meta, yg, seq, tm=512):
    t = x2.shape[0]
    nt = t // tm
    row = lambda i: (i, 0)
    return pl.pallas_call(
        _combine_kernel,
        grid=(nt,),
        in_specs=[pl.BlockSpec((tm, D_MODEL), row), _mod_spec(5, tm, seq),
                  pl.BlockSpec((tm, LANES), row),
                  pl.BlockSpec((SC_PIECES, tm, SC_ROW_WORDS), lambda i: (0, i, 0)),
                  pl.BlockSpec((SC_PIECES, tm, SC_ROW_WORDS), lambda i: (0, nt + i, 0))],
        out_specs=pl.BlockSpec((tm, D_MODEL), row),
        out_shape=jax.ShapeDtypeStruct((t, D_MODEL), F32),
        compiler_params=_cparams(("arbitrary",)),
        name="moe_combine",
    )(x2, mod, meta, yg, yg)


def _sc_gather(table, idx):
    n = idx.shape[0]
    nrows = table.shape[1]
    pieces = n * SC_PIECES
    idx_pieces = jnp.concatenate([idx + s * nrows for s in range(SC_PIECES)]).reshape(1, pieces)
    mesh = plsc.VectorSubcoreMesh(core_axis_name="core", subcore_axis_name="subcore")

    @pl.kernel(out_type=jax.ShapeDtypeStruct((pieces, SC_ROW_WORDS), table.dtype), mesh=mesh, scratch_types=[])
    def gather_kernel(table_hbm, idx_hbm, out_hbm):
        def body(idx_vmem, out_vmem):
            pltpu.sync_copy(table_hbm.at[idx_vmem.at[0]], out_vmem)

        pltpu.emit_pipeline(
            body,
            grid=(pieces // SC_WINDOW,),
            in_specs=[pl.BlockSpec((1, SC_WINDOW), lambda i: (0, i))],
            out_specs=[pl.BlockSpec((SC_WINDOW, SC_ROW_WORDS), lambda i: (i, 0))],
            core_axis_name=("core", "subcore"),
            dimension_semantics=(pltpu.PARALLEL,),
        )(idx_hbm, out_hbm)

    out = gather_kernel(table.reshape(SC_PIECES * nrows, SC_ROW_WORDS), idx_pieces)
    return out.reshape(SC_PIECES, n, SC_ROW_WORDS)


def _moe(x2, mod, norm_g, mix, wo, rw_pad, rb_pad, w1, w3, w2, seq, tg=1024):
    t = x2.shape[0]
    x2, h, meta = _router(x2, mod, norm_g, mix, wo, rw_pad, rb_pad, seq)
    e_flat = meta[:, 0:2].astype(jnp.int32).reshape(-1)
    onehot = (e_flat[:, None] == jnp.arange(N_EXPERTS)[None, :]).astype(jnp.int32)
    csum = jnp.cumsum(onehot, axis=0)
    counts = csum[-1]
    rank = jnp.take_along_axis(csum, e_flat[:, None], axis=1)[:, 0] - 1
    padded = ((counts + tg - 1) // tg) * tg
    pend = jnp.cumsum(padded)
    pos = (pend - padded)[e_flat] + rank
    p_rows = 2 * t + N_EXPERTS * tg
    row_token = jnp.zeros((p_rows,), jnp.int32).at[pos].set(
        jnp.arange(2 * t, dtype=jnp.int32) // 2, unique_indices=True, mode="promise_in_bounds")
    tile_start = jnp.arange(p_rows // tg, dtype=jnp.int32) * tg
    tile_expert = jnp.minimum(jnp.sum((tile_start[:, None] >= pend[None, :]).astype(jnp.int32), axis=1),
                              N_EXPERTS - 1)
    num_tiles = (pend[-1] // tg).astype(jnp.int32).reshape(1)
    xs = _sc_gather(h, row_token)
    ys = _moe_grouped(xs, tile_expert, num_tiles, w1, w3, w2, tg)
    yg = _sc_gather(ys, jnp.concatenate([pos[0::2], pos[1::2]]))
    return _combine(x2, mod, meta, yg, seq)


def _rope_tables(seq):
    t = np.arange(seq)
    lane = np.arange(LANES)
    d = lane % HEAD_DIM
    pos = np.where((d // 32)[None, :] == 0, (t // GRID_W)[:, None], (t % GRID_W)[:, None]).astype(np.float32)
    inv = (ROPE_THETA ** (-np.arange(16, dtype=np.float32) / 16)).astype(np.float32)
    ang = pos * inv[(d % 16)][None, :]
    return jnp.asarray(np.cos(ang), F32), jnp.asarray(np.sin(ang), F32)


def _pair_gain(g):
    return jnp.concatenate([g, g]).astype(F32)


def _prepare(p, seq):
    even, odd = {}, {}
    w_in = p["w_in_even"][0]
    qa, ka, va, qb, kb, vb = jnp.split(w_in, [512, 640, 768, 1280, 1792], axis=1)
    dup = lambda w: jnp.concatenate([w[:, 0:64], w[:, 0:64], w[:, 64:128], w[:, 64:128]], axis=1)
    even["w_in"] = jnp.concatenate([qa, qb, dup(ka), dup(va), kb, vb], axis=1).astype(BF16)
    gains = jnp.zeros((8, LANES), F32)
    gains = gains.at[0].set(_pair_gain(p["qnorm_a"][0])).at[1].set(_pair_gain(p["knorm_a"][0]))
    gains = gains.at[2].set(_pair_gain(p["qnorm_b"][0])).at[3].set(_pair_gain(p["knorm_b"][0]))
    even["gains"] = gains
    qscale = SCALE * LOG2E
    even["groups"] = ((512, 0, True, qscale), (512, 2, False, qscale), (256, 1, True, 1.0),
                      (256, None, False, 1.0), (512, 3, False, 1.0), (512, None, False, 1.0))
    lam = jnp.zeros((8, LANES), F32)
    for r, name in enumerate(("lam_q1", "lam_k1", "lam_q2", "lam_k2")):
        lam = lam.at[r, 0:HEAD_DIM].set(p[name][0])
    even["lam"] = lam
    even["subg"] = p["subln_b"][0].reshape(1, LANES).astype(F32)
    wo = p["w_out_even"][0].astype(BF16)
    even["wo"] = (wo[0:512], wo[512:1024])
    even["slopes"] = jnp.asarray(LOG2E * 2.0 ** (-8.0 * (np.arange(B_HEADS) + 1.0) / B_HEADS), F32)
    for name in ("ffn_w1", "ffn_w3", "ffn_w2"):
        even[name] = p[name][0].astype(BF16)
    for name in ("ada_w", "ada_b", "norm_mix", "norm_ffn"):
        even[name] = p[name + "_even"][0]
        odd[name] = p[name + "_odd"][0]

    odd["w_qkv"] = p["w_qkv_odd"][0].astype(BF16)
    gains = jnp.zeros((8, LANES), F32)
    odd["gains"] = gains.at[0].set(_pair_gain(p["qnorm_c"][0])).at[1].set(_pair_gain(p["knorm_c"][0]))
    odd["groups"] = ((1024, 0, False, qscale), (1024, 1, False, 1.0), (1024, None, False, 1.0))
    odd["bias_tab"] = _natten_bias_table(p["rpb_c"][0], seq)
    odd["wo"] = (p["w_out_odd"][0].astype(BF16),)
    odd["rw"] = jnp.zeros((D_MODEL, LANES), F32).at[:, 0:N_EXPERTS].set(p["router_w"][0])
    odd["rb"] = jnp.zeros((1, LANES), F32).at[0, 0:N_EXPERTS].set(p["router_b"][0])
    for name in ("moe_w1", "moe_w3", "moe_w2"):
        odd[name] = p[name][0].astype(BF16)
    cos_t, sn_t = _rope_tables(seq)
    gmat = jnp.asarray(np.kron(np.eye(2), np.full((HEAD_DIM, HEAD_DIM), 1.0 / HEAD_DIM)), BF16)
    return even, odd, (cos_t, sn_t, gmat)


def _even_layer(x2, c, ev, shared, nbatch, seq):
    cos_t, sn_t, gmat = shared
    mod = _ada_modulation(c, ev["ada_w"], ev["ada_b"])
    qa, qb, ka, va, kb, vb = _projection(x2, mod, ev["norm_mix"], ev["w_in"], cos_t, sn_t, ev["gains"], gmat,
                                         ev["groups"], seq)
    shp = lambda a: a.reshape(nbatch, seq, a.shape[1])
    lam_init = 0.8 - 0.6 * math.exp(-0.3 * 0)
    mix_a = _flash_attention(shp(qa), shp(ka), shp(va), ev["slopes"], ev["lam"], ev["subg"], nbatch=nbatch, seq=seq,
                             ngroups=A_KV_HEADS, nstack=4, alibi=False, lam_init=lam_init)
    mix_b = _flash_attention(shp(qb), shp(kb), shp(vb), ev["slopes"], ev["lam"], ev["subg"], nbatch=nbatch, seq=seq,
                             ngroups=B_HEADS, nstack=2, alibi=True, lam_init=lam_init)
    t = nbatch * seq
    return _ffn(x2, mod, ev["norm_ffn"], (mix_a.reshape(t, -1), mix_b.reshape(t, -1)), ev["wo"],
                ev["ffn_w1"], ev["ffn_w3"], ev["ffn_w2"], seq)


def _odd_layer(x2, c, od, shared, nbatch, seq):
    cos_t, sn_t, gmat = shared
    mod = _ada_modulation(c, od["ada_w"], od["ada_b"])
    q, k, v = _projection(x2, mod, od["norm_mix"], od["w_qkv"], cos_t, sn_t, od["gains"], gmat, od["groups"], seq)
    shp = lambda a: a.reshape(nbatch, seq, a.shape[1])
    mix = _natten(shp(q), shp(k), shp(v), od["bias_tab"], nbatch=nbatch, seq=seq)
    return _moe(x2, mod, od["norm_ffn"], mix.reshape(nbatch * seq, -1), od["wo"][0], od["rw"], od["rb"],
                od["moe_w1"], od["moe_w3"], od["moe_w2"], seq)


def _trunk(x, c, ev, od, shared):
    nbatch, seq, d = x.shape
    x2 = x.reshape(nbatch * seq, d)
    x2 = _even_layer(x2, c, ev, shared, nbatch, seq)
    x2 = _odd_layer(x2, c, od, shared, nbatch, seq)
    return x2.reshape(nbatch, seq, d)


def kernel(x_prompt, x_sample, c_prompt, c_sample, ada_w_even, ada_b_even, norm_mix_even, norm_ffn_even, w_in_even, qnorm_a, knorm_a, qnorm_b, knorm_b, lam_q1, lam_k1, lam_q2, lam_k2, subln_b, w_out_even, ffn_w1, ffn_w3, ffn_w2, ada_w_odd, ada_b_odd, norm_mix_odd, norm_ffn_odd, w_qkv_odd, qnorm_c, knorm_c, rpb_c, w_out_odd, router_w, router_b, moe_w1, moe_w3, moe_w2):
    params = dict(ada_w_even=ada_w_even, ada_b_even=ada_b_even, norm_mix_even=norm_mix_even,
                  norm_ffn_even=norm_ffn_even, w_in_even=w_in_even, qnorm_a=qnorm_a, knorm_a=knorm_a,
                  qnorm_b=qnorm_b, knorm_b=knorm_b, lam_q1=lam_q1, lam_k1=lam_k1, lam_q2=lam_q2, lam_k2=lam_k2,
                  subln_b=subln_b, w_out_even=w_out_even, ffn_w1=ffn_w1, ffn_w3=ffn_w3, ffn_w2=ffn_w2,
                  ada_w_odd=ada_w_odd, ada_b_odd=ada_b_odd, norm_mix_odd=norm_mix_odd, norm_ffn_odd=norm_ffn_odd,
                  w_qkv_odd=w_qkv_odd, qnorm_c=qnorm_c, knorm_c=knorm_c, rpb_c=rpb_c, w_out_odd=w_out_odd,
                  router_w=router_w, router_b=router_b, moe_w1=moe_w1, moe_w3=moe_w3, moe_w2=moe_w2)
    seq = x_prompt.shape[1]
    ev, od, shared = _prepare(params, seq)
    y_prompt = _trunk(x_prompt, c_prompt, ev, od, shared)
    y_sample = _trunk(x_sample, c_sample, ev, od, shared)
    return (y_prompt, y_sample)
```

```python
import functools
import math

import numpy as np
import jax
import jax.numpy as jnp
from jax import lax
from jax.experimental import pallas as pl
from jax.experimental.pallas import tpu as pltpu
from jax.experimental.pallas import tpu_sc as plsc

F32 = jnp.float32
BF16 = jnp.bfloat16

D_MODEL = 1024
HEAD_DIM = 64
LANES = 128
SCALE = HEAD_DIM ** -0.5
LOG2E = 1.4426950408889634
GRID_W = 64
EPS = 1e-6
ROPE_THETA = 10000.0
A_Q_HEADS = 8
A_KV_HEADS = 2
B_HEADS = 4
C_HEADS = 16
WIN_H = 8
WIN_W = 16
N_EXPERTS = 8
D_FF = 2816
D_FF_EXPERT = 3584
VMEM_LIMIT = 56 * 1024 * 1024

NAT_QROWS = 4
NAT_KROWS = 12
NAT_TQ = NAT_QROWS * GRID_W
NAT_TK = NAT_KROWS * GRID_W
PACK_W = D_MODEL // 2
SC_WINDOW = 128
SC_ROW_WORDS = 256
SC_PIECES = PACK_W // SC_ROW_WORDS
FFN_CHUNK = 1408


def _cparams(sem):
    return pltpu.CompilerParams(dimension_semantics=sem, vmem_limit_bytes=VMEM_LIMIT)


def _norm_mod(x, g, shift, scale):
    ms = jnp.mean(x * x, axis=-1, keepdims=True)
    y = x * lax.rsqrt(ms + EPS) * g
    return y * (1.0 + scale) + shift


def _head_norm(x, gain, gmat):
    ms = jnp.dot((x * x).astype(BF16), gmat, preferred_element_type=F32)
    return x * lax.rsqrt(ms + EPS) * gain


def _rope(x, cos, sn, first_quarter):
    up = pltpu.roll(x, x.shape[1] - 16, 1)
    down = pltpu.roll(x, 16, 1)
    return x * cos + sn * jnp.where(first_quarter, -up, down)


def _sigmoid(a):
    return 1.0 / (1.0 + jnp.exp(-a))


def _pack_halves(a, b):
    hi = lax.bitcast_convert_type(a.astype(BF16).astype(F32), jnp.int32)
    lo = lax.bitcast_convert_type(b.astype(BF16).astype(F32), jnp.int32)
    return hi | lax.shift_right_logical(lo, jnp.full_like(lo, 16))


def _unpack_halves(w):
    hi = lax.bitcast_convert_type(w & jnp.int32(-65536), F32)
    lo = lax.bitcast_convert_type(lax.shift_left(w, jnp.full_like(w, 16)), F32)
    return jnp.concatenate([hi, lo], axis=1)


def _load_pieces(ref):
    return jnp.concatenate([ref[s] for s in range(SC_PIECES)], axis=1)


def _ada_kernel(c_ref, w_ref, b_ref, o_ref):
    c = c_ref[...]
    s = c * _sigmoid(c)
    o_ref[...] = jnp.dot(s, w_ref[...], preferred_element_type=F32,
                         precision=lax.Precision.HIGHEST) + b_ref[...]


def _ada_modulation(c, w, b):
    nb, d = c.shape
    n = w.shape[1]
    tn = 512
    mod = pl.pallas_call(
        _ada_kernel,
        grid=(n // tn,),
        in_specs=[pl.BlockSpec((nb, d), lambda j: (0, 0)),
                  pl.BlockSpec((d, tn), lambda j: (0, j)),
                  pl.BlockSpec((1, tn), lambda j: (0, j))],
        out_specs=pl.BlockSpec((nb, tn), lambda j: (0, j)),
        out_shape=jax.ShapeDtypeStruct((nb, n), F32),
        compiler_params=_cparams(("arbitrary",)),
        name="ada_mod",
    )(c, w, b.reshape(1, n))
    return mod.reshape(nb, 6, 1, d)


def _mod_spec(k, tm, seq):
    return pl.BlockSpec((None, None, 1, D_MODEL), lambda i, *_: ((i * tm) // seq, k, 0, 0))


def _proj_kernel(x_ref, sh_ref, sc_ref, g_ref, w_ref, cos_ref, sn_ref, gains_ref, gmat_ref, *o_refs,
                 groups):
    h = _norm_mod(x_ref[...], g_ref[...], sh_ref[...], sc_ref[...])
    y = jnp.dot(h.astype(BF16), w_ref[...], preferred_element_type=F32)
    wide = 2 * LANES
    lane = lax.broadcasted_iota(jnp.int32, (1, wide), 1)
    first_quarter = (lane % 32) < 16
    gmat = gmat_ref[...]
    cos = jnp.concatenate([cos_ref[...]] * 2, axis=1)
    sn = jnp.concatenate([sn_ref[...]] * 2, axis=1)
    off = 0
    for o_ref, (width, gain_row, rope, mult) in zip(o_refs, groups):
        if gain_row is None:
            o_ref[...] = y[:, off:off + width].astype(o_ref.dtype)
        else:
            gain = jnp.concatenate([gains_ref[gain_row:gain_row + 1, :]] * 2, axis=1)
            for t in range(width // wide):
                z = _head_norm(y[:, off + t * wide: off + (t + 1) * wide], gain, gmat)
                if rope:
                    z = _rope(z, cos, sn, first_quarter)
                if mult != 1.0:
                    z = z * mult
                o_ref[:, t * wide:(t + 1) * wide] = z.astype(o_ref.dtype)
        off += width


def _projection(x2, mod, norm_g, w, cos_t, sn_t, gains, gmat, groups, seq, tm=512):
    t = x2.shape[0]
    n = w.shape[1]
    nseq = seq // tm
    row = lambda i: (i, 0)
    const = lambda i: (0, 0)
    tab = lambda i: (i % nseq, 0)
    return pl.pallas_call(
        functools.partial(_proj_kernel, groups=groups),
        grid=(t // tm,),
        in_specs=[pl.BlockSpec((tm, D_MODEL), row),
                  _mod_spec(0, tm, seq), _mod_spec(1, tm, seq),
                  pl.BlockSpec((1, D_MODEL), const),
                  pl.BlockSpec((D_MODEL, n), const),
                  pl.BlockSpec((tm, LANES), tab), pl.BlockSpec((tm, LANES), tab),
                  pl.BlockSpec(gains.shape, const),
                  pl.BlockSpec(gmat.shape, const)],
        out_specs=[pl.BlockSpec((tm, g[0]), row) for g in groups],
        out_shape=[jax.ShapeDtypeStruct((t, g[0]), BF16) for g in groups],
        compiler_params=_cparams(("arbitrary",)),
        name="norm_mod_proj",
    )(x2, mod, mod, norm_g.reshape(1, D_MODEL), w, cos_t, sn_t, gains, gmat)


def _flash_kernel(slope_ref, q_ref, k_ref, v_ref, lam_ref, subg_ref, o_ref,
                  q_sc, v_sc, m_sc, acc_sc, s0_sc, s1_sc, p0_sc, p1_sc, a0_sc, a1_sc,
                  *, tq, tk, seq, nstack, alibi, lam_init):
    g = pl.program_id(1)
    lane = lax.broadcasted_iota(jnp.int32, (1, LANES), 1)
    low_half = lane < HEAD_DIM
    s_bufs, p_bufs, a_bufs = (s0_sc, s1_sc), (p0_sc, p1_sc), (a0_sc, a1_sc)
    nchunks = seq // tk
    ntiles = seq // tq
    nrows = nstack * tq

    v_sc[:, 0:LANES] = v_ref[...]
    v_sc[:, LANES:2 * LANES] = jnp.ones((seq, LANES), BF16)

    if alibi:
        rc = (lax.broadcasted_iota(jnp.int32, (tq, tk), 0)
              - lax.broadcasted_iota(jnp.int32, (tq, tk), 1)).astype(F32)
        neg_slope = -slope_ref[g]
        lp = lam_ref[...]
        l1 = jnp.sum(lp[0:1, :] * lp[1:2, :], axis=-1, keepdims=True)
        l2 = jnp.sum(lp[2:3, :] * lp[3:4, :], axis=-1, keepdims=True)
        lam = jnp.exp(l1) - jnp.exp(l2) + lam_init

    def tile_rows(t):
        return pl.ds(t * tq, tq) if isinstance(t, int) else pl.ds(pl.multiple_of(t * tq, tq), tq)

    def chunk_rows(c):
        return pl.ds(c * tk, tk) if isinstance(c, int) else pl.ds(pl.multiple_of(c * tk, tk), tk)

    def load_queries(t, slot):
        for u in range(nstack):
            src = q_ref[tile_rows(t), (u // 2) * LANES:(u // 2 + 1) * LANES]
            keep = low_half if u % 2 == 0 else jnp.logical_not(low_half)
            q_sc[slot, u * tq:(u + 1) * tq, :] = jnp.where(keep, src, jnp.zeros_like(src))

    def scores(t, c, slot, par):
        s = lax.dot_general(q_sc[slot], k_ref[chunk_rows(c), :], (((1,), (1,)), ((), ())),
                            preferred_element_type=F32)
        if alibi:
            base = (t * tq - c * tk).astype(F32) if not (isinstance(t, int) and isinstance(c, int)) \
                else float(t * tq - c * tk)
            bias = neg_slope * jnp.abs(rc + base)
            s = s + jnp.concatenate([bias] * nstack, axis=0)
        s_bufs[par][...] = s

    def softmax(slot, par, first):
        s = s_bufs[par][...]
        m_cur = jnp.max(s, axis=-1, keepdims=True)
        if first:
            m_new = jnp.broadcast_to(m_cur, (nrows, LANES))
        else:
            m_old = m_sc[slot]
            m_new = jnp.maximum(m_old, m_cur)
            a_bufs[par][...] = jnp.exp2(m_old - m_new)
        p_bufs[par][...] = jnp.exp2(s - pltpu.repeat(m_new, tk // LANES, 1)).astype(BF16)
        m_sc[slot] = m_new

    def values(c, slot, par, first):
        d = jnp.dot(p_bufs[par][...], v_sc[chunk_rows(c), :], preferred_element_type=F32)
        if first:
            acc_sc[slot] = d
        else:
            acc_sc[slot] = pltpu.repeat(a_bufs[par][...], 2, 1) * acc_sc[slot] + d

    def finalize(t, slot):
        acc = acc_sc[slot]
        o = acc[:, 0:LANES] * (1.0 / acc[:, LANES:2 * LANES])
        if alibi:
            ob = o[0:tq, :] - lam * o[tq:2 * tq, :]
            ms = jnp.mean(ob * ob, axis=-1, keepdims=True)
            ob = ob * lax.rsqrt(ms + EPS) * subg_ref[...] * (1.0 - lam_init)
            o_ref[tile_rows(t), :] = ob.astype(o_ref.dtype)
        else:
            for pair in range(nstack // 2):
                lo = o[(2 * pair) * tq:(2 * pair + 1) * tq, :]
                hi = o[(2 * pair + 1) * tq:(2 * pair + 2) * tq, :]
                o_ref[tile_rows(t), pair * LANES:(pair + 1) * LANES] = (
                    jnp.where(low_half, lo, hi).astype(o_ref.dtype))

    def step(t, t_next, slot, c):
        static = isinstance(c, int)
        par = c % 2 if static else None
        ahead2 = c + 2
        if static and ahead2 >= nchunks:
            scores(t_next, ahead2 - nchunks, 1 - slot, par)
        else:
            scores(t, ahead2, slot, par)
        if static and c + 1 >= nchunks:
            softmax(1 - slot, 1 - par, first=True)
        else:
            softmax(slot, 1 - par, first=False)
        values(c, slot, par, first=static and c == 0)

    load_queries(0, 0)
    scores(0, 0, 0, 0)
    scores(0, 1, 0, 1)
    softmax(0, 0, first=True)

    def tile_body(t, carry):
        slot = t % 2
        t_next = (t + 1) % ntiles
        load_queries(t_next, 1 - slot)
        lead = min(2, nchunks - 2)
        for c in range(lead):
            step(t, t_next, slot, c)

        def pair_body(j, inner):
            for par in range(2):
                c = 2 * j + par
                scores(t, c + 2, slot, par)
                softmax(slot, 1 - par, first=False)
                values(c, slot, par, first=False)
            return inner

        lax.fori_loop(lead // 2, (nchunks - 2) // 2, pair_body, 0)
        for c in range(nchunks - 2, nchunks):
            step(t, t_next, slot, c)
        finalize(t, slot)
        return carry

    lax.fori_loop(0, ntiles, tile_body, 0)


def _flash_attention(q, k, v, slopes, lam_pack, subg, *, nbatch, seq, ngroups, nstack, alibi, lam_init,
                     nrows=1024):
    qw = (nstack // 2) * LANES
    tq = nrows // nstack
    tk = min(1024, seq // 2)
    kernel = functools.partial(_flash_kernel, tq=tq, tk=tk, seq=seq, nstack=nstack, alibi=alibi,
                               lam_init=lam_init)
    grid_spec = pltpu.PrefetchScalarGridSpec(
        num_scalar_prefetch=1,
        grid=(nbatch, ngroups),
        in_specs=[pl.BlockSpec((None, seq, qw), lambda b, g, s: (b, 0, g)),
                  pl.BlockSpec((None, seq, LANES), lambda b, g, s: (b, 0, g)),
                  pl.BlockSpec((None, seq, LANES), lambda b, g, s: (b, 0, g)),
                  pl.BlockSpec(lam_pack.shape, lambda b, g, s: (0, 0)),
                  pl.BlockSpec(subg.shape, lambda b, g, s: (0, 0))],
        out_specs=pl.BlockSpec((None, seq, qw), lambda b, g, s: (b, 0, g)),
        scratch_shapes=[pltpu.VMEM((2, nrows, LANES), BF16),
                        pltpu.VMEM((seq, 2 * LANES), BF16),
                        pltpu.VMEM((2, nrows, LANES), F32),
                        pltpu.VMEM((2, nrows, 2 * LANES), F32),
                        pltpu.VMEM((nrows, tk), F32), pltpu.VMEM((nrows, tk), F32),
                        pltpu.VMEM((nrows, tk), BF16), pltpu.VMEM((nrows, tk), BF16),
                        pltpu.VMEM((nrows, LANES), F32), pltpu.VMEM((nrows, LANES), F32)])
    return pl.pallas_call(
        kernel,
        grid_spec=grid_spec,
        out_shape=jax.ShapeDtypeStruct((nbatch, seq, ngroups * qw), BF16),
        compiler_params=_cparams(("arbitrary", "arbitrary")),
        name="flash_alibi" if alibi else "flash_gqa",
    )(slopes, q, k, v, lam_pack, subg)


def _natten_kernel(q_ref, k_ref, v_ref, bias_ref, o_ref, v_sc, s0_sc, s1_sc, *, seq):
    lane = lax.broadcasted_iota(jnp.int32, (1, LANES), 1)
    low_half = lane < HEAD_DIM
    ntiles = seq // NAT_TQ
    rows = seq // GRID_W
    s_bufs = (s0_sc, s1_sc)

    v_sc[:, 0:LANES] = v_ref[...]
    v_sc[:, LANES:2 * LANES] = jnp.ones((seq, LANES), BF16)

    def window(t):
        w0 = jnp.clip(t * NAT_QROWS - WIN_H // 2, 0, rows - NAT_KROWS)
        return pl.multiple_of(w0 * GRID_W, NAT_TQ)

    def scores(t, par):
        q = q_ref[pl.ds(pl.multiple_of(t * NAT_TQ, NAT_TQ), NAT_TQ), :]
        zero = jnp.zeros_like(q)
        q2 = jnp.concatenate([jnp.where(low_half, q, zero), jnp.where(low_half, zero, q)], axis=0)
        s = lax.dot_general(q2, k_ref[pl.ds(window(t), NAT_TK), :], (((1,), (1,)), ((), ())),
                            preferred_element_type=F32)
        cls = jnp.where(t == 0, 0, jnp.where(t == ntiles - 1, 2, 1))
        bias = bias_ref[cls].astype(F32).reshape(2 * NAT_TQ, NAT_TK)
        s_bufs[par][...] = s + bias

    def finish(t, par):
        s = s_bufs[par][...]
        m = jnp.max(s, axis=-1, keepdims=True)
        p = jnp.exp2(s - m).astype(BF16)
        acc = jnp.dot(p, v_sc[pl.ds(window(t), NAT_TK), :], preferred_element_type=F32)
        o = acc[:, 0:LANES] * (1.0 / acc[:, LANES:2 * LANES])
        out = jnp.where(low_half, o[0:NAT_TQ, :], o[NAT_TQ:, :])
        o_ref[pl.ds(pl.multiple_of(t * NAT_TQ, NAT_TQ), NAT_TQ), :] = out.astype(o_ref.dtype)

    scores(jnp.int32(0), 0)

    def pair_body(u, carry):
        t = 2 * u
        scores(t + 1, 1)
        finish(t, 0)
        scores((t + 2) % ntiles, 0)
        finish(t + 1, 1)
        return carry

    lax.fori_loop(0, ntiles // 2, pair_body, 0)


def _natten(q, k, v, bias_tab, *, nbatch, seq):
    npairs = C_HEADS // 2
    blk = pl.BlockSpec((None, seq, LANES), lambda p, b: (b, 0, p))
    return pl.pallas_call(
        functools.partial(_natten_kernel, seq=seq),
        grid=(npairs, nbatch),
        in_specs=[blk, blk, blk,
                  pl.BlockSpec((3, None, 2, NAT_TQ, NAT_TK), lambda p, b: (0, p, 0, 0, 0))],
        out_specs=blk,
        out_shape=jax.ShapeDtypeStruct((nbatch, seq, C_HEADS * HEAD_DIM), BF16),
        scratch_shapes=[pltpu.VMEM((seq, 2 * LANES), BF16),
                        pltpu.VMEM((2 * NAT_TQ, NAT_TK), F32), pltpu.VMEM((2 * NAT_TQ, NAT_TK), F32)],
        compiler_params=_cparams(("arbitrary", "arbitrary")),
        name="natten",
    )(q, k, v, bias_tab)


def _natten_bias_table(rpb, seq):
    rows = seq // GRID_W
    ntiles = rows // NAT_QROWS
    col = jnp.arange(GRID_W)
    cstart = jnp.clip(col - WIN_W // 2, 0, GRID_W - WIN_W)
    col_valid = (col[None, :] >= cstart[:, None]) & (col[None, :] < cstart[:, None] + WIN_W)
    dc_idx = jnp.clip(col[None, :] - col[:, None] + WIN_W - 1, 0, 2 * WIN_W - 2)
    rpb_cols = rpb[:, :, dc_idx]
    tabs = []
    for tile in (0, 1, ntiles - 1):
        r = tile * NAT_QROWS + jnp.arange(NAT_QROWS)
        w0 = int(np.clip(tile * NAT_QROWS - WIN_H // 2, 0, rows - NAT_KROWS))
        kr = w0 + jnp.arange(NAT_KROWS)
        rstart = jnp.clip(r - WIN_H // 2, 0, rows - WIN_H)
        row_valid = (kr[None, :] >= rstart[:, None]) & (kr[None, :] < rstart[:, None] + WIN_H)
        dr_idx = jnp.clip(kr[None, :] - r[:, None] + WIN_H - 1, 0, 2 * WIN_H - 2)
        pick = (dr_idx[:, :, None] == jnp.arange(2 * WIN_H - 1)[None, None, :]).astype(F32)
        b = jnp.einsum("qkd,hdcx->hqckx", pick, rpb_cols * LOG2E,
                       precision=lax.Precision.HIGHEST)
        valid = row_valid[:, None, :, None] & col_valid[None, :, None, :]
        b = jnp.where(valid[None], b, -jnp.inf)
        tabs.append(b.reshape(C_HEADS, NAT_TQ, NAT_TK))
    return jnp.stack(tabs).reshape(3, C_HEADS // 2, 2, NAT_TQ, NAT_TK).astype(BF16)


def _ffn_kernel(x_ref, g1_ref, sh_ref, sc_ref, g2_ref, ng_ref, ma_ref, mb_ref, woa_ref, wob_ref,
                w1_ref, w3_ref, w2_ref, o_ref):
    y = (jnp.dot(ma_ref[...], woa_ref[...], preferred_element_type=F32)
         + jnp.dot(mb_ref[...], wob_ref[...], preferred_element_type=F32))
    x1 = x_ref[...] + g1_ref[...] * y
    h = _norm_mod(x1, ng_ref[...], sh_ref[...], sc_ref[...]).astype(BF16)
    y = None
    for f in range(D_FF // FFN_CHUNK):
        cols = slice(f * FFN_CHUNK, (f + 1) * FFN_CHUNK)
        a = jnp.dot(h, w1_ref[:, cols], preferred_element_type=F32)
        b = jnp.dot(h, w3_ref[:, cols], preferred_element_type=F32)
        d = jnp.dot((a * _sigmoid(a) * b).astype(BF16), w2_ref[cols, :], preferred_element_type=F32)
        y = d if y is None else y + d
    o_ref[...] = x1 + g2_ref[...] * y


def _ffn(x2, mod, norm_g, mixes, wos, w1, w3, w2, seq, tm=512):
    t = x2.shape[0]
    row = lambda i: (i, 0)
    resident = lambda a: pl.BlockSpec(a.shape, lambda i: (0, 0), pipeline_mode=pl.Buffered(1))
    return pl.pallas_call(
        _ffn_kernel,
        grid=(t // tm,),
        in_specs=[pl.BlockSpec((tm, D_MODEL), row),
                  _mod_spec(2, tm, seq), _mod_spec(3, tm, seq), _mod_spec(4, tm, seq), _mod_spec(5, tm, seq),
                  pl.BlockSpec((1, D_MODEL), lambda i: (0, 0)),
                  pl.BlockSpec((tm, mixes[0].shape[1]), row), pl.BlockSpec((tm, mixes[1].shape[1]), row),
                  resident(wos[0]), resident(wos[1]), resident(w1), resident(w3), resident(w2)],
        out_specs=pl.BlockSpec((tm, D_MODEL), row),
        out_shape=jax.ShapeDtypeStruct((t, D_MODEL), F32),
        compiler_params=_cparams(("arbitrary",)),
        name="outproj_ffn_swiglu",
    )(x2, mod, mod, mod, mod, norm_g.reshape(1, D_MODEL), mixes[0], mixes[1], wos[0], wos[1], w1, w3, w2)


def _router_kernel(x_ref, g1_ref, sh_ref, sc_ref, g_ref, mix_ref, wo_ref, rw_ref, rb_ref,
                   x1_ref, h_ref, meta_ref):
    lane = lax.broadcasted_iota(jnp.int32, (1, LANES), 1).astype(F32)
    nsub = 4
    rb = x_ref.shape[0] // nsub
    for r in range(nsub):
        rows = slice(r * rb, (r + 1) * rb)
        x1 = x_ref[rows, :] + g1_ref[...] * jnp.dot(mix_ref[rows, :], wo_ref[...], preferred_element_type=F32)
        x1_ref[rows, :] = x1
        h = _norm_mod(x1, g_ref[...], sh_ref[...], sc_ref[...])
        packed = _pack_halves(h[:, 0:PACK_W], h[:, PACK_W:D_MODEL])
        for s in range(SC_PIECES):
            h_ref[s, rows, :] = packed[:, s * SC_ROW_WORDS:(s + 1) * SC_ROW_WORDS]
        logits = jnp.dot(h, rw_ref[...], preferred_element_type=F32,
                         precision=lax.Precision.HIGHEST) + rb_ref[...]
        logits = jnp.where(lane < N_EXPERTS, logits, -jnp.inf)
        m1 = jnp.max(logits, axis=-1, keepdims=True)
        i1 = jnp.min(jnp.where(logits == m1, lane, float(LANES)), axis=-1, keepdims=True)
        rest = jnp.where(lane == i1, -jnp.inf, logits)
        m2 = jnp.max(rest, axis=-1, keepdims=True)
        i2 = jnp.min(jnp.where(rest == m2, lane, float(LANES)), axis=-1, keepdims=True)
        e = jnp.exp(m2 - m1)
        g1 = 1.0 / (1.0 + e)
        g2 = e * g1
        meta_ref[rows, :] = jnp.where(lane == 0, i1, jnp.where(lane == 1, i2, jnp.where(
            lane == 2, g1, jnp.where(lane == 3, g2, 0.0))))


def _router(x2, mod, norm_g, mix, wo, rw_pad, rb_pad, seq, tm=512):
    t = x2.shape[0]
    row = lambda i: (i, 0)
    const = lambda i: (0, 0)
    return pl.pallas_call(
        _router_kernel,
        grid=(t // tm,),
        in_specs=[pl.BlockSpec((tm, D_MODEL), row),
                  _mod_spec(2, tm, seq), _mod_spec(3, tm, seq), _mod_spec(4, tm, seq),
                  pl.BlockSpec((1, D_MODEL), const),
                  pl.BlockSpec((tm, D_MODEL), row), pl.BlockSpec((D_MODEL, D_MODEL), const),
                  pl.BlockSpec((D_MODEL, LANES), const), pl.BlockSpec((1, LANES), const)],
        out_specs=[pl.BlockSpec((tm, D_MODEL), row),
                   pl.BlockSpec((SC_PIECES, tm, SC_ROW_WORDS), lambda i: (0, i, 0)),
                   pl.BlockSpec((tm, LANES), row)],
        out_shape=[jax.ShapeDtypeStruct((t, D_MODEL), F32),
                   jax.ShapeDtypeStruct((SC_PIECES, t, SC_ROW_WORDS), jnp.int32),
                   jax.ShapeDtypeStruct((t, LANES), F32)],
        compiler_params=_cparams(("arbitrary",)),
        name="outproj_moe_router",
    )(x2, mod, mod, mod, norm_g.reshape(1, D_MODEL), mix, wo, rw_pad, rb_pad)


def _moe_kernel(te_ref, nt_ref, x_ref, w1_ref, w3_ref, w2a_ref, w2b_ref, o_ref, x_sc, g_sc, *, nf, tf):
    i = pl.program_id(0)
    j = pl.program_id(1)
    active = i < nt_ref[0]

    @pl.when(jnp.logical_and(active, j == 0))
    def _():
        x_sc[...] = _unpack_halves(_load_pieces(x_ref)).astype(BF16)

    @pl.when(jnp.logical_and(active, j < nf))
    def _():
        x = x_sc[...]
        a = jnp.dot(x, w1_ref[...], preferred_element_type=F32)
        b = jnp.dot(x, w3_ref[...], preferred_element_type=F32)
        g_sc[j] = (a * _sigmoid(a) * b).astype(BF16)

    @pl.when(jnp.logical_and(active, j >= nf))
    def _():
        ya, yb = None, None
        for f in range(nf):
            g = g_sc[f]
            da = jnp.dot(g, w2a_ref[f * tf:(f + 1) * tf, :], preferred_element_type=F32)
            db = jnp.dot(g, w2b_ref[f * tf:(f + 1) * tf, :], preferred_element_type=F32)
            ya = da if ya is None else ya + da
            yb = db if yb is None else yb + db
        o_ref[...] = _pack_halves(ya, yb)

    @pl.when(jnp.logical_and(jnp.logical_not(active), j >= nf))
    def _():
        o_ref[...] = jnp.zeros(o_ref.shape, o_ref.dtype)


def _moe_grouped(xs, tile_expert, num_tiles, w1, w3, w2, tg, tf=512):
    p = xs.shape[1]
    nf = w1.shape[2] // tf
    tn = SC_ROW_WORDS
    nb = SC_PIECES
    fcl = lambda j: jnp.minimum(j, nf - 1)
    ncl = lambda j: jnp.maximum(j - nf, 0)

    def w2_map(col0, switch):
        def index(i, j, te, nt):
            early = j < switch
            expert = jnp.where(early, te[jnp.maximum(i - 1, 0)], te[i])
            return (expert, 0, jnp.where(early, col0 + nb - 1, col0 + ncl(j)))
        return index

    grid_spec = pltpu.PrefetchScalarGridSpec(
        num_scalar_prefetch=2,
        grid=(p // tg, nf + nb),
        in_specs=[pl.BlockSpec((SC_PIECES, tg, SC_ROW_WORDS), lambda i, j, te, nt: (0, i, 0)),
                  pl.BlockSpec((None, D_MODEL, tf), lambda i, j, te, nt: (te[i], 0, fcl(j))),
                  pl.BlockSpec((None, D_MODEL, tf), lambda i, j, te, nt: (te[i], 0, fcl(j))),
                  pl.BlockSpec((None, nf * tf, tn), w2_map(0, nf // 2)),
                  pl.BlockSpec((None, nf * tf, tn), w2_map(nb, nf // 2 + 2))],
        out_specs=pl.BlockSpec((None, tg, tn), lambda i, j, te, nt: (ncl(j), i, 0)),
        scratch_shapes=[pltpu.VMEM((tg, D_MODEL), BF16), pltpu.VMEM((nf, tg, tf), BF16)])
    return pl.pallas_call(
        functools.partial(_moe_kernel, nf=nf, tf=tf),
        grid_spec=grid_spec,
        out_shape=jax.ShapeDtypeStruct((SC_PIECES, p, SC_ROW_WORDS), jnp.int32),
        compiler_params=_cparams(("arbitrary", "arbitrary")),
        name="moe_grouped",
    )(tile_expert, num_tiles, xs, w1, w3, w2, w2)


def _combine_kernel(x_ref, gate_ref, meta_ref, y1_ref, y2_ref, o_ref):
    meta = meta_ref[...]
    moe = (meta[:, 2:3] * _unpack_halves(_load_pieces(y1_ref))
           + meta[:, 3:4] * _unpack_halves(_load_pieces(y2_ref)))
    o_ref[...] = x_ref[...] + gate_ref[...] * moe


def _combine(x2, mod, meta, yg, seq, tm=512):
    t = x2.shape[0]
    nt = t // tm
    row = lambda i: (i, 0)
    return pl.pallas_call(
        _combine_kernel,
        grid=(nt,),
        in_specs=[pl.BlockSpec((tm, D_MODEL), row), _mod_spec(5, tm, seq),
                  pl.BlockSpec((tm, LANES), row),
                  pl.BlockSpec((SC_PIECES, tm, SC_ROW_WORDS), lambda i: (0, i, 0)),
                  pl.BlockSpec((SC_PIECES, tm, SC_ROW_WORDS), lambda i: (0, nt + i, 0))],
        out_specs=pl.BlockSpec((tm, D_MODEL), row),
        out_shape=jax.ShapeDtypeStruct((t, D_MODEL), F32),
        compiler_params=_cparams(("arbitrary",)),
        name="moe_combine",
    )(x2, mod, meta, yg, yg)


def _sc_gather(table, idx):
    n = idx.shape[0]
    nrows = table.shape[1]
    pieces = n * SC_PIECES
    idx_pieces = jnp.concatenate([idx + s * nrows for s in range(SC_PIECES)]).reshape(1, pieces)
    mesh = plsc.VectorSubcoreMesh(core_axis_name="core", subcore_axis_name="subcore")

    @pl.kernel(out_type=jax.ShapeDtypeStruct((pieces, SC_ROW_WORDS), table.dtype), mesh=mesh, scratch_types=[])
    def gather_kernel(table_hbm, idx_hbm, out_hbm):
        def body(idx_vmem, out_vmem):
            pltpu.sync_copy(table_hbm.at[idx_vmem.at[0]], out_vmem)

        pltpu.emit_pipeline(
            body,
            grid=(pieces // SC_WINDOW,),
            in_specs=[pl.BlockSpec((1, SC_WINDOW), lambda i: (0, i))],
            out_specs=[pl.BlockSpec((SC_WINDOW, SC_ROW_WORDS), lambda i: (i, 0))],
            core_axis_name=("core", "subcore"),
            dimension_semantics=(pltpu.PARALLEL,),
        )(idx_hbm, out_hbm)

    out = gather_kernel(table.reshape(SC_PIECES * nrows, SC_ROW_WORDS), idx_pieces)
    return out.reshape(SC_PIECES, n, SC_ROW_WORDS)


def _sc_invert(pos, nrows):
    n = pos.shape[0]
    src = jnp.broadcast_to(jnp.arange(n, dtype=jnp.int32)[:, None], (n, LANES))
    mesh = plsc.VectorSubcoreMesh(core_axis_name="core", subcore_axis_name="subcore")

    @pl.kernel(out_type=jax.ShapeDtypeStruct((nrows, LANES), jnp.int32), mesh=mesh, scratch_types=[])
    def scatter_kernel(src_hbm, idx_hbm, out_hbm):
        def body(src_vmem, idx_vmem):
            pltpu.sync_copy(src_vmem, out_hbm.at[idx_vmem.at[0]])

        pltpu.emit_pipeline(
            body,
            grid=(n // SC_WINDOW,),
            in_specs=[pl.BlockSpec((SC_WINDOW, LANES), lambda i: (i, 0)),
                      pl.BlockSpec((1, SC_WINDOW), lambda i: (0, i))],
            out_specs=[],
            core_axis_name=("core", "subcore"),
            dimension_semantics=(pltpu.PARALLEL,),
        )(src_hbm, idx_hbm)

    return scatter_kernel(src, pos.reshape(1, n))[:, 0]


def _moe(x2, mod, norm_g, mix, wo, rw_pad, rb_pad, w1, w3, w2, seq, tg=1024):
    t = x2.shape[0]
    x2, h, meta = _router(x2, mod, norm_g, mix, wo, rw_pad, rb_pad, seq)
    e_flat = meta[:, 0:2].astype(jnp.int32).reshape(-1)
    onehot = (e_flat[:, None] == jnp.arange(N_EXPERTS)[None, :]).astype(jnp.int32)
    csum = jnp.cumsum(onehot, axis=0)
    counts = csum[-1]
    rank = jnp.take_along_axis(csum, e_flat[:, None], axis=1)[:, 0] - 1
    padded = ((counts + tg - 1) // tg) * tg
    pend = jnp.cumsum(padded)
    pos = (pend - padded)[e_flat] + rank
    p_rows = 2 * t + N_EXPERTS * tg
    row_token = jnp.clip(_sc_invert(pos, p_rows) // 2, 0, t - 1)
    tile_start = jnp.arange(p_rows // tg, dtype=jnp.int32) * tg
    tile_expert = jnp.minimum(jnp.sum((tile_start[:, None] >= pend[None, :]).astype(jnp.int32), axis=1),
                              N_EXPERTS - 1)
    num_tiles = (pend[-1] // tg).astype(jnp.int32).reshape(1)
    xs = _sc_gather(h, row_token)
    ys = _moe_grouped(xs, tile_expert, num_tiles, w1, w3, w2, tg)
    yg = _sc_gather(ys, jnp.concatenate([pos[0::2], pos[1::2]]))
    return _combine(x2, mod, meta, yg, seq)


def _rope_tables(seq):
    t = np.arange(seq)
    lane = np.arange(LANES)
    d = lane % HEAD_DIM
    pos = np.where((d // 32)[None, :] == 0, (t // GRID_W)[:, None], (t % GRID_W)[:, None]).astype(np.float32)
    inv = (ROPE_THETA ** (-np.arange(16, dtype=np.float32) / 16)).astype(np.float32)
    ang = pos * inv[(d % 16)][None, :]
    return jnp.asarray(np.cos(ang), F32), jnp.asarray(np.sin(ang), F32)


def _pair_gain(g):
    return jnp.concatenate([g, g]).astype(F32)


def _prepare(p, seq):
    even, odd = {}, {}
    w_in = p["w_in_even"][0]
    qa, ka, va, qb, kb, vb = jnp.split(w_in, [512, 640, 768, 1280, 1792], axis=1)
    dup = lambda w: jnp.concatenate([w[:, 0:64], w[:, 0:64], w[:, 64:128], w[:, 64:128]], axis=1)
    even["w_in"] = jnp.concatenate([qa, qb, dup(ka), dup(va), kb, vb], axis=1).astype(BF16)
    gains = jnp.zeros((8, LANES), F32)
    gains = gains.at[0].set(_pair_gain(p["qnorm_a"][0])).at[1].set(_pair_gain(p["knorm_a"][0]))
    gains = gains.at[2].set(_pair_gain(p["qnorm_b"][0])).at[3].set(_pair_gain(p["knorm_b"][0]))
    even["gains"] = gains
    qscale = SCALE * LOG2E
    even["groups"] = ((512, 0, True, qscale), (512, 2, False, qscale), (256, 1, True, 1.0),
                      (256, None, False, 1.0), (512, 3, False, 1.0), (512, None, False, 1.0))
    lam = jnp.zeros((8, LANES), F32)
    for r, name in enumerate(("lam_q1", "lam_k1", "lam_q2", "lam_k2")):
        lam = lam.at[r, 0:HEAD_DIM].set(p[name][0])
    even["lam"] = lam
    even["subg"] = p["subln_b"][0].reshape(1, LANES).astype(F32)
    wo = p["w_out_even"][0].astype(BF16)
    even["wo"] = (wo[0:512], wo[512:1024])
    even["slopes"] = jnp.asarray(LOG2E * 2.0 ** (-8.0 * (np.arange(B_HEADS) + 1.0) / B_HEADS), F32)
    for name in ("ffn_w1", "ffn_w3", "ffn_w2"):
        even[name] = p[name][0].astype(BF16)
    for name in ("ada_w", "ada_b", "norm_mix", "norm_ffn"):
        even[name] = p[name + "_even"][0]
        odd[name] = p[name + "_odd"][0]

    odd["w_qkv"] = p["w_qkv_odd"][0].astype(BF16)
    gains = jnp.zeros((8, LANES), F32)
    odd["gains"] = gains.at[0].set(_pair_gain(p["qnorm_c"][0])).at[1].set(_pair_gain(p["knorm_c"][0]))
    odd["groups"] = ((1024, 0, False, qscale), (1024, 1, False, 1.0), (1024, None, False, 1.0))
    odd["bias_tab"] = _natten_bias_table(p["rpb_c"][0], seq)
    odd["wo"] = (p["w_out_odd"][0].astype(BF16),)
    odd["rw"] = jnp.zeros((D_MODEL, LANES), F32).at[:, 0:N_EXPERTS].set(p["router_w"][0])
    odd["rb"] = jnp.zeros((1, LANES), F32).at[0, 0:N_EXPERTS].set(p["router_b"][0])
    for name in ("moe_w1", "moe_w3", "moe_w2"):
        odd[name] = p[name][0].astype(BF16)
    cos_t, sn_t = _rope_tables(seq)
    gmat = jnp.asarray(np.kron(np.eye(4), np.full((HEAD_DIM, HEAD_DIM), 1.0 / HEAD_DIM)), BF16)
    return even, odd, (cos_t, sn_t, gmat)


def _even_layer(x2, c, ev, shared, nbatch, seq):
    cos_t, sn_t, gmat = shared
    mod = _ada_modulation(c, ev["ada_w"], ev["ada_b"])
    qa, qb, ka, va, kb, vb = _projection(x2, mod, ev["norm_mix"], ev["w_in"], cos_t, sn_t, ev["gains"], gmat,
                                         ev["groups"], seq)
    shp = lambda a: a.reshape(nbatch, seq, a.shape[1])
    lam_init = 0.8 - 0.6 * math.exp(-0.3 * 0)
    mix_a = _flash_attention(shp(qa), shp(ka), shp(va), ev["slopes"], ev["lam"], ev["subg"], nbatch=nbatch, seq=seq,
                             ngroups=A_KV_HEADS, nstack=4, alibi=False, lam_init=lam_init)
    mix_b = _flash_attention(shp(qb), shp(kb), shp(vb), ev["slopes"], ev["lam"], ev["subg"], nbatch=nbatch, seq=seq,
                             ngroups=B_HEADS, nstack=2, alibi=True, lam_init=lam_init)
    t = nbatch * seq
    return _ffn(x2, mod, ev["norm_ffn"], (mix_a.reshape(t, -1), mix_b.reshape(t, -1)), ev["wo"],
                ev["ffn_w1"], ev["ffn_w3"], ev["ffn_w2"], seq)


def _odd_layer(x2, c, od, shared, nbatch, seq):
    cos_t, sn_t, gmat = shared
    mod = _ada_modulation(c, od["ada_w"], od["ada_b"])
    q, k, v = _projection(x2, mod, od["norm_mix"], od["w_qkv"], cos_t, sn_t, od["gains"], gmat, od["groups"], seq)
    shp = lambda a: a.reshape(nbatch, seq, a.shape[1])
    mix = _natten(shp(q), shp(k), shp(v), od["bias_tab"], nbatch=nbatch, seq=seq)
    return _moe(x2, mod, od["norm_ffn"], mix.reshape(nbatch * seq, -1), od["wo"][0], od["rw"], od["rb"],
                od["moe_w1"], od["moe_w3"], od["moe_w2"], seq)


def _trunk(x, c, ev, od, shared):
    nbatch, seq, d = x.shape
    x2 = x.reshape(nbatch * seq, d)
    x2 = _even_layer(x2, c, ev, shared, nbatch, seq)
    x2 = _odd_layer(x2, c, od, shared, nbatch, seq)
    return x2.reshape(nbatch, seq, d)


def kernel(x_prompt, x_sample, c_prompt, c_sample, ada_w_even, ada_b_even, norm_mix_even, norm_ffn_even, w_in_even, qnorm_a, knorm_a, qnorm_b, knorm_b, lam_q1, lam_k1, lam_q2, lam_k2, subln_b, w_out_even, ffn_w1, ffn_w3, ffn_w2, ada_w_odd, ada_b_odd, norm_mix_odd, norm_ffn_odd, w_qkv_odd, qnorm_c, knorm_c, rpb_c, w_out_odd, router_w, router_b, moe_w1, moe_w3, moe_w2):
    params = dict(ada_w_even=ada_w_even, ada_b_even=ada_b_even, norm_mix_even=norm_mix_even,
                  norm_ffn_even=norm_ffn_even, w_in_even=w_in_even, qnorm_a=qnorm_a, knorm_a=knorm_a,
                  qnorm_b=qnorm_b, knorm_b=knorm_b, lam_q1=lam_q1, lam_k1=lam_k1, lam_q2=lam_q2, lam_k2=lam_k2,
                  subln_b=subln_b, w_out_even=w_out_even, ffn_w1=ffn_w1, ffn_w3=ffn_w3, ffn_w2=ffn_w2,
                  ada_w_odd=ada_w_odd, ada_b_odd=ada_b_odd, norm_mix_odd=norm_mix_odd, norm_ffn_odd=norm_ffn_odd,
                  w_qkv_odd=w_qkv_odd, qnorm_c=qnorm_c, knorm_c=knorm_c, rpb_c=rpb_c, w_out_odd=w_out_odd,
                  router_w=router_w, router_b=router_b, moe_w1=moe_w1, moe_w3=moe_w3, moe_w2=moe_w2)
    seq = x_prompt.shape[1]
    ev, od, shared = _prepare(params, seq)
    y_prompt = _trunk(x_prompt, c_prompt, ev, od, shared)
    y_sample = _trunk(x_sample, c_sample, ev, od, shared)
    return (y_prompt, y_sample)
```

```python
import functools
import math

import numpy as np
import jax
import jax.numpy as jnp
from jax import lax
from jax.experimental import pallas as pl
from jax.experimental.pallas import tpu as pltpu
from jax.experimental.pallas import tpu_sc as plsc

F32 = jnp.float32
BF16 = jnp.bfloat16

D_MODEL = 1024
HEAD_DIM = 64
LANES = 128
SCALE = HEAD_DIM ** -0.5
LOG2E = 1.4426950408889634
GRID_W = 64
EPS = 1e-6
ROPE_THETA = 10000.0
A_Q_HEADS = 8
A_KV_HEADS = 2
B_HEADS = 4
C_HEADS = 16
WIN_H = 8
WIN_W = 16
N_EXPERTS = 8
D_FF = 2816
D_FF_EXPERT = 3584
VMEM_LIMIT = 56 * 1024 * 1024

NAT_QROWS = 4
NAT_KROWS = 12
NAT_TQ = NAT_QROWS * GRID_W
NAT_TK = NAT_KROWS * GRID_W
PACK_W = D_MODEL // 2
SC_WINDOW = 128
SC_ROW_WORDS = 256
SC_PIECES = PACK_W // SC_ROW_WORDS
FFN_CHUNK = 1408


def _cparams(sem):
    return pltpu.CompilerParams(dimension_semantics=sem, vmem_limit_bytes=VMEM_LIMIT)


def _norm_mod(x, g, shift, scale):
    ms = jnp.mean(x * x, axis=-1, keepdims=True)
    y = x * lax.rsqrt(ms + EPS) * g
    return y * (1.0 + scale) + shift


def _head_norm(x, gain, gmat):
    ms = jnp.dot((x * x).astype(BF16), gmat, preferred_element_type=F32)
    return x * lax.rsqrt(ms + EPS) * gain


def _rope(x, cos, sn, first_quarter):
    up = pltpu.roll(x, x.shape[1] - 16, 1)
    down = pltpu.roll(x, 16, 1)
    return x * cos + sn * jnp.where(first_quarter, -up, down)


def _sigmoid(a):
    return 1.0 / (1.0 + jnp.exp(-a))


def _pack_halves(a, b):
    hi = lax.bitcast_convert_type(a.astype(BF16).astype(F32), jnp.int32)
    lo = lax.bitcast_convert_type(b.astype(BF16).astype(F32), jnp.int32)
    return hi | lax.shift_right_logical(lo, jnp.full_like(lo, 16))


def _unpack_halves(w):
    hi = lax.bitcast_convert_type(w & jnp.int32(-65536), F32)
    lo = lax.bitcast_convert_type(lax.shift_left(w, jnp.full_like(w, 16)), F32)
    return jnp.concatenate([hi, lo], axis=1)


def _load_pieces(ref):
    return jnp.concatenate([ref[s] for s in range(SC_PIECES)], axis=1)


def _ada_kernel(c_ref, w_ref, b_ref, o_ref):
    c = c_ref[...]
    s = c * _sigmoid(c)
    o_ref[...] = jnp.dot(s, w_ref[...], preferred_element_type=F32,
                         precision=lax.Precision.HIGHEST) + b_ref[...]


def _ada_modulation(c, w, b):
    nb, d = c.shape
    n = w.shape[1]
    tn = 512
    mod = pl.pallas_call(
        _ada_kernel,
        grid=(n // tn,),
        in_specs=[pl.BlockSpec((nb, d), lambda j: (0, 0)),
                  pl.BlockSpec((d, tn), lambda j: (0, j)),
                  pl.BlockSpec((1, tn), lambda j: (0, j))],
        out_specs=pl.BlockSpec((nb, tn), lambda j: (0, j)),
        out_shape=jax.ShapeDtypeStruct((nb, n), F32),
        compiler_params=_cparams(("arbitrary",)),
        name="ada_mod",
    )(c, w, b.reshape(1, n))
    return mod.reshape(nb, 6, 1, d)


def _mod_spec(k, tm, seq):
    return pl.BlockSpec((None, None, 1, D_MODEL), lambda i, *_: ((i * tm) // seq, k, 0, 0))


def _proj_kernel(x_ref, sh_ref, sc_ref, g_ref, w_ref, cos_ref, sn_ref, gains_ref, gmat_ref, *o_refs,
                 groups):
    wide = 2 * LANES
    lane = lax.broadcasted_iota(jnp.int32, (1, wide), 1)
    first_quarter = (lane % 32) < 16
    gmat = gmat_ref[...]
    h = _norm_mod(x_ref[...], g_ref[...], sh_ref[...], sc_ref[...])
    y = jnp.dot(h.astype(BF16), w_ref[...], preferred_element_type=F32)
    cos = jnp.concatenate([cos_ref[...]] * 2, axis=1)
    sn = jnp.concatenate([sn_ref[...]] * 2, axis=1)
    off = 0
    for o_ref, (width, gain_row, rope, mult) in zip(o_refs, groups):
        if gain_row is None:
            o_ref[...] = y[:, off:off + width].astype(o_ref.dtype)
        else:
            gain = jnp.concatenate([gains_ref[gain_row:gain_row + 1, :]] * 2, axis=1)
            for t in range(width // wide):
                z = _head_norm(y[:, off + t * wide: off + (t + 1) * wide], gain, gmat)
                if rope:
                    z = _rope(z, cos, sn, first_quarter)
                if mult != 1.0:
                    z = z * mult
                o_ref[:, t * wide:(t + 1) * wide] = z.astype(o_ref.dtype)
        off += width


def _projection(x2, mod, norm_g, w, cos_t, sn_t, gains, gmat, groups, seq, tm=512):
    t = x2.shape[0]
    n = w.shape[1]
    nseq = seq // tm
    row = lambda i: (i, 0)
    const = lambda i: (0, 0)
    tab = lambda i: (i % nseq, 0)
    return pl.pallas_call(
        functools.partial(_proj_kernel, groups=groups),
        grid=(t // tm,),
        in_specs=[pl.BlockSpec((tm, D_MODEL), row),
                  _mod_spec(0, tm, seq), _mod_spec(1, tm, seq),
                  pl.BlockSpec((1, D_MODEL), const),
                  pl.BlockSpec((D_MODEL, n), const),
                  pl.BlockSpec((tm, LANES), tab), pl.BlockSpec((tm, LANES), tab),
                  pl.BlockSpec(gains.shape, const),
                  pl.BlockSpec(gmat.shape, const)],
        out_specs=[pl.BlockSpec((tm, g[0]), row) for g in groups],
        out_shape=[jax.ShapeDtypeStruct((t, g[0]), BF16) for g in groups],
        compiler_params=_cparams(("arbitrary",)),
        name="norm_mod_proj",
    )(x2, mod, mod, norm_g.reshape(1, D_MODEL), w, cos_t, sn_t, gains, gmat)


def _flash_kernel(slope_ref, q_ref, k_ref, v_ref, lam_ref, subg_ref, o_ref,
                  q_sc, v_sc, m_sc, acc_sc, s0_sc, s1_sc, p0_sc, p1_sc, a0_sc, a1_sc,
                  *, tq, tk, seq, nstack, alibi, lam_init):
    g = pl.program_id(1)
    lane = lax.broadcasted_iota(jnp.int32, (1, LANES), 1)
    low_half = lane < HEAD_DIM
    s_bufs, p_bufs, a_bufs = (s0_sc, s1_sc), (p0_sc, p1_sc), (a0_sc, a1_sc)
    nchunks = seq // tk
    ntiles = seq // tq
    nrows = nstack * tq

    v_sc[:, 0:LANES] = v_ref[...]
    v_sc[:, LANES:2 * LANES] = jnp.ones((seq, LANES), BF16)

    if alibi:
        rc = (lax.broadcasted_iota(jnp.int32, (tq, tk), 0)
              - lax.broadcasted_iota(jnp.int32, (tq, tk), 1)).astype(F32)
        neg_slope = -slope_ref[g]
        lp = lam_ref[...]
        l1 = jnp.sum(lp[0:1, :] * lp[1:2, :], axis=-1, keepdims=True)
        l2 = jnp.sum(lp[2:3, :] * lp[3:4, :], axis=-1, keepdims=True)
        lam = jnp.exp(l1) - jnp.exp(l2) + lam_init

    def tile_rows(t):
        return pl.ds(t * tq, tq) if isinstance(t, int) else pl.ds(pl.multiple_of(t * tq, tq), tq)

    def chunk_rows(c):
        return pl.ds(c * tk, tk) if isinstance(c, int) else pl.ds(pl.multiple_of(c * tk, tk), tk)

    def load_queries(t, slot):
        for u in range(nstack):
            src = q_ref[tile_rows(t), (u // 2) * LANES:(u // 2 + 1) * LANES]
            keep = low_half if u % 2 == 0 else jnp.logical_not(low_half)
            q_sc[slot, u * tq:(u + 1) * tq, :] = jnp.where(keep, src, jnp.zeros_like(src))

    def scores(t, c, slot, par):
        s = lax.dot_general(q_sc[slot], k_ref[chunk_rows(c), :], (((1,), (1,)), ((), ())),
                            preferred_element_type=F32)
        if alibi:
            base = (t * tq - c * tk).astype(F32) if not (isinstance(t, int) and isinstance(c, int)) \
                else float(t * tq - c * tk)
            bias = neg_slope * jnp.abs(rc + base)
            s = s + jnp.concatenate([bias] * nstack, axis=0)
        s_bufs[par][...] = s

    def softmax(slot, par, first):
        s = s_bufs[par][...]
        m_cur = jnp.max(s, axis=-1, keepdims=True)
        if first:
            m_new = jnp.broadcast_to(m_cur, (nrows, LANES))
        else:
            m_old = m_sc[slot]
            m_new = jnp.maximum(m_old, m_cur)
            a_bufs[par][...] = jnp.exp2(m_old - m_new)
        p_bufs[par][...] = jnp.exp2(s - pltpu.repeat(m_new, tk // LANES, 1)).astype(BF16)
        m_sc[slot] = m_new

    def values(c, slot, par, first):
        d = jnp.dot(p_bufs[par][...], v_sc[chunk_rows(c), :], preferred_element_type=F32)
        if first:
            acc_sc[slot] = d
        else:
            acc_sc[slot] = pltpu.repeat(a_bufs[par][...], 2, 1) * acc_sc[slot] + d

    def finalize(t, slot):
        acc = acc_sc[slot]
        o = acc[:, 0:LANES] * (1.0 / acc[:, LANES:2 * LANES])
        if alibi:
            ob = o[0:tq, :] - lam * o[tq:2 * tq, :]
            ms = jnp.mean(ob * ob, axis=-1, keepdims=True)
            ob = ob * lax.rsqrt(ms + EPS) * subg_ref[...] * (1.0 - lam_init)
            o_ref[tile_rows(t), :] = ob.astype(o_ref.dtype)
        else:
            for pair in range(nstack // 2):
                lo = o[(2 * pair) * tq:(2 * pair + 1) * tq, :]
                hi = o[(2 * pair + 1) * tq:(2 * pair + 2) * tq, :]
                o_ref[tile_rows(t), pair * LANES:(pair + 1) * LANES] = (
                    jnp.where(low_half, lo, hi).astype(o_ref.dtype))

    def step(t, t_next, slot, c):
        static = isinstance(c, int)
        par = c % 2 if static else None
        ahead2 = c + 2
        if static and ahead2 >= nchunks:
            scores(t_next, ahead2 - nchunks, 1 - slot, par)
        else:
            scores(t, ahead2, slot, par)
        if static and c + 1 >= nchunks:
            softmax(1 - slot, 1 - par, first=True)
        else:
            softmax(slot, 1 - par, first=False)
        values(c, slot, par, first=static and c == 0)

    load_queries(0, 0)
    scores(0, 0, 0, 0)
    scores(0, 1, 0, 1)
    softmax(0, 0, first=True)

    def tile_body(t, carry):
        slot = t % 2
        t_next = (t + 1) % ntiles
        load_queries(t_next, 1 - slot)
        lead = min(2, nchunks - 2)
        for c in range(lead):
            step(t, t_next, slot, c)

        def pair_body(j, inner):
            for par in range(2):
                c = 2 * j + par
                scores(t, c + 2, slot, par)
                softmax(slot, 1 - par, first=False)
                values(c, slot, par, first=False)
            return inner

        lax.fori_loop(lead // 2, (nchunks - 2) // 2, pair_body, 0)
        for c in range(nchunks - 2, nchunks):
            step(t, t_next, slot, c)
        finalize(t, slot)
        return carry

    lax.fori_loop(0, ntiles, tile_body, 0)


def _flash_attention(q, k, v, slopes, lam_pack, subg, *, nbatch, seq, ngroups, nstack, alibi, lam_init,
                     nrows=1024):
    qw = (nstack // 2) * LANES
    tq = nrows // nstack
    tk = min(1024, seq // 2)
    kernel = functools.partial(_flash_kernel, tq=tq, tk=tk, seq=seq, nstack=nstack, alibi=alibi,
                               lam_init=lam_init)
    grid_spec = pltpu.PrefetchScalarGridSpec(
        num_scalar_prefetch=1,
        grid=(nbatch, ngroups),
        in_specs=[pl.BlockSpec((None, seq, qw), lambda b, g, s: (b, 0, g)),
                  pl.BlockSpec((None, seq, LANES), lambda b, g, s: (b, 0, g)),
                  pl.BlockSpec((None, seq, LANES), lambda b, g, s: (b, 0, g)),
                  pl.BlockSpec(lam_pack.shape, lambda b, g, s: (0, 0)),
                  pl.BlockSpec(subg.shape, lambda b, g, s: (0, 0))],
        out_specs=pl.BlockSpec((None, seq, qw), lambda b, g, s: (b, 0, g)),
        scratch_shapes=[pltpu.VMEM((2, nrows, LANES), BF16),
                        pltpu.VMEM((seq, 2 * LANES), BF16),
                        pltpu.VMEM((2, nrows, LANES), F32),
                        pltpu.VMEM((2, nrows, 2 * LANES), F32),
                        pltpu.VMEM((nrows, tk), F32), pltpu.VMEM((nrows, tk), F32),
                        pltpu.VMEM((nrows, tk), BF16), pltpu.VMEM((nrows, tk), BF16),
                        pltpu.VMEM((nrows, LANES), F32), pltpu.VMEM((nrows, LANES), F32)])
    return pl.pallas_call(
        kernel,
        grid_spec=grid_spec,
        out_shape=jax.ShapeDtypeStruct((nbatch, seq, ngroups * qw), BF16),
        compiler_params=_cparams(("arbitrary", "arbitrary")),
        name="flash_alibi" if alibi else "flash_gqa",
    )(slopes, q, k, v, lam_pack, subg)


def _natten_kernel(q_ref, k_ref, v_ref, bias_ref, o_ref, v_sc, s0_sc, s1_sc, *, seq):
    lane = lax.broadcasted_iota(jnp.int32, (1, LANES), 1)
    low_half = lane < HEAD_DIM
    ntiles = seq // NAT_TQ
    rows = seq // GRID_W
    s_bufs = (s0_sc, s1_sc)

    v_sc[:, 0:LANES] = v_ref[...]
    v_sc[:, LANES:2 * LANES] = jnp.ones((seq, LANES), BF16)

    def window(t):
        w0 = jnp.clip(t * NAT_QROWS - WIN_H // 2, 0, rows - NAT_KROWS)
        return pl.multiple_of(w0 * GRID_W, NAT_TQ)

    def scores(t, par):
        q = q_ref[pl.ds(pl.multiple_of(t * NAT_TQ, NAT_TQ), NAT_TQ), :]
        zero = jnp.zeros_like(q)
        q2 = jnp.concatenate([jnp.where(low_half, q, zero), jnp.where(low_half, zero, q)], axis=0)
        s = lax.dot_general(q2, k_ref[pl.ds(window(t), NAT_TK), :], (((1,), (1,)), ((), ())),
                            preferred_element_type=F32)
        cls = jnp.where(t == 0, 0, jnp.where(t == ntiles - 1, 2, 1))
        bias = bias_ref[cls].astype(F32).reshape(2 * NAT_TQ, NAT_TK)
        s_bufs[par][...] = s + bias

    def finish(t, par):
        s = s_bufs[par][...]
        m = jnp.max(s, axis=-1, keepdims=True)
        p = jnp.exp2(s - m).astype(BF16)
        acc = jnp.dot(p, v_sc[pl.ds(window(t), NAT_TK), :], preferred_element_type=F32)
        o = acc[:, 0:LANES] * (1.0 / acc[:, LANES:2 * LANES])
        out = jnp.where(low_half, o[0:NAT_TQ, :], o[NAT_TQ:, :])
        o_ref[pl.ds(pl.multiple_of(t * NAT_TQ, NAT_TQ), NAT_TQ), :] = out.astype(o_ref.dtype)

    scores(jnp.int32(0), 0)

    unroll = 4 if ntiles % 4 == 0 else 2

    def group_body(u, carry):
        for k in range(unroll):
            t = unroll * u + k
            scores((t + 1) % ntiles, (k + 1) % 2)
            finish(t, k % 2)
        return carry

    lax.fori_loop(0, ntiles // unroll, group_body, 0)


def _natten(q, k, v, bias_tab, *, nbatch, seq):
    npairs = C_HEADS // 2
    blk = pl.BlockSpec((None, seq, LANES), lambda p, b: (b, 0, p))
    return pl.pallas_call(
        functools.partial(_natten_kernel, seq=seq),
        grid=(npairs, nbatch),
        in_specs=[blk, blk, blk,
                  pl.BlockSpec((3, None, 2, NAT_TQ, NAT_TK), lambda p, b: (0, p, 0, 0, 0))],
        out_specs=blk,
        out_shape=jax.ShapeDtypeStruct((nbatch, seq, C_HEADS * HEAD_DIM), BF16),
        scratch_shapes=[pltpu.VMEM((seq, 2 * LANES), BF16),
                        pltpu.VMEM((2 * NAT_TQ, NAT_TK), F32), pltpu.VMEM((2 * NAT_TQ, NAT_TK), F32)],
        compiler_params=_cparams(("arbitrary", "arbitrary")),
        name="natten",
    )(q, k, v, bias_tab)


def _natten_bias_table(rpb, seq):
    rows = seq // GRID_W
    ntiles = rows // NAT_QROWS
    col = jnp.arange(GRID_W)
    cstart = jnp.clip(col - WIN_W // 2, 0, GRID_W - WIN_W)
    col_valid = (col[None, :] >= cstart[:, None]) & (col[None, :] < cstart[:, None] + WIN_W)
    dc_idx = jnp.clip(col[None, :] - col[:, None] + WIN_W - 1, 0, 2 * WIN_W - 2)
    rpb_cols = rpb[:, :, dc_idx]
    tabs = []
    for tile in (0, 1, ntiles - 1):
        r = tile * NAT_QROWS + jnp.arange(NAT_QROWS)
        w0 = int(np.clip(tile * NAT_QROWS - WIN_H // 2, 0, rows - NAT_KROWS))
        kr = w0 + jnp.arange(NAT_KROWS)
        rstart = jnp.clip(r - WIN_H // 2, 0, rows - WIN_H)
        row_valid = (kr[None, :] >= rstart[:, None]) & (kr[None, :] < rstart[:, None] + WIN_H)
        dr_idx = jnp.clip(kr[None, :] - r[:, None] + WIN_H - 1, 0, 2 * WIN_H - 2)
        pick = (dr_idx[:, :, None] == jnp.arange(2 * WIN_H - 1)[None, None, :]).astype(F32)
        b = jnp.einsum("qkd,hdcx->hqckx", pick, rpb_cols * LOG2E,
                       precision=lax.Precision.HIGHEST)
        valid = row_valid[:, None, :, None] & col_valid[None, :, None, :]
        b = jnp.where(valid[None], b, -jnp.inf)
        tabs.append(b.reshape(C_HEADS, NAT_TQ, NAT_TK))
    return jnp.stack(tabs).reshape(3, C_HEADS // 2, 2, NAT_TQ, NAT_TK).astype(BF16)


def _ffn_kernel(x_ref, g1_ref, sh_ref, sc_ref, g2_ref, ng_ref, ma_ref, mb_ref, woa_ref, wob_ref,
                w1_ref, w3_ref, w2_ref, o_ref):
    y = (jnp.dot(ma_ref[...], woa_ref[...], preferred_element_type=F32)
         + jnp.dot(mb_ref[...], wob_ref[...], preferred_element_type=F32))
    x1 = x_ref[...] + g1_ref[...] * y
    h = _norm_mod(x1, ng_ref[...], sh_ref[...], sc_ref[...]).astype(BF16)
    y = None
    for f in range(D_FF // FFN_CHUNK):
        cols = slice(f * FFN_CHUNK, (f + 1) * FFN_CHUNK)
        a = jnp.dot(h, w1_ref[:, cols], preferred_element_type=F32)
        b = jnp.dot(h, w3_ref[:, cols], preferred_element_type=F32)
        d = jnp.dot((a * _sigmoid(a) * b).astype(BF16), w2_ref[cols, :], preferred_element_type=F32)
        y = d if y is None else y + d
    o_ref[...] = x1 + g2_ref[...] * y


def _ffn(x2, mod, norm_g, mixes, wos, w1, w3, w2, seq, tm=512):
    t = x2.shape[0]
    row = lambda i: (i, 0)
    resident = lambda a: pl.BlockSpec(a.shape, lambda i: (0, 0), pipeline_mode=pl.Buffered(1))
    return pl.pallas_call(
        _ffn_kernel,
        grid=(t // tm,),
        in_specs=[pl.BlockSpec((tm, D_MODEL), row),
                  _mod_spec(2, tm, seq), _mod_spec(3, tm, seq), _mod_spec(4, tm, seq), _mod_spec(5, tm, seq),
                  pl.BlockSpec((1, D_MODEL), lambda i: (0, 0)),
                  pl.BlockSpec((tm, mixes[0].shape[1]), row), pl.BlockSpec((tm, mixes[1].shape[1]), row),
                  resident(wos[0]), resident(wos[1]), resident(w1), resident(w3), resident(w2)],
        out_specs=pl.BlockSpec((tm, D_MODEL), row),
        out_shape=jax.ShapeDtypeStruct((t, D_MODEL), F32),
        compiler_params=_cparams(("arbitrary",)),
        name="outproj_ffn_swiglu",
    )(x2, mod, mod, mod, mod, norm_g.reshape(1, D_MODEL), mixes[0], mixes[1], wos[0], wos[1], w1, w3, w2)


def _router_kernel(x_ref, g1_ref, sh_ref, sc_ref, g_ref, mix_ref, wo_ref, rw_ref, rb_ref,
                   x1_ref, h_ref, meta_ref):
    lane = lax.broadcasted_iota(jnp.int32, (1, LANES), 1).astype(F32)
    nsub = 4
    rb = x_ref.shape[0] // nsub
    for r in range(nsub):
        rows = slice(r * rb, (r + 1) * rb)
        x1 = x_ref[rows, :] + g1_ref[...] * jnp.dot(mix_ref[rows, :], wo_ref[...], preferred_element_type=F32)
        x1_ref[rows, :] = x1
        h = _norm_mod(x1, g_ref[...], sh_ref[...], sc_ref[...])
        packed = _pack_halves(h[:, 0:PACK_W], h[:, PACK_W:D_MODEL])
        for s in range(SC_PIECES):
            h_ref[s, rows, :] = packed[:, s * SC_ROW_WORDS:(s + 1) * SC_ROW_WORDS]
        logits = jnp.dot(h, rw_ref[...], preferred_element_type=F32,
                         precision=lax.Precision.HIGHEST) + rb_ref[...]
        logits = jnp.where(lane < N_EXPERTS, logits, -jnp.inf)
        m1 = jnp.max(logits, axis=-1, keepdims=True)
        i1 = jnp.min(jnp.where(logits == m1, lane, float(LANES)), axis=-1, keepdims=True)
        rest = jnp.where(lane == i1, -jnp.inf, logits)
        m2 = jnp.max(rest, axis=-1, keepdims=True)
        i2 = jnp.min(jnp.where(rest == m2, lane, float(LANES)), axis=-1, keepdims=True)
        e = jnp.exp(m2 - m1)
        g1 = 1.0 / (1.0 + e)
        g2 = e * g1
        meta_ref[rows, :] = jnp.where(lane == 0, i1, jnp.where(lane == 1, i2, jnp.where(
            lane == 2, g1, jnp.where(lane == 3, g2, 0.0))))


def _router(x2, mod, norm_g, mix, wo, rw_pad, rb_pad, seq, tm=512):
    t = x2.shape[0]
    row = lambda i: (i, 0)
    const = lambda i: (0, 0)
    return pl.pallas_call(
        _router_kernel,
        grid=(t // tm,),
        in_specs=[pl.BlockSpec((tm, D_MODEL), row),
                  _mod_spec(2, tm, seq), _mod_spec(3, tm, seq), _mod_spec(4, tm, seq),
                  pl.BlockSpec((1, D_MODEL), const),
                  pl.BlockSpec((tm, D_MODEL), row), pl.BlockSpec((D_MODEL, D_MODEL), const),
                  pl.BlockSpec((D_MODEL, LANES), const), pl.BlockSpec((1, LANES), const)],
        out_specs=[pl.BlockSpec((tm, D_MODEL), row),
                   pl.BlockSpec((SC_PIECES, tm, SC_ROW_WORDS), lambda i: (0, i, 0)),
                   pl.BlockSpec((tm, LANES), row)],
        out_shape=[jax.ShapeDtypeStruct((t, D_MODEL), F32),
                   jax.ShapeDtypeStruct((SC_PIECES, t, SC_ROW_WORDS), jnp.int32),
                   jax.ShapeDtypeStruct((t, LANES), F32)],
        compiler_params=_cparams(("arbitrary",)),
        name="outproj_moe_router",
    )(x2, mod, mod, mod, norm_g.reshape(1, D_MODEL), mix, wo, rw_pad, rb_pad)


def _moe_kernel(te_ref, nt_ref, x_ref, w1_ref, w3_ref, w2a_ref, w2b_ref, o_ref, x_sc, g_sc, *, nf, tf):
    i = pl.program_id(0)
    j = pl.program_id(1)
    active = i < nt_ref[0]

    @pl.when(jnp.logical_and(active, j == 0))
    def _():
        x_sc[...] = _unpack_halves(_load_pieces(x_ref)).astype(BF16)

    @pl.when(jnp.logical_and(active, j < nf))
    def _():
        x = x_sc[...]
        a = jnp.dot(x, w1_ref[...], preferred_element_type=F32)
        b = jnp.dot(x, w3_ref[...], preferred_element_type=F32)
        g_sc[j] = (a * _sigmoid(a) * b).astype(BF16)

    @pl.when(jnp.logical_and(active, j >= nf))
    def _():
        ya, yb = None, None
        for f in range(nf):
            g = g_sc[f]
            da = jnp.dot(g, w2a_ref[f * tf:(f + 1) * tf, :], preferred_element_type=F32)
            db = jnp.dot(g, w2b_ref[f * tf:(f + 1) * tf, :], preferred_element_type=F32)
            ya = da if ya is None else ya + da
            yb = db if yb is None else yb + db
        o_ref[...] = _pack_halves(ya, yb)

    @pl.when(jnp.logical_and(jnp.logical_not(active), j >= nf))
    def _():
        o_ref[...] = jnp.zeros(o_ref.shape, o_ref.dtype)


def _moe_grouped(xs, tile_expert, num_tiles, w1, w3, w2, tg, tf=512):
    p = xs.shape[1]
    nf = w1.shape[2] // tf
    tn = SC_ROW_WORDS
    nb = SC_PIECES
    fcl = lambda j: jnp.minimum(j, nf - 1)
    ncl = lambda j: jnp.maximum(j - nf, 0)

    def w2_map(col0, switch):
        def index(i, j, te, nt):
            early = j < switch
            expert = jnp.where(early, te[jnp.maximum(i - 1, 0)], te[i])
            return (expert, 0, jnp.where(early, col0 + nb - 1, col0 + ncl(j)))
        return index

    grid_spec = pltpu.PrefetchScalarGridSpec(
        num_scalar_prefetch=2,
        grid=(p // tg, nf + nb),
        in_specs=[pl.BlockSpec((SC_PIECES, tg, SC_ROW_WORDS), lambda i, j, te, nt: (0, i, 0)),
                  pl.BlockSpec((None, D_MODEL, tf), lambda i, j, te, nt: (te[i], 0, fcl(j))),
                  pl.BlockSpec((None, D_MODEL, tf), lambda i, j, te, nt: (te[i], 0, fcl(j))),
                  pl.BlockSpec((None, nf * tf, tn), w2_map(0, nf // 2)),
                  pl.BlockSpec((None, nf * tf, tn), w2_map(nb, nf // 2 + 2))],
        out_specs=pl.BlockSpec((None, tg, tn), lambda i, j, te, nt: (ncl(j), i, 0)),
        scratch_shapes=[pltpu.VMEM((tg, D_MODEL), BF16), pltpu.VMEM((nf, tg, tf), BF16)])
    return pl.pallas_call(
        functools.partial(_moe_kernel, nf=nf, tf=tf),
        grid_spec=grid_spec,
        out_shape=jax.ShapeDtypeStruct((SC_PIECES, p, SC_ROW_WORDS), jnp.int32),
        compiler_params=_cparams(("arbitrary", "arbitrary")),
        name="moe_grouped",
    )(tile_expert, num_tiles, xs, w1, w3, w2, w2)


def _combine_kernel(x_ref, gate_ref, meta_ref, y1_ref, y2_ref, o_ref):
    meta = meta_ref[...]
    moe = (meta[:, 2:3] * _unpack_halves(_load_pieces(y1_ref))
           + meta[:, 3:4] * _unpack_halves(_load_pieces(y2_ref)))
    o_ref[...] = x_ref[...] + gate_ref[...] * moe


def _combine(x2, mod, meta, yg, seq, tm=512):
    t = x2.shape[0]
    nt = t // tm
    row = lambda i: (i, 0)
    return pl.pallas_call(
        _combine_kernel,
        grid=(nt,),
        in_specs=[pl.BlockSpec((tm, D_MODEL), row), _mod_spec(5, tm, seq),
                  pl.BlockSpec((tm, LANES), row),
                  pl.BlockSpec((SC_PIECES, tm, SC_ROW_WORDS), lambda i: (0, i, 0)),
                  pl.BlockSpec((SC_PIECES, tm, SC_ROW_WORDS), lambda i: (0, nt + i, 0))],
        out_specs=pl.BlockSpec((tm, D_MODEL), row),
        out_shape=jax.ShapeDtypeStruct((t, D_MODEL), F32),
        compiler_params=_cparams(("arbitrary",)),
        name="moe_combine",
    )(x2, mod, meta, yg, yg)


def _sc_gather(table, idx):
    n = idx.shape[0]
    nrows = table.shape[1]
    pieces = n * SC_PIECES
    idx_pieces = jnp.concatenate([idx + s * nrows for s in range(SC_PIECES)]).reshape(1, pieces)
    mesh = plsc.VectorSubcoreMesh(core_axis_name="core", subcore_axis_name="subcore")

    @pl.kernel(out_type=jax.ShapeDtypeStruct((pieces, SC_ROW_WORDS), table.dtype), mesh=mesh, scratch_types=[])
    def gather_kernel(table_hbm, idx_hbm, out_hbm):
        def body(idx_vmem, out_vmem):
            pltpu.sync_copy(table_hbm.at[idx_vmem.at[0]], out_vmem)

        pltpu.emit_pipeline(
            body,
            grid=(pieces // SC_WINDOW,),
            in_specs=[pl.BlockSpec((1, SC_WINDOW), lambda i: (0, i))],
            out_specs=[pl.BlockSpec((SC_WINDOW, SC_ROW_WORDS), lambda i: (i, 0))],
            core_axis_name=("core", "subcore"),
            dimension_semantics=(pltpu.PARALLEL,),
        )(idx_hbm, out_hbm)

    out = gather_kernel(table.reshape(SC_PIECES * nrows, SC_ROW_WORDS), idx_pieces)
    return out.reshape(SC_PIECES, n, SC_ROW_WORDS)


def _sc_invert(pos, nrows):
    n = pos.shape[0]
    src = jnp.broadcast_to(jnp.arange(n, dtype=jnp.int32)[:, None], (n, LANES))
    mesh = plsc.VectorSubcoreMesh(core_axis_name="core", subcore_axis_name="subcore")

    @pl.kernel(out_type=jax.ShapeDtypeStruct((nrows, LANES), jnp.int32), mesh=mesh, scratch_types=[])
    def scatter_kernel(src_hbm, idx_hbm, out_hbm):
        def body(src_vmem, idx_vmem):
            pltpu.sync_copy(src_vmem, out_hbm.at[idx_vmem.at[0]])

        pltpu.emit_pipeline(
            body,
            grid=(n // SC_WINDOW,),
            in_specs=[pl.BlockSpec((SC_WINDOW, LANES), lambda i: (i, 0)),
                      pl.BlockSpec((1, SC_WINDOW), lambda i: (0, i))],
            out_specs=[],
            core_axis_name=("core", "subcore"),
            dimension_semantics=(pltpu.PARALLEL,),
        )(src_hbm, idx_hbm)

    return scatter_kernel(src, pos.reshape(1, n))[:, 0]


def _moe(x2, mod, norm_g, mix, wo, rw_pad, rb_pad, w1, w3, w2, seq, tg=1024):
    t = x2.shape[0]
    x2, h, meta = _router(x2, mod, norm_g, mix, wo, rw_pad, rb_pad, seq)
    e_flat = meta[:, 0:2].astype(jnp.int32).reshape(-1)
    onehot = (e_flat[:, None] == jnp.arange(N_EXPERTS)[None, :]).astype(jnp.int32)
    csum = jnp.cumsum(onehot, axis=0)
    counts = csum[-1]
    rank = jnp.take_along_axis(csum, e_flat[:, None], axis=1)[:, 0] - 1
    padded = ((counts + tg - 1) // tg) * tg
    pend = jnp.cumsum(padded)
    pos = (pend - padded)[e_flat] + rank
    p_rows = 2 * t + N_EXPERTS * tg
    row_token = jnp.clip(_sc_invert(pos, p_rows) // 2, 0, t - 1)
    tile_start = jnp.arange(p_rows // tg, dtype=jnp.int32) * tg
    tile_expert = jnp.minimum(jnp.sum((tile_start[:, None] >= pend[None, :]).astype(jnp.int32), axis=1),
                              N_EXPERTS - 1)
    num_tiles = (pend[-1] // tg).astype(jnp.int32).reshape(1)
    xs = _sc_gather(h, row_token)
    ys = _moe_grouped(xs, tile_expert, num_tiles, w1, w3, w2, tg)
    yg = _sc_gather(ys, jnp.concatenate([pos[0::2], pos[1::2]]))
    return _combine(x2, mod, meta, yg, seq)


def _rope_tables(seq):
    t = np.arange(seq)
    lane = np.arange(LANES)
    d = lane % HEAD_DIM
    pos = np.where((d // 32)[None, :] == 0, (t // GRID_W)[:, None], (t % GRID_W)[:, None]).astype(np.float32)
    inv = (ROPE_THETA ** (-np.arange(16, dtype=np.float32) / 16)).astype(np.float32)
    ang = pos * inv[(d % 16)][None, :]
    return jnp.asarray(np.cos(ang), F32), jnp.asarray(np.sin(ang), F32)


def _pair_gain(g):
    return jnp.concatenate([g, g]).astype(F32)


def _prepare(p, seq):
    even, odd = {}, {}
    w_in = p["w_in_even"][0]
    qa, ka, va, qb, kb, vb = jnp.split(w_in, [512, 640, 768, 1280, 1792], axis=1)
    dup = lambda w: jnp.concatenate([w[:, 0:64], w[:, 0:64], w[:, 64:128], w[:, 64:128]], axis=1)
    even["w_in"] = jnp.concatenate([qa, qb, dup(ka), dup(va), kb, vb], axis=1).astype(BF16)
    gains = jnp.zeros((8, LANES), F32)
    gains = gains.at[0].set(_pair_gain(p["qnorm_a"][0])).at[1].set(_pair_gain(p["knorm_a"][0]))
    gains = gains.at[2].set(_pair_gain(p["qnorm_b"][0])).at[3].set(_pair_gain(p["knorm_b"][0]))
    even["gains"] = gains
    qscale = SCALE * LOG2E
    even["groups"] = ((512, 0, True, qscale), (512, 2, False, qscale), (256, 1, True, 1.0),
                      (256, None, False, 1.0), (512, 3, False, 1.0), (512, None, False, 1.0))
    lam = jnp.zeros((8, LANES), F32)
    for r, name in enumerate(("lam_q1", "lam_k1", "lam_q2", "lam_k2")):
        lam = lam.at[r, 0:HEAD_DIM].set(p[name][0])
    even["lam"] = lam
    even["subg"] = p["subln_b"][0].reshape(1, LANES).astype(F32)
    wo = p["w_out_even"][0].astype(BF16)
    even["wo"] = (wo[0:512], wo[512:1024])
    even["slopes"] = jnp.asarray(LOG2E * 2.0 ** (-8.0 * (np.arange(B_HEADS) + 1.0) / B_HEADS), F32)
    for name in ("ffn_w1", "ffn_w3", "ffn_w2"):
        even[name] = p[name][0].astype(BF16)
    for name in ("ada_w", "ada_b", "norm_mix", "norm_ffn"):
        even[name] = p[name + "_even"][0]
        odd[name] = p[name + "_odd"][0]

    odd["w_qkv"] = p["w_qkv_odd"][0].astype(BF16)
    gains = jnp.zeros((8, LANES), F32)
    odd["gains"] = gains.at[0].set(_pair_gain(p["qnorm_c"][0])).at[1].set(_pair_gain(p["knorm_c"][0]))
    odd["groups"] = ((1024, 0, False, qscale), (1024, 1, False, 1.0), (1024, None, False, 1.0))
    odd["bias_tab"] = _natten_bias_table(p["rpb_c"][0], seq)
    odd["wo"] = (p["w_out_odd"][0].astype(BF16),)
    odd["rw"] = jnp.zeros((D_MODEL, LANES), F32).at[:, 0:N_EXPERTS].set(p["router_w"][0])
    odd["rb"] = jnp.zeros((1, LANES), F32).at[0, 0:N_EXPERTS].set(p["router_b"][0])
    for name in ("moe_w1", "moe_w3", "moe_w2"):
        odd[name] = p[name][0].astype(BF16)
    cos_t, sn_t = _rope_tables(seq)
    gmat = jnp.asarray(np.kron(np.eye(4), np.full((HEAD_DIM, HEAD_DIM), 1.0 / HEAD_DIM)), BF16)
    return even, odd, (cos_t, sn_t, gmat)


def _even_layer(x2, c, ev, shared, nbatch, seq):
    cos_t, sn_t, gmat = shared
    mod = _ada_modulation(c, ev["ada_w"], ev["ada_b"])
    qa, qb, ka, va, kb, vb = _projection(x2, mod, ev["norm_mix"], ev["w_in"], cos_t, sn_t, ev["gains"], gmat,
                                         ev["groups"], seq)
    shp = lambda a: a.reshape(nbatch, seq, a.shape[1])
    lam_init = 0.8 - 0.6 * math.exp(-0.3 * 0)
    mix_a = _flash_attention(shp(qa), shp(ka), shp(va), ev["slopes"], ev["lam"], ev["subg"], nbatch=nbatch, seq=seq,
                             ngroups=A_KV_HEADS, nstack=4, alibi=False, lam_init=lam_init)
    mix_b = _flash_attention(shp(qb), shp(kb), shp(vb), ev["slopes"], ev["lam"], ev["subg"], nbatch=nbatch, seq=seq,
                             ngroups=B_HEADS, nstack=2, alibi=True, lam_init=lam_init)
    t = nbatch * seq
    return _ffn(x2, mod, ev["norm_ffn"], (mix_a.reshape(t, -1), mix_b.reshape(t, -1)), ev["wo"],
                ev["ffn_w1"], ev["ffn_w3"], ev["ffn_w2"], seq)


def _odd_layer(x2, c, od, shared, nbatch, seq):
    cos_t, sn_t, gmat = shared
    mod = _ada_modulation(c, od["ada_w"], od["ada_b"])
    q, k, v = _projection(x2, mod, od["norm_mix"], od["w_qkv"], cos_t, sn_t, od["gains"], gmat, od["groups"], seq)
    shp = lambda a: a.reshape(nbatch, seq, a.shape[1])
    mix = _natten(shp(q), shp(k), shp(v), od["bias_tab"], nbatch=nbatch, seq=seq)
    return _moe(x2, mod, od["norm_ffn"], mix.reshape(nbatch * seq, -1), od["wo"][0], od["rw"], od["rb"],
                od["moe_w1"], od["moe_w3"], od["moe_w2"], seq)


def _trunk(x, c, ev, od, shared):
    nbatch, seq, d = x.shape
    x2 = x.reshape(nbatch * seq, d)
    x2 = _even_layer(x2, c, ev, shared, nbatch, seq)
    x2 = _odd_layer(x2, c, od, shared, nbatch, seq)
    return x2.reshape(nbatch, seq, d)


def kernel(x_prompt, x_sample, c_prompt, c_sample, ada_w_even, ada_b_even, norm_mix_even, norm_ffn_even, w_in_even, qnorm_a, knorm_a, qnorm_b, knorm_b, lam_q1, lam_k1, lam_q2, lam_k2, subln_b, w_out_even, ffn_w1, ffn_w3, ffn_w2, ada_w_odd, ada_b_odd, norm_mix_odd, norm_ffn_odd, w_qkv_odd, qnorm_c, knorm_c, rpb_c, w_out_odd, router_w, router_b, moe_w1, moe_w3, moe_w2):
    params = dict(ada_w_even=ada_w_even, ada_b_even=ada_b_even, norm_mix_even=norm_mix_even,
                  norm_ffn_even=norm_ffn_even, w_in_even=w_in_even, qnorm_a=qnorm_a, knorm_a=knorm_a,
                  qnorm_b=qnorm_b, knorm_b=knorm_b, lam_q1=lam_q1, lam_k1=lam_k1, lam_q2=lam_q2, lam_k2=lam_k2,
                  subln_b=subln_b, w_out_even=w_out_even, ffn_w1=ffn_w1, ffn_w3=ffn_w3, ffn_w2=ffn_w2,
                  ada_w_odd=ada_w_odd, ada_b_odd=ada_b_odd, norm_mix_odd=norm_mix_odd, norm_ffn_odd=norm_ffn_odd,
                  w_qkv_odd=w_qkv_odd, qnorm_c=qnorm_c, knorm_c=knorm_c, rpb_c=rpb_c, w_out_odd=w_out_odd,
                  router_w=router_w, router_b=router_b, moe_w1=moe_w1, moe_w3=moe_w3, moe_w2=moe_w2)
    seq = x_prompt.shape[1]
    ev, od, shared = _prepare(params, seq)
    y_prompt = _trunk(x_prompt, c_prompt, ev, od, shared)
    y_sample = _trunk(x_sample, c_sample, ev, od, shared)
    return (y_prompt, y_sample)
```

```python
import functools
import math

import numpy as np
import jax
import jax.numpy as jnp
from jax import lax
from jax.experimental import pallas as pl
from jax.experimental.pallas import tpu as pltpu
from jax.experimental.pallas import tpu_sc as plsc

F32 = jnp.float32
BF16 = jnp.bfloat16

D_MODEL = 1024
HEAD_DIM = 64
LANES = 128
SCALE = HEAD_DIM ** -0.5
LOG2E = 1.4426950408889634
GRID_W = 64
EPS = 1e-6
ROPE_THETA = 10000.0
A_Q_HEADS = 8
A_KV_HEADS = 2
B_HEADS = 4
C_HEADS = 16
WIN_H = 8
WIN_W = 16
N_EXPERTS = 8
D_FF = 2816
D_FF_EXPERT = 3584
VMEM_LIMIT = 56 * 1024 * 1024

NAT_QROWS = 4
NAT_KROWS = 12
NAT_TQ = NAT_QROWS * GRID_W
NAT_TK = NAT_KROWS * GRID_W
PACK_W = D_MODEL // 2
SC_WINDOW = 128
SC_ROW_WORDS = 256
SC_PIECES = PACK_W // SC_ROW_WORDS
FFN_CHUNK = 1408


def _cparams(sem):
    return pltpu.CompilerParams(dimension_semantics=sem, vmem_limit_bytes=VMEM_LIMIT)


def _norm_mod(x, g, shift, scale):
    ms = jnp.mean(x * x, axis=-1, keepdims=True)
    y = x * lax.rsqrt(ms + EPS) * g
    return y * (1.0 + scale) + shift


def _head_norm(x, gain, gmat):
    ms = jnp.dot((x * x).astype(BF16), gmat, preferred_element_type=F32)
    return x * lax.rsqrt(ms + EPS) * gain


def _rope(x, cos, sn, first_quarter):
    up = pltpu.roll(x, x.shape[1] - 16, 1)
    down = pltpu.roll(x, 16, 1)
    return x * cos + sn * jnp.where(first_quarter, -up, down)


def _sigmoid(a):
    return 1.0 / (1.0 + jnp.exp(-a))


def _pack_halves(a, b):
    hi = lax.bitcast_convert_type(a.astype(BF16).astype(F32), jnp.int32)
    lo = lax.bitcast_convert_type(b.astype(BF16).astype(F32), jnp.int32)
    return hi | lax.shift_right_logical(lo, jnp.full_like(lo, 16))


def _unpack_halves(w):
    hi = lax.bitcast_convert_type(w & jnp.int32(-65536), F32)
    lo = lax.bitcast_convert_type(lax.shift_left(w, jnp.full_like(w, 16)), F32)
    return jnp.concatenate([hi, lo], axis=1)


def _load_pieces(ref):
    return jnp.concatenate([ref[s] for s in range(SC_PIECES)], axis=1)


def _ada_kernel(c_ref, w_ref, b_ref, o_ref):
    c = c_ref[...]
    s = c * _sigmoid(c)
    o_ref[...] = jnp.dot(s, w_ref[...], preferred_element_type=F32,
                         precision=lax.Precision.HIGHEST) + b_ref[...]


def _ada_modulation(c, w, b):
    nb, d = c.shape
    n = w.shape[1]
    tn = 512
    mod = pl.pallas_call(
        _ada_kernel,
        grid=(n // tn,),
        in_specs=[pl.BlockSpec((nb, d), lambda j: (0, 0)),
                  pl.BlockSpec((d, tn), lambda j: (0, j)),
                  pl.BlockSpec((1, tn), lambda j: (0, j))],
        out_specs=pl.BlockSpec((nb, tn), lambda j: (0, j)),
        out_shape=jax.ShapeDtypeStruct((nb, n), F32),
        compiler_params=_cparams(("arbitrary",)),
        name="ada_mod",
    )(c, w, b.reshape(1, n))
    return mod.reshape(nb, 6, 1, d)


def _mod_spec(k, tm, seq):
    return pl.BlockSpec((None, None, 1, D_MODEL), lambda i, *_: ((i * tm) // seq, k, 0, 0))


def _proj_kernel(x_ref, sh_ref, sc_ref, g_ref, w_ref, cos_ref, sn_ref, gains_ref, gmat_ref, *o_refs,
                 groups):
    wide = 2 * LANES
    lane = lax.broadcasted_iota(jnp.int32, (1, wide), 1)
    first_quarter = (lane % 32) < 16
    gmat = gmat_ref[...]
    h = _norm_mod(x_ref[...], g_ref[...], sh_ref[...], sc_ref[...])
    y = jnp.dot(h.astype(BF16), w_ref[...], preferred_element_type=F32)
    cos = jnp.concatenate([cos_ref[...]] * 2, axis=1)
    sn = jnp.concatenate([sn_ref[...]] * 2, axis=1)
    off = 0
    for o_ref, (width, gain_row, rope, mult) in zip(o_refs, groups):
        if gain_row is None:
            o_ref[...] = y[:, off:off + width].astype(o_ref.dtype)
        else:
            gain = jnp.concatenate([gains_ref[gain_row:gain_row + 1, :]] * 2, axis=1)
            for t in range(width // wide):
                z = _head_norm(y[:, off + t * wide: off + (t + 1) * wide], gain, gmat)
                if rope:
                    z = _rope(z, cos, sn, first_quarter)
                if mult != 1.0:
                    z = z * mult
                o_ref[:, t * wide:(t + 1) * wide] = z.astype(o_ref.dtype)
        off += width


def _projection(x2, mod, norm_g, w, cos_t, sn_t, gains, gmat, groups, seq, tm=512):
    t = x2.shape[0]
    n = w.shape[1]
    nseq = seq // tm
    row = lambda i: (i, 0)
    const = lambda i: (0, 0)
    tab = lambda i: (i % nseq, 0)
    return pl.pallas_call(
        functools.partial(_proj_kernel, groups=groups),
        grid=(t // tm,),
        in_specs=[pl.BlockSpec((tm, D_MODEL), row),
                  _mod_spec(0, tm, seq), _mod_spec(1, tm, seq),
                  pl.BlockSpec((1, D_MODEL), const),
                  pl.BlockSpec((D_MODEL, n), const),
                  pl.BlockSpec((tm, LANES), tab), pl.BlockSpec((tm, LANES), tab),
                  pl.BlockSpec(gains.shape, const),
                  pl.BlockSpec(gmat.shape, const)],
        out_specs=[pl.BlockSpec((tm, g[0]), row) for g in groups],
        out_shape=[jax.ShapeDtypeStruct((t, g[0]), BF16) for g in groups],
        compiler_params=_cparams(("arbitrary",)),
        name="norm_mod_proj",
    )(x2, mod, mod, norm_g.reshape(1, D_MODEL), w, cos_t, sn_t, gains, gmat)


def _flash_kernel(slope_ref, q_ref, k_ref, v_ref, lam_ref, subg_ref, o_ref,
                  q_sc, v_sc, m_sc, acc_sc, s0_sc, s1_sc, p0_sc, p1_sc, a0_sc, a1_sc,
                  *, tq, tk, seq, nstack, alibi, lam_init):
    g = pl.program_id(1)
    lane = lax.broadcasted_iota(jnp.int32, (1, LANES), 1)
    low_half = lane < HEAD_DIM
    s_bufs, p_bufs, a_bufs = (s0_sc, s1_sc), (p0_sc, p1_sc), (a0_sc, a1_sc)
    nchunks = seq // tk
    ntiles = seq // tq
    nrows = nstack * tq

    v_sc[:, 0:LANES] = v_ref[...]
    v_sc[:, LANES:2 * LANES] = jnp.ones((seq, LANES), BF16)

    if alibi:
        rc = (lax.broadcasted_iota(jnp.int32, (tq, tk), 0)
              - lax.broadcasted_iota(jnp.int32, (tq, tk), 1)).astype(F32)
        neg_slope = -slope_ref[g]
        lp = lam_ref[...]
        l1 = jnp.sum(lp[0:1, :] * lp[1:2, :], axis=-1, keepdims=True)
        l2 = jnp.sum(lp[2:3, :] * lp[3:4, :], axis=-1, keepdims=True)
        lam = jnp.exp(l1) - jnp.exp(l2) + lam_init

    def tile_rows(t):
        return pl.ds(t * tq, tq) if isinstance(t, int) else pl.ds(pl.multiple_of(t * tq, tq), tq)

    def chunk_rows(c):
        return pl.ds(c * tk, tk) if isinstance(c, int) else pl.ds(pl.multiple_of(c * tk, tk), tk)

    def load_queries(t, slot):
        for u in range(nstack):
            src = q_ref[tile_rows(t), (u // 2) * LANES:(u // 2 + 1) * LANES]
            keep = low_half if u % 2 == 0 else jnp.logical_not(low_half)
            q_sc[slot, u * tq:(u + 1) * tq, :] = jnp.where(keep, src, jnp.zeros_like(src))

    def scores(t, c, slot, par):
        s = lax.dot_general(q_sc[slot], k_ref[chunk_rows(c), :], (((1,), (1,)), ((), ())),
                            preferred_element_type=F32)
        if alibi:
            base = (t * tq - c * tk).astype(F32) if not (isinstance(t, int) and isinstance(c, int)) \
                else float(t * tq - c * tk)
            bias = neg_slope * jnp.abs(rc + base)
            s = s + jnp.concatenate([bias] * nstack, axis=0)
        s_bufs[par][...] = s

    def softmax(slot, par, first):
        s = s_bufs[par][...]
        m_cur = jnp.max(s, axis=-1, keepdims=True)
        if first:
            m_new = jnp.broadcast_to(m_cur, (nrows, LANES))
        else:
            m_old = m_sc[slot]
            m_new = jnp.maximum(m_old, m_cur)
            a_bufs[par][...] = jnp.exp2(m_old - m_new)
        p_bufs[par][...] = jnp.exp2(s - pltpu.repeat(m_new, tk // LANES, 1)).astype(BF16)
        m_sc[slot] = m_new

    def values(c, slot, par, first):
        d = jnp.dot(p_bufs[par][...], v_sc[chunk_rows(c), :], preferred_element_type=F32)
        if first:
            acc_sc[slot] = d
        else:
            acc_sc[slot] = pltpu.repeat(a_bufs[par][...], 2, 1) * acc_sc[slot] + d

    def finalize(t, slot):
        acc = acc_sc[slot]
        o = acc[:, 0:LANES] * (1.0 / acc[:, LANES:2 * LANES])
        if alibi:
            ob = o[0:tq, :] - lam * o[tq:2 * tq, :]
            ms = jnp.mean(ob * ob, axis=-1, keepdims=True)
            ob = ob * lax.rsqrt(ms + EPS) * subg_ref[...] * (1.0 - lam_init)
            o_ref[tile_rows(t), :] = ob.astype(o_ref.dtype)
        else:
            for pair in range(nstack // 2):
                lo = o[(2 * pair) * tq:(2 * pair + 1) * tq, :]
                hi = o[(2 * pair + 1) * tq:(2 * pair + 2) * tq, :]
                o_ref[tile_rows(t), pair * LANES:(pair + 1) * LANES] = (
                    jnp.where(low_half, lo, hi).astype(o_ref.dtype))

    def step(t, t_next, slot, c):
        static = isinstance(c, int)
        par = c % 2 if static else None
        ahead2 = c + 2
        if static and ahead2 >= nchunks:
            scores(t_next, ahead2 - nchunks, 1 - slot, par)
        else:
            scores(t, ahead2, slot, par)
        if static and c + 1 >= nchunks:
            softmax(1 - slot, 1 - par, first=True)
        else:
            softmax(slot, 1 - par, first=False)
        values(c, slot, par, first=static and c == 0)

    load_queries(0, 0)
    scores(0, 0, 0, 0)
    scores(0, 1, 0, 1)
    softmax(0, 0, first=True)

    def one_tile(t, slot):
        t_next = (t + 1) % ntiles
        load_queries(t_next, 1 - slot)
        lead = min(2, nchunks - 2)
        for c in range(lead):
            step(t, t_next, slot, c)

        def pair_body(j, inner):
            for par in range(2):
                c = 2 * j + par
                scores(t, c + 2, slot, par)
                softmax(slot, 1 - par, first=False)
                values(c, slot, par, first=False)
            return inner

        lax.fori_loop(lead // 2, (nchunks - 2) // 2, pair_body, 0)
        for c in range(nchunks - 2, nchunks):
            step(t, t_next, slot, c)
        finalize(t, slot)

    def tile_pair_body(u, carry):
        one_tile(2 * u, 0)
        one_tile(2 * u + 1, 1)
        return carry

    lax.fori_loop(0, ntiles // 2, tile_pair_body, 0)


def _flash_attention(q, k, v, slopes, lam_pack, subg, *, nbatch, seq, ngroups, nstack, alibi, lam_init,
                     nrows=1024):
    qw = (nstack // 2) * LANES
    tq = nrows // nstack
    tk = min(1024, seq // 2)
    kernel = functools.partial(_flash_kernel, tq=tq, tk=tk, seq=seq, nstack=nstack, alibi=alibi,
                               lam_init=lam_init)
    grid_spec = pltpu.PrefetchScalarGridSpec(
        num_scalar_prefetch=1,
        grid=(nbatch, ngroups),
        in_specs=[pl.BlockSpec((None, seq, qw), lambda b, g, s: (b, 0, g)),
                  pl.BlockSpec((None, seq, LANES), lambda b, g, s: (b, 0, g)),
                  pl.BlockSpec((None, seq, LANES), lambda b, g, s: (b, 0, g)),
                  pl.BlockSpec(lam_pack.shape, lambda b, g, s: (0, 0)),
                  pl.BlockSpec(subg.shape, lambda b, g, s: (0, 0))],
        out_specs=pl.BlockSpec((None, seq, qw), lambda b, g, s: (b, 0, g)),
        scratch_shapes=[pltpu.VMEM((2, nrows, LANES), BF16),
                        pltpu.VMEM((seq, 2 * LANES), BF16),
                        pltpu.VMEM((2, nrows, LANES), F32),
                        pltpu.VMEM((2, nrows, 2 * LANES), F32),
                        pltpu.VMEM((nrows, tk), F32), pltpu.VMEM((nrows, tk), F32),
                        pltpu.VMEM((nrows, tk), BF16), pltpu.VMEM((nrows, tk), BF16),
                        pltpu.VMEM((nrows, LANES), F32), pltpu.VMEM((nrows, LANES), F32)])
    return pl.pallas_call(
        kernel,
        grid_spec=grid_spec,
        out_shape=jax.ShapeDtypeStruct((nbatch, seq, ngroups * qw), BF16),
        compiler_params=_cparams(("arbitrary", "arbitrary")),
        name="flash_alibi" if alibi else "flash_gqa",
    )(slopes, q, k, v, lam_pack, subg)


def _natten_kernel(q_ref, k_ref, v_ref, bias_ref, o_ref, v_sc, s0_sc, s1_sc, *, seq):
    lane = lax.broadcasted_iota(jnp.int32, (1, LANES), 1)
    low_half = lane < HEAD_DIM
    ntiles = seq // NAT_TQ
    rows = seq // GRID_W
    s_bufs = (s0_sc, s1_sc)

    v_sc[:, 0:LANES] = v_ref[...]
    v_sc[:, LANES:2 * LANES] = jnp.ones((seq, LANES), BF16)

    def window(t):
        w0 = jnp.clip(t * NAT_QROWS - WIN_H // 2, 0, rows - NAT_KROWS)
        return pl.multiple_of(w0 * GRID_W, NAT_TQ)

    def scores(t, par):
        q = q_ref[pl.ds(pl.multiple_of(t * NAT_TQ, NAT_TQ), NAT_TQ), :]
        zero = jnp.zeros_like(q)
        q2 = jnp.concatenate([jnp.where(low_half, q, zero), jnp.where(low_half, zero, q)], axis=0)
        s = lax.dot_general(q2, k_ref[pl.ds(window(t), NAT_TK), :], (((1,), (1,)), ((), ())),
                            preferred_element_type=F32)
        cls = jnp.where(t == 0, 0, jnp.where(t == ntiles - 1, 2, 1))
        bias = bias_ref[cls].astype(F32).reshape(2 * NAT_TQ, NAT_TK)
        s_bufs[par][...] = s + bias

    def finish(t, par):
        s = s_bufs[par][...]
        m = jnp.max(s, axis=-1, keepdims=True)
        p = jnp.exp2(s - m).astype(BF16)
        acc = jnp.dot(p, v_sc[pl.ds(window(t), NAT_TK), :], preferred_element_type=F32)
        o = acc[:, 0:LANES] * (1.0 / acc[:, LANES:2 * LANES])
        out = jnp.where(low_half, o[0:NAT_TQ, :], o[NAT_TQ:, :])
        o_ref[pl.ds(pl.multiple_of(t * NAT_TQ, NAT_TQ), NAT_TQ), :] = out.astype(o_ref.dtype)

    scores(jnp.int32(0), 0)

    unroll = 4 if ntiles % 4 == 0 else 2

    def group_body(u, carry):
        for k in range(unroll):
            t = unroll * u + k
            scores((t + 1) % ntiles, (k + 1) % 2)
            finish(t, k % 2)
        return carry

    lax.fori_loop(0, ntiles // unroll, group_body, 0)


def _natten(q, k, v, bias_tab, *, nbatch, seq):
    npairs = C_HEADS // 2
    blk = pl.BlockSpec((None, seq, LANES), lambda p, b: (b, 0, p))
    return pl.pallas_call(
        functools.partial(_natten_kernel, seq=seq),
        grid=(npairs, nbatch),
        in_specs=[blk, blk, blk,
                  pl.BlockSpec((3, None, 2, NAT_TQ, NAT_TK), lambda p, b: (0, p, 0, 0, 0))],
        out_specs=blk,
        out_shape=jax.ShapeDtypeStruct((nbatch, seq, C_HEADS * HEAD_DIM), BF16),
        scratch_shapes=[pltpu.VMEM((seq, 2 * LANES), BF16),
                        pltpu.VMEM((2 * NAT_TQ, NAT_TK), F32), pltpu.VMEM((2 * NAT_TQ, NAT_TK), F32)],
        compiler_params=_cparams(("arbitrary", "arbitrary")),
        name="natten",
    )(q, k, v, bias_tab)


def _natten_bias_table(rpb, seq):
    rows = seq // GRID_W
    ntiles = rows // NAT_QROWS
    col = jnp.arange(GRID_W)
    cstart = jnp.clip(col - WIN_W // 2, 0, GRID_W - WIN_W)
    col_valid = (col[None, :] >= cstart[:, None]) & (col[None, :] < cstart[:, None] + WIN_W)
    dc_idx = jnp.clip(col[None, :] - col[:, None] + WIN_W - 1, 0, 2 * WIN_W - 2)
    rpb_cols = rpb[:, :, dc_idx]
    tabs = []
    for tile in (0, 1, ntiles - 1):
        r = tile * NAT_QROWS + jnp.arange(NAT_QROWS)
        w0 = int(np.clip(tile * NAT_QROWS - WIN_H // 2, 0, rows - NAT_KROWS))
        kr = w0 + jnp.arange(NAT_KROWS)
        rstart = jnp.clip(r - WIN_H // 2, 0, rows - WIN_H)
        row_valid = (kr[None, :] >= rstart[:, None]) & (kr[None, :] < rstart[:, None] + WIN_H)
        dr_idx = jnp.clip(kr[None, :] - r[:, None] + WIN_H - 1, 0, 2 * WIN_H - 2)
        pick = (dr_idx[:, :, None] == jnp.arange(2 * WIN_H - 1)[None, None, :]).astype(F32)
        b = jnp.einsum("qkd,hdcx->hqckx", pick, rpb_cols * LOG2E,
                       precision=lax.Precision.HIGHEST)
        valid = row_valid[:, None, :, None] & col_valid[None, :, None, :]
        b = jnp.where(valid[None], b, -jnp.inf)
        tabs.append(b.reshape(C_HEADS, NAT_TQ, NAT_TK))
    return jnp.stack(tabs).reshape(3, C_HEADS // 2, 2, NAT_TQ, NAT_TK).astype(BF16)


def _ffn_kernel(x_ref, g1_ref, sh_ref, sc_ref, g2_ref, ng_ref, ma_ref, mb_ref, woa_ref, wob_ref,
                w1_ref, w3_ref, w2_ref, o_ref):
    y = (jnp.dot(ma_ref[...], woa_ref[...], preferred_element_type=F32)
         + jnp.dot(mb_ref[...], wob_ref[...], preferred_element_type=F32))
    x1 = x_ref[...] + g1_ref[...] * y
    h = _norm_mod(x1, ng_ref[...], sh_ref[...], sc_ref[...]).astype(BF16)
    y = None
    for f in range(D_FF // FFN_CHUNK):
        cols = slice(f * FFN_CHUNK, (f + 1) * FFN_CHUNK)
        a = jnp.dot(h, w1_ref[:, cols], preferred_element_type=F32)
        b = jnp.dot(h, w3_ref[:, cols], preferred_element_type=F32)
        d = jnp.dot((a * _sigmoid(a) * b).astype(BF16), w2_ref[cols, :], preferred_element_type=F32)
        y = d if y is None else y + d
    o_ref[...] = x1 + g2_ref[...] * y


def _ffn(x2, mod, norm_g, mixes, wos, w1, w3, w2, seq, tm=512):
    t = x2.shape[0]
    row = lambda i: (i, 0)
    resident = lambda a: pl.BlockSpec(a.shape, lambda i: (0, 0), pipeline_mode=pl.Buffered(1))
    return pl.pallas_call(
        _ffn_kernel,
        grid=(t // tm,),
        in_specs=[pl.BlockSpec((tm, D_MODEL), row),
                  _mod_spec(2, tm, seq), _mod_spec(3, tm, seq), _mod_spec(4, tm, seq), _mod_spec(5, tm, seq),
                  pl.BlockSpec((1, D_MODEL), lambda i: (0, 0)),
                  pl.BlockSpec((tm, mixes[0].shape[1]), row), pl.BlockSpec((tm, mixes[1].shape[1]), row),
                  resident(wos[0]), resident(wos[1]), resident(w1), resident(w3), resident(w2)],
        out_specs=pl.BlockSpec((tm, D_MODEL), row),
        out_shape=jax.ShapeDtypeStruct((t, D_MODEL), F32),
        compiler_params=_cparams(("arbitrary",)),
        name="outproj_ffn_swiglu",
    )(x2, mod, mod, mod, mod, norm_g.reshape(1, D_MODEL), mixes[0], mixes[1], wos[0], wos[1], w1, w3, w2)


def _router_kernel(x_ref, g1_ref, sh_ref, sc_ref, g_ref, mix_ref, wo_ref, rw_ref, rb_ref,
                   x1_ref, h_ref, meta_ref):
    lane = lax.broadcasted_iota(jnp.int32, (1, LANES), 1).astype(F32)
    nsub = 4
    rb = x_ref.shape[0] // nsub
    for r in range(nsub):
        rows = slice(r * rb, (r + 1) * rb)
        x1 = x_ref[rows, :] + g1_ref[...] * jnp.dot(mix_ref[rows, :], wo_ref[...], preferred_element_type=F32)
        x1_ref[rows, :] = x1
        h = _norm_mod(x1, g_ref[...], sh_ref[...], sc_ref[...])
        packed = _pack_halves(h[:, 0:PACK_W], h[:, PACK_W:D_MODEL])
        for s in range(SC_PIECES):
            h_ref[s, rows, :] = packed[:, s * SC_ROW_WORDS:(s + 1) * SC_ROW_WORDS]
        logits = jnp.dot(h, rw_ref[...], preferred_element_type=F32,
                         precision=lax.Precision.HIGHEST) + rb_ref[...]
        logits = jnp.where(lane < N_EXPERTS, logits, -jnp.inf)
        m1 = jnp.max(logits, axis=-1, keepdims=True)
        i1 = jnp.min(jnp.where(logits == m1, lane, float(LANES)), axis=-1, keepdims=True)
        rest = jnp.where(lane == i1, -jnp.inf, logits)
        m2 = jnp.max(rest, axis=-1, keepdims=True)
        i2 = jnp.min(jnp.where(rest == m2, lane, float(LANES)), axis=-1, keepdims=True)
        e = jnp.exp(m2 - m1)
        g1 = 1.0 / (1.0 + e)
        g2 = e * g1
        meta_ref[rows, :] = jnp.where(lane == 0, i1, jnp.where(lane == 1, i2, jnp.where(
            lane == 2, g1, jnp.where(lane == 3, g2, 0.0))))


def _router(x2, mod, norm_g, mix, wo, rw_pad, rb_pad, seq, tm=512):
    t = x2.shape[0]
    row = lambda i: (i, 0)
    const = lambda i: (0, 0)
    return pl.pallas_call(
        _router_kernel,
        grid=(t // tm,),
        in_specs=[pl.BlockSpec((tm, D_MODEL), row),
                  _mod_spec(2, tm, seq), _mod_spec(3, tm, seq), _mod_spec(4, tm, seq),
                  pl.BlockSpec((1, D_MODEL), const),
                  pl.BlockSpec((tm, D_MODEL), row), pl.BlockSpec((D_MODEL, D_MODEL), const),
                  pl.BlockSpec((D_MODEL, LANES), const), pl.BlockSpec((1, LANES), const)],
        out_specs=[pl.BlockSpec((tm, D_MODEL), row),
                   pl.BlockSpec((SC_PIECES, tm, SC_ROW_WORDS), lambda i: (0, i, 0)),
                   pl.BlockSpec((tm, LANES), row)],
        out_shape=[jax.ShapeDtypeStruct((t, D_MODEL), F32),
                   jax.ShapeDtypeStruct((SC_PIECES, t, SC_ROW_WORDS), jnp.int32),
                   jax.ShapeDtypeStruct((t, LANES), F32)],
        compiler_params=_cparams(("arbitrary",)),
        name="outproj_moe_router",
    )(x2, mod, mod, mod, norm_g.reshape(1, D_MODEL), mix, wo, rw_pad, rb_pad)


def _moe_kernel(te_ref, nt_ref, x_ref, w1_ref, w3_ref, w2a_ref, w2b_ref, o_ref, x_sc, g_sc, *, nf, tf):
    i = pl.program_id(0)
    j = pl.program_id(1)
    active = i < nt_ref[0]

    @pl.when(jnp.logical_and(active, j == 0))
    def _():
        x_sc[...] = _unpack_halves(_load_pieces(x_ref)).astype(BF16)

    @pl.when(jnp.logical_and(active, j < nf))
    def _():
        x = x_sc[...]
        a = jnp.dot(x, w1_ref[...], preferred_element_type=F32)
        b = jnp.dot(x, w3_ref[...], preferred_element_type=F32)
        g_sc[j] = (a * _sigmoid(a) * b).astype(BF16)

    @pl.when(jnp.logical_and(active, j >= nf))
    def _():
        ya, yb = None, None
        for f in range(nf):
            g = g_sc[f]
            da = jnp.dot(g, w2a_ref[f * tf:(f + 1) * tf, :], preferred_element_type=F32)
            db = jnp.dot(g, w2b_ref[f * tf:(f + 1) * tf, :], preferred_element_type=F32)
            ya = da if ya is None else ya + da
            yb = db if yb is None else yb + db
        o_ref[...] = _pack_halves(ya, yb)

    @pl.when(jnp.logical_and(jnp.logical_not(active), j >= nf))
    def _():
        o_ref[...] = jnp.zeros(o_ref.shape, o_ref.dtype)


def _moe_grouped(xs, tile_expert, num_tiles, w1, w3, w2, tg, tf=512):
    p = xs.shape[1]
    nf = w1.shape[2] // tf
    tn = SC_ROW_WORDS
    nb = SC_PIECES
    fcl = lambda j: jnp.minimum(j, nf - 1)
    ncl = lambda j: jnp.maximum(j - nf, 0)

    def w2_map(col0, switch):
        def index(i, j, te, nt):
            early = j < switch
            expert = jnp.where(early, te[jnp.maximum(i - 1, 0)], te[i])
            return (expert, 0, jnp.where(early, col0 + nb - 1, col0 + ncl(j)))
        return index

    grid_spec = pltpu.PrefetchScalarGridSpec(
        num_scalar_prefetch=2,
        grid=(p // tg, nf + nb),
        in_specs=[pl.BlockSpec((SC_PIECES, tg, SC_ROW_WORDS), lambda i, j, te, nt: (0, i, 0)),
                  pl.BlockSpec((None, D_MODEL, tf), lambda i, j, te, nt: (te[i], 0, fcl(j))),
                  pl.BlockSpec((None, D_MODEL, tf), lambda i, j, te, nt: (te[i], 0, fcl(j))),
                  pl.BlockSpec((None, nf * tf, tn), w2_map(0, nf // 2)),
                  pl.BlockSpec((None, nf * tf, tn), w2_map(nb, nf // 2 + 2))],
        out_specs=pl.BlockSpec((None, tg, tn), lambda i, j, te, nt: (ncl(j), i, 0)),
        scratch_shapes=[pltpu.VMEM((tg, D_MODEL), BF16), pltpu.VMEM((nf, tg, tf), BF16)])
    return pl.pallas_call(
        functools.partial(_moe_kernel, nf=nf, tf=tf),
        grid_spec=grid_spec,
        out_shape=jax.ShapeDtypeStruct((SC_PIECES, p, SC_ROW_WORDS), jnp.int32),
        compiler_params=_cparams(("arbitrary", "arbitrary")),
        name="moe_grouped",
    )(tile_expert, num_tiles, xs, w1, w3, w2, w2)


def _combine_kernel(x_ref, gate_ref, meta_ref, y1_ref, y2_ref, o_ref):
    meta = meta_ref[...]
    moe = (meta[:, 2:3] * _unpack_halves(_load_pieces(y1_ref))
           + meta[:, 3:4] * _unpack_halves(_load_pieces(y2_ref)))
    o_ref[...] = x_ref[...] + gate_ref[...] * moe


def _combine(x2, mod, meta, yg, seq, tm=512):
    t = x2.shape[0]
    nt = t // tm
    row = lambda i: (i, 0)
    return pl.pallas_call(
        _combine_kernel,
        grid=(nt,),
        in_specs=[pl.BlockSpec((tm, D_MODEL), row), _mod_spec(5, tm, seq),
                  pl.BlockSpec((tm, LANES), row),
                  pl.BlockSpec((SC_PIECES, tm, SC_ROW_WORDS), lambda i: (0, i, 0)),
                  pl.BlockSpec((SC_PIECES, tm, SC_ROW_WORDS), lambda i: (0, nt + i, 0))],
        out_specs=pl.BlockSpec((tm, D_MODEL), row),
        out_shape=jax.ShapeDtypeStruct((t, D_MODEL), F32),
        compiler_params=_cparams(("arbitrary",)),
        name="moe_combine",
    )(x2, mod, meta, yg, yg)


def _sc_gather(table, idx):
    n = idx.shape[0]
    nrows = table.shape[1]
    pieces = n * SC_PIECES
    idx_pieces = jnp.concatenate([idx + s * nrows for s in range(SC_PIECES)]).reshape(1, pieces)
    mesh = plsc.VectorSubcoreMesh(core_axis_name="core", subcore_axis_name="subcore")

    @pl.kernel(out_type=jax.ShapeDtypeStruct((pieces, SC_ROW_WORDS), table.dtype), mesh=mesh, scratch_types=[])
    def gather_kernel(table_hbm, idx_hbm, out_hbm):
        def body(idx_vmem, out_vmem):
            pltpu.sync_copy(table_hbm.at[idx_vmem.at[0]], out_vmem)

        pltpu.emit_pipeline(
            body,
            grid=(pieces // SC_WINDOW,),
            in_specs=[pl.BlockSpec((1, SC_WINDOW), lambda i: (0, i))],
            out_specs=[pl.BlockSpec((SC_WINDOW, SC_ROW_WORDS), lambda i: (i, 0))],
            core_axis_name=("core", "subcore"),
            dimension_semantics=(pltpu.PARALLEL,),
        )(idx_hbm, out_hbm)

    out = gather_kernel(table.reshape(SC_PIECES * nrows, SC_ROW_WORDS), idx_pieces)
    return out.reshape(SC_PIECES, n, SC_ROW_WORDS)


def _sc_invert(pos, nrows):
    n = pos.shape[0]
    src = jnp.broadcast_to(jnp.arange(n, dtype=jnp.int32)[:, None], (n, LANES))
    mesh = plsc.VectorSubcoreMesh(core_axis_name="core", subcore_axis_name="subcore")

    @pl.kernel(out_type=jax.ShapeDtypeStruct((nrows, LANES), jnp.int32), mesh=mesh, scratch_types=[])
    def scatter_kernel(src_hbm, idx_hbm, out_hbm):
        def body(src_vmem, idx_vmem):
            pltpu.sync_copy(src_vmem, out_hbm.at[idx_vmem.at[0]])

        pltpu.emit_pipeline(
            body,
            grid=(n // SC_WINDOW,),
            in_specs=[pl.BlockSpec((SC_WINDOW, LANES), lambda i: (i, 0)),
                      pl.BlockSpec((1, SC_WINDOW), lambda i: (0, i))],
            out_specs=[],
            core_axis_name=("core", "subcore"),
            dimension_semantics=(pltpu.PARALLEL,),
        )(src_hbm, idx_hbm)

    return scatter_kernel(src, pos.reshape(1, n))[:, 0]


def _moe(x2, mod, norm_g, mix, wo, rw_pad, rb_pad, w1, w3, w2, seq, tg=1024):
    t = x2.shape[0]
    x2, h, meta = _router(x2, mod, norm_g, mix, wo, rw_pad, rb_pad, seq)
    e_flat = meta[:, 0:2].astype(jnp.int32).reshape(-1)
    onehot = (e_flat[:, None] == jnp.arange(N_EXPERTS)[None, :]).astype(jnp.int32)
    csum = jnp.cumsum(onehot, axis=0)
    counts = csum[-1]
    rank = jnp.take_along_axis(csum, e_flat[:, None], axis=1)[:, 0] - 1
    padded = ((counts + tg - 1) // tg) * tg
    pend = jnp.cumsum(padded)
    pos = (pend - padded)[e_flat] + rank
    p_rows = 2 * t + N_EXPERTS * tg
    row_token = jnp.clip(_sc_invert(pos, p_rows) // 2, 0, t - 1)
    tile_start = jnp.arange(p_rows // tg, dtype=jnp.int32) * tg
    tile_expert = jnp.minimum(jnp.sum((tile_start[:, None] >= pend[None, :]).astype(jnp.int32), axis=1),
                              N_EXPERTS - 1)
    num_tiles = (pend[-1] // tg).astype(jnp.int32).reshape(1)
    xs = _sc_gather(h, row_token)
    ys = _moe_grouped(xs, tile_expert, num_tiles, w1, w3, w2, tg)
    yg = _sc_gather(ys, jnp.concatenate([pos[0::2], pos[1::2]]))
    return _combine(x2, mod, meta, yg, seq)


def _rope_tables(seq):
    t = np.arange(seq)
    lane = np.arange(LANES)
    d = lane % HEAD_DIM
    pos = np.where((d // 32)[None, :] == 0, (t // GRID_W)[:, None], (t % GRID_W)[:, None]).astype(np.float32)
    inv = (ROPE_THETA ** (-np.arange(16, dtype=np.float32) / 16)).astype(np.float32)
    ang = pos * inv[(d % 16)][None, :]
    return jnp.asarray(np.cos(ang), F32), jnp.asarray(np.sin(ang), F32)


def _pair_gain(g):
    return jnp.concatenate([g, g]).astype(F32)


def _prepare(p, seq):
    even, odd = {}, {}
    w_in = p["w_in_even"][0]
    qa, ka, va, qb, kb, vb = jnp.split(w_in, [512, 640, 768, 1280, 1792], axis=1)
    dup = lambda w: jnp.concatenate([w[:, 0:64], w[:, 0:64], w[:, 64:128], w[:, 64:128]], axis=1)
    even["w_in"] = jnp.concatenate([qa, qb, dup(ka), dup(va), kb, vb], axis=1).astype(BF16)
    gains = jnp.zeros((8, LANES), F32)
    gains = gains.at[0].set(_pair_gain(p["qnorm_a"][0])).at[1].set(_pair_gain(p["knorm_a"][0]))
    gains = gains.at[2].set(_pair_gain(p["qnorm_b"][0])).at[3].set(_pair_gain(p["knorm_b"][0]))
    even["gains"] = gains
    qscale = SCALE * LOG2E
    even["groups"] = ((512, 0, True, qscale), (512, 2, False, qscale), (256, 1, True, 1.0),
                      (256, None, False, 1.0), (512, 3, False, 1.0), (512, None, False, 1.0))
    lam = jnp.zeros((8, LANES), F32)
    for r, name in enumerate(("lam_q1", "lam_k1", "lam_q2", "lam_k2")):
        lam = lam.at[r, 0:HEAD_DIM].set(p[name][0])
    even["lam"] = lam
    even["subg"] = p["subln_b"][0].reshape(1, LANES).astype(F32)
    wo = p["w_out_even"][0].astype(BF16)
    even["wo"] = (wo[0:512], wo[512:1024])
    even["slopes"] = jnp.asarray(LOG2E * 2.0 ** (-8.0 * (np.arange(B_HEADS) + 1.0) / B_HEADS), F32)
    for name in ("ffn_w1", "ffn_w3", "ffn_w2"):
        even[name] = p[name][0].astype(BF16)
    for name in ("ada_w", "ada_b", "norm_mix", "norm_ffn"):
        even[name] = p[name + "_even"][0]
        odd[name] = p[name + "_odd"][0]

    odd["w_qkv"] = p["w_qkv_odd"][0].astype(BF16)
    gains = jnp.zeros((8, LANES), F32)
    odd["gains"] = gains.at[0].set(_pair_gain(p["qnorm_c"][0])).at[1].set(_pair_gain(p["knorm_c"][0]))
    odd["groups"] = ((1024, 0, False, qscale), (1024, 1, False, 1.0), (1024, None, False, 1.0))
    odd["bias_tab"] = _natten_bias_table(p["rpb_c"][0], seq)
    odd["wo"] = (p["w_out_odd"][0].astype(BF16),)
    odd["rw"] = jnp.zeros((D_MODEL, LANES), F32).at[:, 0:N_EXPERTS].set(p["router_w"][0])
    odd["rb"] = jnp.zeros((1, LANES), F32).at[0, 0:N_EXPERTS].set(p["router_b"][0])
    for name in ("moe_w1", "moe_w3", "moe_w2"):
        odd[name] = p[name][0].astype(BF16)
    cos_t, sn_t = _rope_tables(seq)
    gmat = jnp.asarray(np.kron(np.eye(4), np.full((HEAD_DIM, HEAD_DIM), 1.0 / HEAD_DIM)), BF16)
    return even, odd, (cos_t, sn_t, gmat)


def _even_layer(x2, c, ev, shared, nbatch, seq):
    cos_t, sn_t, gmat = shared
    mod = _ada_modulation(c, ev["ada_w"], ev["ada_b"])
    qa, qb, ka, va, kb, vb = _projection(x2, mod, ev["norm_mix"], ev["w_in"], cos_t, sn_t, ev["gains"], gmat,
                                         ev["groups"], seq)
    shp = lambda a: a.reshape(nbatch, seq, a.shape[1])
    lam_init = 0.8 - 0.6 * math.exp(-0.3 * 0)
    mix_a = _flash_attention(shp(qa), shp(ka), shp(va), ev["slopes"], ev["lam"], ev["subg"], nbatch=nbatch, seq=seq,
                             ngroups=A_KV_HEADS, nstack=4, alibi=False, lam_init=lam_init)
    mix_b = _flash_attention(shp(qb), shp(kb), shp(vb), ev["slopes"], ev["lam"], ev["subg"], nbatch=nbatch, seq=seq,
                             ngroups=B_HEADS, nstack=2, alibi=True, lam_init=lam_init)
    t = nbatch * seq
    return _ffn(x2, mod, ev["norm_ffn"], (mix_a.reshape(t, -1), mix_b.reshape(t, -1)), ev["wo"],
                ev["ffn_w1"], ev["ffn_w3"], ev["ffn_w2"], seq)


def _odd_layer(x2, c, od, shared, nbatch, seq):
    cos_t, sn_t, gmat = shared
    mod = _ada_modulation(c, od["ada_w"], od["ada_b"])
    q, k, v = _projection(x2, mod, od["norm_mix"], od["w_qkv"], cos_t, sn_t, od["gains"], gmat, od["groups"], seq)
    shp = lambda a: a.reshape(nbatch, seq, a.shape[1])
    mix = _natten(shp(q), shp(k), shp(v), od["bias_tab"], nbatch=nbatch, seq=seq)
    return _moe(x2, mod, od["norm_ffn"], mix.reshape(nbatch * seq, -1), od["wo"][0], od["rw"], od["rb"],
                od["moe_w1"], od["moe_w3"], od["moe_w2"], seq)


def _trunk(x, c, ev, od, shared):
    nbatch, seq, d = x.shape
    x2 = x.reshape(nbatch * seq, d)
    x2 = _even_layer(x2, c, ev, shared, nbatch, seq)
    x2 = _odd_layer(x2, c, od, shared, nbatch, seq)
    return x2.reshape(nbatch, seq, d)


def kernel(x_prompt, x_sample, c_prompt, c_sample, ada_w_even, ada_b_even, norm_mix_even, norm_ffn_even, w_in_even, qnorm_a, knorm_a, qnorm_b, knorm_b, lam_q1, lam_k1, lam_q2, lam_k2, subln_b, w_out_even, ffn_w1, ffn_w3, ffn_w2, ada_w_odd, ada_b_odd, norm_mix_odd, norm_ffn_odd, w_qkv_odd, qnorm_c, knorm_c, rpb_c, w_out_odd, router_w, router_b, moe_w1, moe_w3, moe_w2):
    params = dict(ada_w_even=ada_w_even, ada_b_even=ada_b_even, norm_mix_even=norm_mix_even,
                  norm_ffn_even=norm_ffn_even, w_in_even=w_in_even, qnorm_a=qnorm_a, knorm_a=knorm_a,
                  qnorm_b=qnorm_b, knorm_b=knorm_b, lam_q1=lam_q1, lam_k1=lam_k1, lam_q2=lam_q2, lam_k2=lam_k2,
                  subln_b=subln_b, w_out_even=w_out_even, ffn_w1=ffn_w1, ffn_w3=ffn_w3, ffn_w2=ffn_w2,
                  ada_w_odd=ada_w_odd, ada_b_odd=ada_b_odd, norm_mix_odd=norm_mix_odd, norm_ffn_odd=norm_ffn_odd,
                  w_qkv_odd=w_qkv_odd, qnorm_c=qnorm_c, knorm_c=knorm_c, rpb_c=rpb_c, w_out_odd=w_out_odd,
                  router_w=router_w, router_b=router_b, moe_w1=moe_w1, moe_w3=moe_w3, moe_w2=moe_w2)
    seq = x_prompt.shape[1]
    ev, od, shared = _prepare(params, seq)
    y_prompt = _trunk(x_prompt, c_prompt, ev, od, shared)
    y_sample = _trunk(x_sample, c_sample, ev, od, shared)
    return (y_prompt, y_sample)
```

```python
import functools
import math

import numpy as np
import jax
import jax.numpy as jnp
from jax import lax
from jax.experimental import pallas as pl
from jax.experimental.pallas import tpu as pltpu
from jax.experimental.pallas import tpu_sc as plsc

F32 = jnp.float32
BF16 = jnp.bfloat16

D_MODEL = 1024
HEAD_DIM = 64
LANES = 128
SCALE = HEAD_DIM ** -0.5
LOG2E = 1.4426950408889634
GRID_W = 64
EPS = 1e-6
ROPE_THETA = 10000.0
A_Q_HEADS = 8
A_KV_HEADS = 2
B_HEADS = 4
C_HEADS = 16
WIN_H = 8
WIN_W = 16
N_EXPERTS = 8
D_FF = 2816
D_FF_EXPERT = 3584
VMEM_LIMIT = 56 * 1024 * 1024

NAT_QROWS = 4
NAT_KROWS = 12
NAT_TQ = NAT_QROWS * GRID_W
NAT_TK = NAT_KROWS * GRID_W
PACK_W = D_MODEL // 2
SC_WINDOW = 128
SC_ROW_WORDS = 256
SC_PIECES = PACK_W // SC_ROW_WORDS
FFN_CHUNK = 1408


def _cparams(sem):
    return pltpu.CompilerParams(dimension_semantics=sem, vmem_limit_bytes=VMEM_LIMIT)


def _norm_mod(x, g, shift, scale):
    ms = jnp.mean(x * x, axis=-1, keepdims=True)
    y = x * lax.rsqrt(ms + EPS) * g
    return y * (1.0 + scale) + shift


def _head_norm(x, gain, gmat):
    ms = jnp.dot((x * x).astype(BF16), gmat, preferred_element_type=F32)
    return x * lax.rsqrt(ms + EPS) * gain


def _rope(x, cos, sn, first_quarter):
    up = pltpu.roll(x, x.shape[1] - 16, 1)
    down = pltpu.roll(x, 16, 1)
    return x * cos + sn * jnp.where(first_quarter, -up, down)


def _sigmoid(a):
    return 1.0 / (1.0 + jnp.exp(-a))


def _pack_halves(a, b):
    hi = lax.bitcast_convert_type(a.astype(BF16).astype(F32), jnp.int32)
    lo = lax.bitcast_convert_type(b.astype(BF16).astype(F32), jnp.int32)
    return hi | lax.shift_right_logical(lo, jnp.full_like(lo, 16))


def _unpack_halves(w):
    hi = lax.bitcast_convert_type(w & jnp.int32(-65536), F32)
    lo = lax.bitcast_convert_type(lax.shift_left(w, jnp.full_like(w, 16)), F32)
    return jnp.concatenate([hi, lo], axis=1)


def _load_pieces(ref):
    return jnp.concatenate([ref[s] for s in range(SC_PIECES)], axis=1)


def _ada_kernel(c_ref, w_ref, b_ref, o_ref):
    c = c_ref[...]
    s = c * _sigmoid(c)
    o_ref[...] = jnp.dot(s, w_ref[...], preferred_element_type=F32,
                         precision=lax.Precision.HIGHEST) + b_ref[...]


def _ada_modulation(c, w, b):
    nb, d = c.shape
    n = w.shape[1]
    tn = 512
    mod = pl.pallas_call(
        _ada_kernel,
        grid=(n // tn,),
        in_specs=[pl.BlockSpec((nb, d), lambda j: (0, 0)),
                  pl.BlockSpec((d, tn), lambda j: (0, j)),
                  pl.BlockSpec((1, tn), lambda j: (0, j))],
        out_specs=pl.BlockSpec((nb, tn), lambda j: (0, j)),
        out_shape=jax.ShapeDtypeStruct((nb, n), F32),
        compiler_params=_cparams(("arbitrary",)),
        name="ada_mod",
    )(c, w, b.reshape(1, n))
    return mod.reshape(nb, 6, 1, d)


def _mod_spec(k, tm, seq):
    return pl.BlockSpec((None, None, 1, D_MODEL), lambda i, *_: ((i * tm) // seq, k, 0, 0))


def _proj_kernel(x_ref, sh_ref, sc_ref, g_ref, w_ref, cos_ref, sn_ref, gains_ref, gmat_ref, *o_refs,
                 groups):
    wide = 2 * LANES
    lane = lax.broadcasted_iota(jnp.int32, (1, wide), 1)
    first_quarter = (lane % 32) < 16
    gmat = gmat_ref[...]
    h = _norm_mod(x_ref[...], g_ref[...], sh_ref[...], sc_ref[...])
    y = jnp.dot(h.astype(BF16), w_ref[...], preferred_element_type=F32)
    cos = jnp.concatenate([cos_ref[...]] * 2, axis=1)
    sn = jnp.concatenate([sn_ref[...]] * 2, axis=1)
    off = 0
    for o_ref, (width, gain_row, rope, mult) in zip(o_refs, groups):
        if gain_row is None:
            o_ref[...] = y[:, off:off + width].astype(o_ref.dtype)
        else:
            gain = jnp.concatenate([gains_ref[gain_row:gain_row + 1, :]] * 2, axis=1)
            for t in range(width // wide):
                z = _head_norm(y[:, off + t * wide: off + (t + 1) * wide], gain, gmat)
                if rope:
                    z = _rope(z, cos, sn, first_quarter)
                if mult != 1.0:
                    z = z * mult
                o_ref[:, t * wide:(t + 1) * wide] = z.astype(o_ref.dtype)
        off += width


def _projection(x2, mod, norm_g, w, cos_t, sn_t, gains, gmat, groups, seq, tm=512):
    t = x2.shape[0]
    n = w.shape[1]
    nseq = seq // tm
    row = lambda i: (i, 0)
    const = lambda i: (0, 0)
    tab = lambda i: (i % nseq, 0)
    return pl.pallas_call(
        functools.partial(_proj_kernel, groups=groups),
        grid=(t // tm,),
        in_specs=[pl.BlockSpec((tm, D_MODEL), row),
                  _mod_spec(0, tm, seq), _mod_spec(1, tm, seq),
                  pl.BlockSpec((1, D_MODEL), const),
                  pl.BlockSpec((D_MODEL, n), const),
                  pl.BlockSpec((tm, LANES), tab), pl.BlockSpec((tm, LANES), tab),
                  pl.BlockSpec(gains.shape, const),
                  pl.BlockSpec(gmat.shape, const)],
        out_specs=[pl.BlockSpec((tm, g[0]), row) for g in groups],
        out_shape=[jax.ShapeDtypeStruct((t, g[0]), BF16) for g in groups],
        compiler_params=_cparams(("arbitrary",)),
        name="norm_mod_proj",
    )(x2, mod, mod, norm_g.reshape(1, D_MODEL), w, cos_t, sn_t, gains, gmat)


def _flash_kernel(slope_ref, q_ref, k_ref, v_ref, lam_ref, subg_ref, o_ref,
                  q_sc, v_sc, m_sc, acc_sc, s0_sc, s1_sc, p0_sc, p1_sc, a0_sc, a1_sc,
                  *, tq, tk, seq, nstack, alibi, lam_init):
    g = pl.program_id(1)
    lane = lax.broadcasted_iota(jnp.int32, (1, LANES), 1)
    low_half = lane < HEAD_DIM
    s_bufs, p_bufs, a_bufs = (s0_sc, s1_sc), (p0_sc, p1_sc), (a0_sc, a1_sc)
    nchunks = seq // tk
    ntiles = seq // tq
    nrows = nstack * tq

    v_sc[:, 0:LANES] = v_ref[...]
    v_sc[:, LANES:2 * LANES] = jnp.ones((seq, LANES), BF16)

    if alibi:
        rc = (lax.broadcasted_iota(jnp.int32, (tq, tk), 0)
              - lax.broadcasted_iota(jnp.int32, (tq, tk), 1)).astype(F32)
        neg_slope = -slope_ref[g]
        lp = lam_ref[...]
        l1 = jnp.sum(lp[0:1, :] * lp[1:2, :], axis=-1, keepdims=True)
        l2 = jnp.sum(lp[2:3, :] * lp[3:4, :], axis=-1, keepdims=True)
        lam = jnp.exp(l1) - jnp.exp(l2) + lam_init

    def tile_rows(t):
        return pl.ds(t * tq, tq) if isinstance(t, int) else pl.ds(pl.multiple_of(t * tq, tq), tq)

    def chunk_rows(c):
        return pl.ds(c * tk, tk) if isinstance(c, int) else pl.ds(pl.multiple_of(c * tk, tk), tk)

    def load_queries(t, slot):
        for u in range(nstack):
            src = q_ref[tile_rows(t), (u // 2) * LANES:(u // 2 + 1) * LANES]
            keep = low_half if u % 2 == 0 else jnp.logical_not(low_half)
            q_sc[slot, u * tq:(u + 1) * tq, :] = jnp.where(keep, src, jnp.zeros_like(src))

    def scores(t, c, slot, par):
        s = lax.dot_general(q_sc[slot], k_ref[chunk_rows(c), :], (((1,), (1,)), ((), ())),
                            preferred_element_type=F32)
        if alibi:
            base = (t * tq - c * tk).astype(F32) if not (isinstance(t, int) and isinstance(c, int)) \
                else float(t * tq - c * tk)
            bias = neg_slope * jnp.abs(rc + base)
            s = s + jnp.concatenate([bias] * nstack, axis=0)
        s_bufs[par][...] = s

    def softmax(slot, par, first):
        s = s_bufs[par][...]
        m_cur = jnp.max(s, axis=-1, keepdims=True)
        if first:
            m_new = jnp.broadcast_to(m_cur, (nrows, LANES))
        else:
            m_old = m_sc[slot]
            m_new = jnp.maximum(m_old, m_cur)
            a_bufs[par][...] = jnp.exp2(m_old - m_new)
        p_bufs[par][...] = jnp.exp2(s - pltpu.repeat(m_new, tk // LANES, 1)).astype(BF16)
        m_sc[slot] = m_new

    def values(c, slot, par, first):
        d = jnp.dot(p_bufs[par][...], v_sc[chunk_rows(c), :], preferred_element_type=F32)
        if first:
            acc_sc[slot] = d
        else:
            acc_sc[slot] = pltpu.repeat(a_bufs[par][...], 2, 1) * acc_sc[slot] + d

    def finalize(t, slot):
        acc = acc_sc[slot]
        o = acc[:, 0:LANES] * (1.0 / acc[:, LANES:2 * LANES])
        if alibi:
            ob = o[0:tq, :] - lam * o[tq:2 * tq, :]
            ms = jnp.mean(ob * ob, axis=-1, keepdims=True)
            ob = ob * lax.rsqrt(ms + EPS) * subg_ref[...] * (1.0 - lam_init)
            o_ref[tile_rows(t), :] = ob.astype(o_ref.dtype)
        else:
            for pair in range(nstack // 2):
                lo = o[(2 * pair) * tq:(2 * pair + 1) * tq, :]
                hi = o[(2 * pair + 1) * tq:(2 * pair + 2) * tq, :]
                o_ref[tile_rows(t), pair * LANES:(pair + 1) * LANES] = (
                    jnp.where(low_half, lo, hi).astype(o_ref.dtype))

    def step(t, t_next, slot, c):
        static = isinstance(c, int)
        par = c % 2 if static else None
        ahead2 = c + 2
        if static and ahead2 >= nchunks:
            scores(t_next, ahead2 - nchunks, 1 - slot, par)
        else:
            scores(t, ahead2, slot, par)
        if static and c + 1 >= nchunks:
            softmax(1 - slot, 1 - par, first=True)
        else:
            softmax(slot, 1 - par, first=False)
        values(c, slot, par, first=static and c == 0)

    load_queries(0, 0)
    scores(0, 0, 0, 0)
    scores(0, 1, 0, 1)
    softmax(0, 0, first=True)

    def one_tile(t, slot):
        t_next = (t + 1) % ntiles
        load_queries(t_next, 1 - slot)
        lead = min(2, nchunks - 2)
        for c in range(lead):
            step(t, t_next, slot, c)

        def pair_body(j, inner):
            for par in range(2):
                c = 2 * j + par
                scores(t, c + 2, slot, par)
                softmax(slot, 1 - par, first=False)
                values(c, slot, par, first=False)
            return inner

        lax.fori_loop(lead // 2, (nchunks - 2) // 2, pair_body, 0)
        for c in range(nchunks - 2, nchunks):
            step(t, t_next, slot, c)
        finalize(t, slot)

    def tile_pair_body(u, carry):
        one_tile(2 * u, 0)
        one_tile(2 * u + 1, 1)
        return carry

    lax.fori_loop(0, ntiles // 2, tile_pair_body, 0)


def _flash_attention(q, k, v, slopes, lam_pack, subg, *, nbatch, seq, ngroups, nstack, alibi, lam_init,
                     nrows=1024):
    qw = (nstack // 2) * LANES
    tq = nrows // nstack
    tk = min(1024, seq // 2)
    kernel = functools.partial(_flash_kernel, tq=tq, tk=tk, seq=seq, nstack=nstack, alibi=alibi,
                               lam_init=lam_init)
    grid_spec = pltpu.PrefetchScalarGridSpec(
        num_scalar_prefetch=1,
        grid=(nbatch, ngroups),
        in_specs=[pl.BlockSpec((None, seq, qw), lambda b, g, s: (b, 0, g)),
                  pl.BlockSpec((None, seq, LANES), lambda b, g, s: (b, 0, g)),
                  pl.BlockSpec((None, seq, LANES), lambda b, g, s: (b, 0, g)),
                  pl.BlockSpec(lam_pack.shape, lambda b, g, s: (0, 0)),
                  pl.BlockSpec(subg.shape, lambda b, g, s: (0, 0))],
        out_specs=pl.BlockSpec((None, seq, qw), lambda b, g, s: (b, 0, g)),
        scratch_shapes=[pltpu.VMEM((2, nrows, LANES), BF16),
                        pltpu.VMEM((seq, 2 * LANES), BF16),
                        pltpu.VMEM((2, nrows, LANES), F32),
                        pltpu.VMEM((2, nrows, 2 * LANES), F32),
                        pltpu.VMEM((nrows, tk), F32), pltpu.VMEM((nrows, tk), F32),
                        pltpu.VMEM((nrows, tk), BF16), pltpu.VMEM((nrows, tk), BF16),
                        pltpu.VMEM((nrows, LANES), F32), pltpu.VMEM((nrows, LANES), F32)])
    return pl.pallas_call(
        kernel,
        grid_spec=grid_spec,
        out_shape=jax.ShapeDtypeStruct((nbatch, seq, ngroups * qw), BF16),
        compiler_params=_cparams(("arbitrary", "arbitrary")),
        name="flash_alibi" if alibi else "flash_gqa",
    )(slopes, q, k, v, lam_pack, subg)


def _natten_kernel(q_ref, k_ref, v_ref, bias_ref, o_ref, v_sc, s0_sc, s1_sc, *, seq):
    lane = lax.broadcasted_iota(jnp.int32, (1, LANES), 1)
    low_half = lane < HEAD_DIM
    ntiles = seq // NAT_TQ
    rows = seq // GRID_W
    s_bufs = (s0_sc, s1_sc)

    v_sc[:, 0:LANES] = v_ref[...]
    v_sc[:, LANES:2 * LANES] = jnp.ones((seq, LANES), BF16)

    def window(t):
        w0 = jnp.clip(t * NAT_QROWS - WIN_H // 2, 0, rows - NAT_KROWS)
        return pl.multiple_of(w0 * GRID_W, NAT_TQ)

    def scores(t, par):
        q = q_ref[pl.ds(pl.multiple_of(t * NAT_TQ, NAT_TQ), NAT_TQ), :]
        zero = jnp.zeros_like(q)
        q2 = jnp.concatenate([jnp.where(low_half, q, zero), jnp.where(low_half, zero, q)], axis=0)
        s = lax.dot_general(q2, k_ref[pl.ds(window(t), NAT_TK), :], (((1,), (1,)), ((), ())),
                            preferred_element_type=F32)
        cls = jnp.where(t == 0, 0, jnp.where(t == ntiles - 1, 2, 1))
        bias = bias_ref[cls].astype(F32).reshape(2 * NAT_TQ, NAT_TK)
        s_bufs[par][...] = s + bias

    def finish(t, par):
        s = s_bufs[par][...]
        m = jnp.max(s, axis=-1, keepdims=True)
        p = jnp.exp2(s - m).astype(BF16)
        acc = jnp.dot(p, v_sc[pl.ds(window(t), NAT_TK), :], preferred_element_type=F32)
        o = acc[:, 0:LANES] * (1.0 / acc[:, LANES:2 * LANES])
        out = jnp.where(low_half, o[0:NAT_TQ, :], o[NAT_TQ:, :])
        o_ref[pl.ds(pl.multiple_of(t * NAT_TQ, NAT_TQ), NAT_TQ), :] = out.astype(o_ref.dtype)

    scores(jnp.int32(0), 0)

    unroll = next(u for u in (8, 4, 2) if ntiles % u == 0)

    def group_body(u, carry):
        for k in range(unroll):
            t = unroll * u + k
            scores((t + 1) % ntiles, (k + 1) % 2)
            finish(t, k % 2)
        return carry

    lax.fori_loop(0, ntiles // unroll, group_body, 0)


def _natten(q, k, v, bias_tab, *, nbatch, seq):
    npairs = C_HEADS // 2
    blk = pl.BlockSpec((None, seq, LANES), lambda p, b: (b, 0, p))
    return pl.pallas_call(
        functools.partial(_natten_kernel, seq=seq),
        grid=(npairs, nbatch),
        in_specs=[blk, blk, blk,
                  pl.BlockSpec((3, None, 2, NAT_TQ, NAT_TK), lambda p, b: (0, p, 0, 0, 0))],
        out_specs=blk,
        out_shape=jax.ShapeDtypeStruct((nbatch, seq, C_HEADS * HEAD_DIM), BF16),
        scratch_shapes=[pltpu.VMEM((seq, 2 * LANES), BF16),
                        pltpu.VMEM((2 * NAT_TQ, NAT_TK), F32), pltpu.VMEM((2 * NAT_TQ, NAT_TK), F32)],
        compiler_params=_cparams(("arbitrary", "arbitrary")),
        name="natten",
    )(q, k, v, bias_tab)


def _natten_bias_table(rpb, seq):
    rows = seq // GRID_W
    ntiles = rows // NAT_QROWS
    col = jnp.arange(GRID_W)
    cstart = jnp.clip(col - WIN_W // 2, 0, GRID_W - WIN_W)
    col_valid = (col[None, :] >= cstart[:, None]) & (col[None, :] < cstart[:, None] + WIN_W)
    dc_idx = jnp.clip(col[None, :] - col[:, None] + WIN_W - 1, 0, 2 * WIN_W - 2)
    rpb_cols = rpb[:, :, dc_idx]
    tabs = []
    for tile in (0, 1, ntiles - 1):
        r = tile * NAT_QROWS + jnp.arange(NAT_QROWS)
        w0 = int(np.clip(tile * NAT_QROWS - WIN_H // 2, 0, rows - NAT_KROWS))
        kr = w0 + jnp.arange(NAT_KROWS)
        rstart = jnp.clip(r - WIN_H // 2, 0, rows - WIN_H)
        row_valid = (kr[None, :] >= rstart[:, None]) & (kr[None, :] < rstart[:, None] + WIN_H)
        dr_idx = jnp.clip(kr[None, :] - r[:, None] + WIN_H - 1, 0, 2 * WIN_H - 2)
        pick = (dr_idx[:, :, None] == jnp.arange(2 * WIN_H - 1)[None, None, :]).astype(F32)
        b = jnp.einsum("qkd,hdcx->hqckx", pick, rpb_cols * LOG2E,
                       precision=lax.Precision.HIGHEST)
        valid = row_valid[:, None, :, None] & col_valid[None, :, None, :]
        b = jnp.where(valid[None], b, -jnp.inf)
        tabs.append(b.reshape(C_HEADS, NAT_TQ, NAT_TK))
    return jnp.stack(tabs).reshape(3, C_HEADS // 2, 2, NAT_TQ, NAT_TK).astype(BF16)


def _ffn_kernel(x_ref, g1_ref, sh_ref, sc_ref, g2_ref, ng_ref, ma_ref, mb_ref, woa_ref, wob_ref,
                w1_ref, w3_ref, w2_ref, o_ref):
    y = (jnp.dot(ma_ref[...], woa_ref[...], preferred_element_type=F32)
         + jnp.dot(mb_ref[...], wob_ref[...], preferred_element_type=F32))
    x1 = x_ref[...] + g1_ref[...] * y
    h = _norm_mod(x1, ng_ref[...], sh_ref[...], sc_ref[...]).astype(BF16)
    y = None
    for f in range(D_FF // FFN_CHUNK):
        cols = slice(f * FFN_CHUNK, (f + 1) * FFN_CHUNK)
        a = jnp.dot(h, w1_ref[:, cols], preferred_element_type=F32)
        b = jnp.dot(h, w3_ref[:, cols], preferred_element_type=F32)
        d = jnp.dot((a * _sigmoid(a) * b).astype(BF16), w2_ref[cols, :], preferred_element_type=F32)
        y = d if y is None else y + d
    o_ref[...] = x1 + g2_ref[...] * y


def _ffn(x2, mod, norm_g, mixes, wos, w1, w3, w2, seq, tm=512):
    t = x2.shape[0]
    row = lambda i: (i, 0)
    resident = lambda a: pl.BlockSpec(a.shape, lambda i: (0, 0), pipeline_mode=pl.Buffered(1))
    return pl.pallas_call(
        _ffn_kernel,
        grid=(t // tm,),
        in_specs=[pl.BlockSpec((tm, D_MODEL), row),
                  _mod_spec(2, tm, seq), _mod_spec(3, tm, seq), _mod_spec(4, tm, seq), _mod_spec(5, tm, seq),
                  pl.BlockSpec((1, D_MODEL), lambda i: (0, 0)),
                  pl.BlockSpec((tm, mixes[0].shape[1]), row), pl.BlockSpec((tm, mixes[1].shape[1]), row),
                  resident(wos[0]), resident(wos[1]), resident(w1), resident(w3), resident(w2)],
        out_specs=pl.BlockSpec((tm, D_MODEL), row),
        out_shape=jax.ShapeDtypeStruct((t, D_MODEL), F32),
        compiler_params=_cparams(("arbitrary",)),
        name="outproj_ffn_swiglu",
    )(x2, mod, mod, mod, mod, norm_g.reshape(1, D_MODEL), mixes[0], mixes[1], wos[0], wos[1], w1, w3, w2)


def _router_kernel(x_ref, g1_ref, sh_ref, sc_ref, g_ref, mix_ref, wo_ref, rw_ref, rb_ref,
                   x1_ref, h_ref, meta_ref):
    lane = lax.broadcasted_iota(jnp.int32, (1, LANES), 1).astype(F32)
    nsub = 4
    rb = x_ref.shape[0] // nsub
    for r in range(nsub):
        rows = slice(r * rb, (r + 1) * rb)
        x1 = x_ref[rows, :] + g1_ref[...] * jnp.dot(mix_ref[rows, :], wo_ref[...], preferred_element_type=F32)
        x1_ref[rows, :] = x1
        h = _norm_mod(x1, g_ref[...], sh_ref[...], sc_ref[...])
        packed = _pack_halves(h[:, 0:PACK_W], h[:, PACK_W:D_MODEL])
        for s in range(SC_PIECES):
            h_ref[s, rows, :] = packed[:, s * SC_ROW_WORDS:(s + 1) * SC_ROW_WORDS]
        logits = jnp.dot(h, rw_ref[...], preferred_element_type=F32,
                         precision=lax.Precision.HIGHEST) + rb_ref[...]
        logits = jnp.where(lane < N_EXPERTS, logits, -jnp.inf)
        m1 = jnp.max(logits, axis=-1, keepdims=True)
        i1 = jnp.min(jnp.where(logits == m1, lane, float(LANES)), axis=-1, keepdims=True)
        rest = jnp.where(lane == i1, -jnp.inf, logits)
        m2 = jnp.max(rest, axis=-1, keepdims=True)
        i2 = jnp.min(jnp.where(rest == m2, lane, float(LANES)), axis=-1, keepdims=True)
        e = jnp.exp(m2 - m1)
        g1 = 1.0 / (1.0 + e)
        g2 = e * g1
        meta_ref[rows, :] = jnp.where(lane == 0, i1, jnp.where(lane == 1, i2, jnp.where(
            lane == 2, g1, jnp.where(lane == 3, g2, 0.0))))


def _router(x2, mod, norm_g, mix, wo, rw_pad, rb_pad, seq, tm=512):
    t = x2.shape[0]
    row = lambda i: (i, 0)
    const = lambda i: (0, 0)
    return pl.pallas_call(
        _router_kernel,
        grid=(t // tm,),
        in_specs=[pl.BlockSpec((tm, D_MODEL), row),
                  _mod_spec(2, tm, seq), _mod_spec(3, tm, seq), _mod_spec(4, tm, seq),
                  pl.BlockSpec((1, D_MODEL), const),
                  pl.BlockSpec((tm, D_MODEL), row), pl.BlockSpec((D_MODEL, D_MODEL), const),
                  pl.BlockSpec((D_MODEL, LANES), const), pl.BlockSpec((1, LANES), const)],
        out_specs=[pl.BlockSpec((tm, D_MODEL), row),
                   pl.BlockSpec((SC_PIECES, tm, SC_ROW_WORDS), lambda i: (0, i, 0)),
                   pl.BlockSpec((tm, LANES), row)],
        out_shape=[jax.ShapeDtypeStruct((t, D_MODEL), F32),
                   jax.ShapeDtypeStruct((SC_PIECES, t, SC_ROW_WORDS), jnp.int32),
                   jax.ShapeDtypeStruct((t, LANES), F32)],
        compiler_params=_cparams(("arbitrary",)),
        name="outproj_moe_router",
    )(x2, mod, mod, mod, norm_g.reshape(1, D_MODEL), mix, wo, rw_pad, rb_pad)


def _moe_kernel(te_ref, nt_ref, x_ref, w1_ref, w3_ref, w2a_ref, w2b_ref, o_ref, x_sc, g_sc, *, nf, tf):
    i = pl.program_id(0)
    j = pl.program_id(1)
    active = i < nt_ref[0]

    @pl.when(jnp.logical_and(active, j == 0))
    def _():
        x_sc[...] = _unpack_halves(_load_pieces(x_ref)).astype(BF16)

    @pl.when(jnp.logical_and(active, j < nf))
    def _():
        x = x_sc[...]
        a = jnp.dot(x, w1_ref[...], preferred_element_type=F32)
        b = jnp.dot(x, w3_ref[...], preferred_element_type=F32)
        g_sc[j] = (a * _sigmoid(a) * b).astype(BF16)

    @pl.when(jnp.logical_and(active, j >= nf))
    def _():
        ya, yb = None, None
        for f in range(nf):
            g = g_sc[f]
            da = jnp.dot(g, w2a_ref[f * tf:(f + 1) * tf, :], preferred_element_type=F32)
            db = jnp.dot(g, w2b_ref[f * tf:(f + 1) * tf, :], preferred_element_type=F32)
            ya = da if ya is None else ya + da
            yb = db if yb is None else yb + db
        o_ref[...] = _pack_halves(ya, yb)

    @pl.when(jnp.logical_and(jnp.logical_not(active), j >= nf))
    def _():
        o_ref[...] = jnp.zeros(o_ref.shape, o_ref.dtype)


def _moe_grouped(xs, tile_expert, num_tiles, w1, w3, w2, tg, tf=512):
    p = xs.shape[1]
    nf = w1.shape[2] // tf
    tn = SC_ROW_WORDS
    nb = SC_PIECES
    fcl = lambda j: jnp.minimum(j, nf - 1)
    ncl = lambda j: jnp.maximum(j - nf, 0)

    def w2_map(col0, switch):
        def index(i, j, te, nt):
            early = jnp.logical_or(j < switch, i >= nt[0])
            expert = jnp.where(early, te[jnp.maximum(i - 1, 0)], te[i])
            return (expert, 0, jnp.where(early, col0 + nb - 1, col0 + ncl(j)))
        return index

    def w13_map(i, j, te, nt):
        return (te[i], 0, jnp.where(i < nt[0], fcl(j), nf - 1))

    grid_spec = pltpu.PrefetchScalarGridSpec(
        num_scalar_prefetch=2,
        grid=(p // tg, nf + nb),
        in_specs=[pl.BlockSpec((SC_PIECES, tg, SC_ROW_WORDS), lambda i, j, te, nt: (0, i, 0)),
                  pl.BlockSpec((None, D_MODEL, tf), w13_map),
                  pl.BlockSpec((None, D_MODEL, tf), w13_map),
                  pl.BlockSpec((None, nf * tf, tn), w2_map(0, nf // 2)),
                  pl.BlockSpec((None, nf * tf, tn), w2_map(nb, nf // 2 + 2))],
        out_specs=pl.BlockSpec((None, tg, tn), lambda i, j, te, nt: (ncl(j), i, 0)),
        scratch_shapes=[pltpu.VMEM((tg, D_MODEL), BF16), pltpu.VMEM((nf, tg, tf), BF16)])
    return pl.pallas_call(
        functools.partial(_moe_kernel, nf=nf, tf=tf),
        grid_spec=grid_spec,
        out_shape=jax.ShapeDtypeStruct((SC_PIECES, p, SC_ROW_WORDS), jnp.int32),
        compiler_params=_cparams(("arbitrary", "arbitrary")),
        name="moe_grouped",
    )(tile_expert, num_tiles, xs, w1, w3, w2, w2)


def _combine_kernel(x_ref, gate_ref, meta_ref, y1_ref, y2_ref, o_ref):
    meta = meta_ref[...]
    moe = (meta[:, 2:3] * _unpack_halves(_load_pieces(y1_ref))
           + meta[:, 3:4] * _unpack_halves(_load_pieces(y2_ref)))
    o_ref[...] = x_ref[...] + gate_ref[...] * moe


def _combine(x2, mod, meta, yg, seq, tm=512):
    t = x2.shape[0]
    nt = t // tm
    row = lambda i: (i, 0)
    return pl.pallas_call(
        _combine_kernel,
        grid=(nt,),
        in_specs=[pl.BlockSpec((tm, D_MODEL), row), _mod_spec(5, tm, seq),
                  pl.BlockSpec((tm, LANES), row),
                  pl.BlockSpec((SC_PIECES, tm, SC_ROW_WORDS), lambda i: (0, i, 0)),
                  pl.BlockSpec((SC_PIECES, tm, SC_ROW_WORDS), lambda i: (0, nt + i, 0))],
        out_specs=pl.BlockSpec((tm, D_MODEL), row),
        out_shape=jax.ShapeDtypeStruct((t, D_MODEL), F32),
        compiler_params=_cparams(("arbitrary",)),
        name="moe_combine",
    )(x2, mod, meta, yg, yg)


def _sc_gather(table, idx):
    n = idx.shape[0]
    nrows = table.shape[1]
    pieces = n * SC_PIECES
    idx_pieces = jnp.concatenate([idx + s * nrows for s in range(SC_PIECES)]).reshape(1, pieces)
    mesh = plsc.VectorSubcoreMesh(core_axis_name="core", subcore_axis_name="subcore")

    @pl.kernel(out_type=jax.ShapeDtypeStruct((pieces, SC_ROW_WORDS), table.dtype), mesh=mesh, scratch_types=[])
    def gather_kernel(table_hbm, idx_hbm, out_hbm):
        def body(idx_vmem, out_vmem):
            pltpu.sync_copy(table_hbm.at[idx_vmem.at[0]], out_vmem)

        pltpu.emit_pipeline(
            body,
            grid=(pieces // SC_WINDOW,),
            in_specs=[pl.BlockSpec((1, SC_WINDOW), lambda i: (0, i))],
            out_specs=[pl.BlockSpec((SC_WINDOW, SC_ROW_WORDS), lambda i: (i, 0))],
            core_axis_name=("core", "subcore"),
            dimension_semantics=(pltpu.PARALLEL,),
        )(idx_hbm, out_hbm)

    out = gather_kernel(table.reshape(SC_PIECES * nrows, SC_ROW_WORDS), idx_pieces)
    return out.reshape(SC_PIECES, n, SC_ROW_WORDS)


def _sc_invert(pos, nrows):
    n = pos.shape[0]
    src = jnp.broadcast_to(jnp.arange(n, dtype=jnp.int32)[:, None], (n, LANES))
    mesh = plsc.VectorSubcoreMesh(core_axis_name="core", subcore_axis_name="subcore")

    @pl.kernel(out_type=jax.ShapeDtypeStruct((nrows, LANES), jnp.int32), mesh=mesh, scratch_types=[])
    def scatter_kernel(src_hbm, idx_hbm, out_hbm):
        def body(src_vmem, idx_vmem):
            pltpu.sync_copy(src_vmem, out_hbm.at[idx_vmem.at[0]])

        pltpu.emit_pipeline(
            body,
            grid=(n // SC_WINDOW,),
            in_specs=[pl.BlockSpec((SC_WINDOW, LANES), lambda i: (i, 0)),
                      pl.BlockSpec((1, SC_WINDOW), lambda i: (0, i))],
            out_specs=[],
            core_axis_name=("core", "subcore"),
            dimension_semantics=(pltpu.PARALLEL,),
        )(src_hbm, idx_hbm)

    return scatter_kernel(src, pos.reshape(1, n))[:, 0]


def _moe(x2, mod, norm_g, mix, wo, rw_pad, rb_pad, w1, w3, w2, seq, tg=1024):
    t = x2.shape[0]
    x2, h, meta = _router(x2, mod, norm_g, mix, wo, rw_pad, rb_pad, seq)
    e_flat = meta[:, 0:2].astype(jnp.int32).reshape(-1)
    onehot = (e_flat[:, None] == jnp.arange(N_EXPERTS)[None, :]).astype(jnp.int32)
    csum = jnp.cumsum(onehot, axis=0)
    counts = csum[-1]
    rank = jnp.take_along_axis(csum, e_flat[:, None], axis=1)[:, 0] - 1
    padded = ((counts + tg - 1) // tg) * tg
    pend = jnp.cumsum(padded)
    pos = (pend - padded)[e_flat] + rank
    p_rows = 2 * t + N_EXPERTS * tg
    row_token = jnp.clip(_sc_invert(pos, p_rows) // 2, 0, t - 1)
    tile_start = jnp.arange(p_rows // tg, dtype=jnp.int32) * tg
    tile_expert = jnp.minimum(jnp.sum((tile_start[:, None] >= pend[None, :]).astype(jnp.int32), axis=1),
                              N_EXPERTS - 1)
    num_tiles = (pend[-1] // tg).astype(jnp.int32).reshape(1)
    xs = _sc_gather(h, row_token)
    ys = _moe_grouped(xs, tile_expert, num_tiles, w1, w3, w2, tg)
    yg = _sc_gather(ys, jnp.concatenate([pos[0::2], pos[1::2]]))
    return _combine(x2, mod, meta, yg, seq)


def _rope_tables(seq):
    t = np.arange(seq)
    lane = np.arange(LANES)
    d = lane % HEAD_DIM
    pos = np.where((d // 32)[None, :] == 0, (t // GRID_W)[:, None], (t % GRID_W)[:, None]).astype(np.float32)
    inv = (ROPE_THETA ** (-np.arange(16, dtype=np.float32) / 16)).astype(np.float32)
    ang = pos * inv[(d % 16)][None, :]
    return jnp.asarray(np.cos(ang), F32), jnp.asarray(np.sin(ang), F32)


def _pair_gain(g):
    return jnp.concatenate([g, g]).astype(F32)


def _prepare(p, seq):
    even, odd = {}, {}
    w_in = p["w_in_even"][0]
    qa, ka, va, qb, kb, vb = jnp.split(w_in, [512, 640, 768, 1280, 1792], axis=1)
    dup = lambda w: jnp.concatenate([w[:, 0:64], w[:, 0:64], w[:, 64:128], w[:, 64:128]], axis=1)
    even["w_in"] = jnp.concatenate([qa, qb, dup(ka), dup(va), kb, vb], axis=1).astype(BF16)
    gains = jnp.zeros((8, LANES), F32)
    gains = gains.at[0].set(_pair_gain(p["qnorm_a"][0])).at[1].set(_pair_gain(p["knorm_a"][0]))
    gains = gains.at[2].set(_pair_gain(p["qnorm_b"][0])).at[3].set(_pair_gain(p["knorm_b"][0]))
    even["gains"] = gains
    qscale = SCALE * LOG2E
    even["groups"] = ((512, 0, True, qscale), (512, 2, False, qscale), (256, 1, True, 1.0),
                      (256, None, False, 1.0), (512, 3, False, 1.0), (512, None, False, 1.0))
    lam = jnp.zeros((8, LANES), F32)
    for r, name in enumerate(("lam_q1", "lam_k1", "lam_q2", "lam_k2")):
        lam = lam.at[r, 0:HEAD_DIM].set(p[name][0])
    even["lam"] = lam
    even["subg"] = p["subln_b"][0].reshape(1, LANES).astype(F32)
    wo = p["w_out_even"][0].astype(BF16)
    even["wo"] = (wo[0:512], wo[512:1024])
    even["slopes"] = jnp.asarray(LOG2E * 2.0 ** (-8.0 * (np.arange(B_HEADS) + 1.0) / B_HEADS), F32)
    for name in ("ffn_w1", "ffn_w3", "ffn_w2"):
        even[name] = p[name][0].astype(BF16)
    for name in ("ada_w", "ada_b", "norm_mix", "norm_ffn"):
        even[name] = p[name + "_even"][0]
        odd[name] = p[name + "_odd"][0]

    odd["w_qkv"] = p["w_qkv_odd"][0].astype(BF16)
    gains = jnp.zeros((8, LANES), F32)
    odd["gains"] = gains.at[0].set(_pair_gain(p["qnorm_c"][0])).at[1].set(_pair_gain(p["knorm_c"][0]))
    odd["groups"] = ((1024, 0, False, qscale), (1024, 1, False, 1.0), (1024, None, False, 1.0))
    odd["bias_tab"] = _natten_bias_table(p["rpb_c"][0], seq)
    odd["wo"] = (p["w_out_odd"][0].astype(BF16),)
    odd["rw"] = jnp.zeros((D_MODEL, LANES), F32).at[:, 0:N_EXPERTS].set(p["router_w"][0])
    odd["rb"] = jnp.zeros((1, LANES), F32).at[0, 0:N_EXPERTS].set(p["router_b"][0])
    for name in ("moe_w1", "moe_w3", "moe_w2"):
        odd[name] = p[name][0].astype(BF16)
    cos_t, sn_t = _rope_tables(seq)
    gmat = jnp.asarray(np.kron(np.eye(4), np.full((HEAD_DIM, HEAD_DIM), 1.0 / HEAD_DIM)), BF16)
    return even, odd, (cos_t, sn_t, gmat)


def _even_layer(x2, c, ev, shared, nbatch, seq):
    cos_t, sn_t, gmat = shared
    mod = _ada_modulation(c, ev["ada_w"], ev["ada_b"])
    qa, qb, ka, va, kb, vb = _projection(x2, mod, ev["norm_mix"], ev["w_in"], cos_t, sn_t, ev["gains"], gmat,
                                         ev["groups"], seq)
    shp = lambda a: a.reshape(nbatch, seq, a.shape[1])
    lam_init = 0.8 - 0.6 * math.exp(-0.3 * 0)
    mix_a = _flash_attention(shp(qa), shp(ka), shp(va), ev["slopes"], ev["lam"], ev["subg"], nbatch=nbatch, seq=seq,
                             ngroups=A_KV_HEADS, nstack=4, alibi=False, lam_init=lam_init)
    mix_b = _flash_attention(shp(qb), shp(kb), shp(vb), ev["slopes"], ev["lam"], ev["subg"], nbatch=nbatch, seq=seq,
                             ngroups=B_HEADS, nstack=2, alibi=True, lam_init=lam_init)
    t = nbatch * seq
    return _ffn(x2, mod, ev["norm_ffn"], (mix_a.reshape(t, -1), mix_b.reshape(t, -1)), ev["wo"],
                ev["ffn_w1"], ev["ffn_w3"], ev["ffn_w2"], seq)


def _odd_layer(x2, c, od, shared, nbatch, seq):
    cos_t, sn_t, gmat = shared
    mod = _ada_modulation(c, od["ada_w"], od["ada_b"])
    q, k, v = _projection(x2, mod, od["norm_mix"], od["w_qkv"], cos_t, sn_t, od["gains"], gmat, od["groups"], seq)
    shp = lambda a: a.reshape(nbatch, seq, a.shape[1])
    mix = _natten(shp(q), shp(k), shp(v), od["bias_tab"], nbatch=nbatch, seq=seq)
    return _moe(x2, mod, od["norm_ffn"], mix.reshape(nbatch * seq, -1), od["wo"][0], od["rw"], od["rb"],
                od["moe_w1"], od["moe_w3"], od["moe_w2"], seq)


def _trunk(x, c, ev, od, shared):
    nbatch, seq, d = x.shape
    x2 = x.reshape(nbatch * seq, d)
    x2 = _even_layer(x2, c, ev, shared, nbatch, seq)
    x2 = _odd_layer(x2, c, od, shared, nbatch, seq)
    return x2.reshape(nbatch, seq, d)


def kernel(x_prompt, x_sample, c_prompt, c_sample, ada_w_even, ada_b_even, norm_mix_even, norm_ffn_even, w_in_even, qnorm_a, knorm_a, qnorm_b, knorm_b, lam_q1, lam_k1, lam_q2, lam_k2, subln_b, w_out_even, ffn_w1, ffn_w3, ffn_w2, ada_w_odd, ada_b_odd, norm_mix_odd, norm_ffn_odd, w_qkv_odd, qnorm_c, knorm_c, rpb_c, w_out_odd, router_w, router_b, moe_w1, moe_w3, moe_w2):
    params = dict(ada_w_even=ada_w_even, ada_b_even=ada_b_even, norm_mix_even=norm_mix_even,
                  norm_ffn_even=norm_ffn_even, w_in_even=w_in_even, qnorm_a=qnorm_a, knorm_a=knorm_a,
                  qnorm_b=qnorm_b, knorm_b=knorm_b, lam_q1=lam_q1, lam_k1=lam_k1, lam_q2=lam_q2, lam_k2=lam_k2,
                  subln_b=subln_b, w_out_even=w_out_even, ffn_w1=ffn_w1, ffn_w3=ffn_w3, ffn_w2=ffn_w2,
                  ada_w_odd=ada_w_odd, ada_b_odd=ada_b_odd, norm_mix_odd=norm_mix_odd, norm_ffn_odd=norm_ffn_odd,
                  w_qkv_odd=w_qkv_odd, qnorm_c=qnorm_c, knorm_c=knorm_c, rpb_c=rpb_c, w_out_odd=w_out_odd,
                  router_w=router_w, router_b=router_b, moe_w1=moe_w1, moe_w3=moe_w3, moe_w2=moe_w2)
    seq = x_prompt.shape[1]
    ev, od, shared = _prepare(params, seq)
    y_prompt = _trunk(x_prompt, c_prompt, ev, od, shared)
    y_sample = _trunk(x_sample, c_sample, ev, od, shared)
    return (y_prompt, y_sample)
```

```python
import functools
import math

import numpy as np
import jax
import jax.numpy as jnp
from jax import lax
from jax.experimental import pallas as pl
from jax.experimental.pallas import tpu as pltpu
from jax.experimental.pallas import tpu_sc as plsc

F32 = jnp.float32
BF16 = jnp.bfloat16

D_MODEL = 1024
HEAD_DIM = 64
LANES = 128
SCALE = HEAD_DIM ** -0.5
LOG2E = 1.4426950408889634
GRID_W = 64
EPS = 1e-6
ROPE_THETA = 10000.0
A_Q_HEADS = 8
A_KV_HEADS = 2
B_HEADS = 4
C_HEADS = 16
WIN_H = 8
WIN_W = 16
N_EXPERTS = 8
D_FF = 2816
D_FF_EXPERT = 3584
VMEM_LIMIT = 56 * 1024 * 1024

NAT_QROWS = 4
NAT_KROWS = 12
NAT_TQ = NAT_QROWS * GRID_W
NAT_TK = NAT_KROWS * GRID_W
PACK_W = D_MODEL // 2
SC_WINDOW = 128
SC_ROW_WORDS = 256
SC_PIECES = PACK_W // SC_ROW_WORDS
FFN_CHUNK = 1408


def _cparams(sem):
    return pltpu.CompilerParams(dimension_semantics=sem, vmem_limit_bytes=VMEM_LIMIT)


def _norm_mod(x, g, shift, scale):
    ms = jnp.mean(x * x, axis=-1, keepdims=True)
    y = x * lax.rsqrt(ms + EPS) * g
    return y * (1.0 + scale) + shift


def _head_norm(x, gain, gmat):
    ms = jnp.dot((x * x).astype(BF16), gmat, preferred_element_type=F32)
    return x * lax.rsqrt(ms + EPS) * gain


def _rope(x, cos, sn, first_quarter):
    up = pltpu.roll(x, x.shape[1] - 16, 1)
    down = pltpu.roll(x, 16, 1)
    return x * cos + sn * jnp.where(first_quarter, -up, down)


def _sigmoid(a):
    return 1.0 / (1.0 + jnp.exp(-a))


def _pack_halves(a, b):
    hi = lax.bitcast_convert_type(a.astype(BF16).astype(F32), jnp.int32)
    lo = lax.bitcast_convert_type(b.astype(BF16).astype(F32), jnp.int32)
    return hi | lax.shift_right_logical(lo, jnp.full_like(lo, 16))


def _unpack_halves(w):
    hi = lax.bitcast_convert_type(w & jnp.int32(-65536), F32)
    lo = lax.bitcast_convert_type(lax.shift_left(w, jnp.full_like(w, 16)), F32)
    return jnp.concatenate([hi, lo], axis=1)


def _load_pieces(ref):
    return jnp.concatenate([ref[s] for s in range(SC_PIECES)], axis=1)


def _ada_kernel(c_ref, w_ref, b_ref, o_ref):
    c = c_ref[...]
    s = c * _sigmoid(c)
    o_ref[...] = jnp.dot(s, w_ref[...], preferred_element_type=F32,
                         precision=lax.Precision.HIGHEST) + b_ref[...]


def _ada_modulation(c, w, b):
    nb, d = c.shape
    n = w.shape[1]
    tn = 512
    mod = pl.pallas_call(
        _ada_kernel,
        grid=(n // tn,),
        in_specs=[pl.BlockSpec((nb, d), lambda j: (0, 0)),
                  pl.BlockSpec((d, tn), lambda j: (0, j)),
                  pl.BlockSpec((1, tn), lambda j: (0, j))],
        out_specs=pl.BlockSpec((nb, tn), lambda j: (0, j)),
        out_shape=jax.ShapeDtypeStruct((nb, n), F32),
        compiler_params=_cparams(("arbitrary",)),
        name="ada_mod",
    )(c, w, b.reshape(1, n))
    return mod.reshape(nb, 6, 1, d)


def _mod_spec(k, tm, seq):
    return pl.BlockSpec((None, None, 1, D_MODEL), lambda i, *_: ((i * tm) // seq, k, 0, 0))


def _proj_kernel(x_ref, sh_ref, sc_ref, g_ref, w_ref, cos_ref, sn_ref, gains_ref, gmat_ref, *o_refs,
                 groups):
    wide = 2 * LANES
    lane = lax.broadcasted_iota(jnp.int32, (1, wide), 1)
    first_quarter = (lane % 32) < 16
    gmat = gmat_ref[...]
    h = _norm_mod(x_ref[...], g_ref[...], sh_ref[...], sc_ref[...])
    y = jnp.dot(h.astype(BF16), w_ref[...], preferred_element_type=F32)
    cos = jnp.concatenate([cos_ref[...]] * 2, axis=1)
    sn = jnp.concatenate([sn_ref[...]] * 2, axis=1)
    off = 0
    for o_ref, (width, gain_row, rope, mult) in zip(o_refs, groups):
        if gain_row is None:
            o_ref[...] = y[:, off:off + width].astype(o_ref.dtype)
        else:
            gain = jnp.concatenate([gains_ref[gain_row:gain_row + 1, :]] * 2, axis=1)
            for t in range(width // wide):
                z = _head_norm(y[:, off + t * wide: off + (t + 1) * wide], gain, gmat)
                if rope:
                    z = _rope(z, cos, sn, first_quarter)
                if mult != 1.0:
                    z = z * mult
                o_ref[:, t * wide:(t + 1) * wide] = z.astype(o_ref.dtype)
        off += width


def _projection(x2, mod, norm_g, w, cos_t, sn_t, gains, gmat, groups, seq, tm=512):
    t = x2.shape[0]
    n = w.shape[1]
    nseq = seq // tm
    row = lambda i: (i, 0)
    const = lambda i: (0, 0)
    tab = lambda i: (i % nseq, 0)
    return pl.pallas_call(
        functools.partial(_proj_kernel, groups=groups),
        grid=(t // tm,),
        in_specs=[pl.BlockSpec((tm, D_MODEL), row),
                  _mod_spec(0, tm, seq), _mod_spec(1, tm, seq),
                  pl.BlockSpec((1, D_MODEL), const),
                  pl.BlockSpec((D_MODEL, n), const),
                  pl.BlockSpec((tm, LANES), tab), pl.BlockSpec((tm, LANES), tab),
                  pl.BlockSpec(gains.shape, const),
                  pl.BlockSpec(gmat.shape, const)],
        out_specs=[pl.BlockSpec((tm, g[0]), row) for g in groups],
        out_shape=[jax.ShapeDtypeStruct((t, g[0]), BF16) for g in groups],
        compiler_params=_cparams(("arbitrary",)),
        name="norm_mod_proj",
    )(x2, mod, mod, norm_g.reshape(1, D_MODEL), w, cos_t, sn_t, gains, gmat)


def _flash_kernel(slope_ref, q_ref, k_ref, v_ref, lam_ref, subg_ref, o_ref,
                  q_sc, v_sc, m_sc, acc_sc, s0_sc, s1_sc, p0_sc, p1_sc, a0_sc, a1_sc,
                  *, tq, tk, seq, nstack, alibi, lam_init):
    g = pl.program_id(1)
    lane = lax.broadcasted_iota(jnp.int32, (1, LANES), 1)
    low_half = lane < HEAD_DIM
    s_bufs, p_bufs, a_bufs = (s0_sc, s1_sc), (p0_sc, p1_sc), (a0_sc, a1_sc)
    nchunks = seq // tk
    ntiles = seq // tq
    nrows = nstack * tq

    v_sc[:, 0:LANES] = v_ref[...]
    v_sc[:, LANES:2 * LANES] = jnp.ones((seq, LANES), BF16)

    if alibi:
        rc = (lax.broadcasted_iota(jnp.int32, (tq, tk), 0)
              - lax.broadcasted_iota(jnp.int32, (tq, tk), 1)).astype(F32)
        slope = jnp.full((1, 1), slope_ref[g], F32)
        rc = slope * rc
        lp = lam_ref[...]
        l1 = jnp.sum(lp[0:1, :] * lp[1:2, :], axis=-1, keepdims=True)
        l2 = jnp.sum(lp[2:3, :] * lp[3:4, :], axis=-1, keepdims=True)
        lam = jnp.exp(l1) - jnp.exp(l2) + lam_init

    def tile_rows(t):
        return pl.ds(t * tq, tq) if isinstance(t, int) else pl.ds(pl.multiple_of(t * tq, tq), tq)

    def chunk_rows(c):
        return pl.ds(c * tk, tk) if isinstance(c, int) else pl.ds(pl.multiple_of(c * tk, tk), tk)

    def load_queries(t, slot):
        for u in range(nstack):
            src = q_ref[tile_rows(t), (u // 2) * LANES:(u // 2 + 1) * LANES]
            keep = low_half if u % 2 == 0 else jnp.logical_not(low_half)
            q_sc[slot, u * tq:(u + 1) * tq, :] = jnp.where(keep, src, jnp.zeros_like(src))

    def scores(t, c, slot, par):
        s = lax.dot_general(q_sc[slot], k_ref[chunk_rows(c), :], (((1,), (1,)), ((), ())),
                            preferred_element_type=F32)
        if alibi:
            base = (t * tq - c * tk).astype(F32) if not (isinstance(t, int) and isinstance(c, int)) \
                else float(t * tq - c * tk)
            dist = jnp.abs(rc + slope * base)
            s = s - jnp.concatenate([dist] * nstack, axis=0)
        s_bufs[par][...] = s

    def softmax(slot, par, first):
        s = s_bufs[par][...]
        m_cur = jnp.max(s, axis=-1, keepdims=True)
        if first:
            m_new = jnp.broadcast_to(m_cur, (nrows, LANES))
        else:
            m_old = m_sc[slot]
            m_new = jnp.maximum(m_old, m_cur)
            a_bufs[par][...] = jnp.exp2(m_old - m_new)
        p_bufs[par][...] = jnp.exp2(s - pltpu.repeat(m_new, tk // LANES, 1)).astype(BF16)
        m_sc[slot] = m_new

    def values(c, slot, par, first):
        d = jnp.dot(p_bufs[par][...], v_sc[chunk_rows(c), :], preferred_element_type=F32)
        if first:
            acc_sc[slot] = d
        else:
            acc_sc[slot] = pltpu.repeat(a_bufs[par][...], 2, 1) * acc_sc[slot] + d

    def finalize(t, slot):
        acc = acc_sc[slot]
        o = acc[:, 0:LANES] * (1.0 / acc[:, LANES:2 * LANES])
        if alibi:
            ob = o[0:tq, :] - lam * o[tq:2 * tq, :]
            ms = jnp.mean(ob * ob, axis=-1, keepdims=True)
            ob = ob * lax.rsqrt(ms + EPS) * subg_ref[...] * (1.0 - lam_init)
            o_ref[tile_rows(t), :] = ob.astype(o_ref.dtype)
        else:
            for pair in range(nstack // 2):
                lo = o[(2 * pair) * tq:(2 * pair + 1) * tq, :]
                hi = o[(2 * pair + 1) * tq:(2 * pair + 2) * tq, :]
                o_ref[tile_rows(t), pair * LANES:(pair + 1) * LANES] = (
                    jnp.where(low_half, lo, hi).astype(o_ref.dtype))

    def step(t, t_next, slot, c):
        static = isinstance(c, int)
        par = c % 2 if static else None
        ahead2 = c + 2
        if static and ahead2 >= nchunks:
            scores(t_next, ahead2 - nchunks, 1 - slot, par)
        else:
            scores(t, ahead2, slot, par)
        if static and c + 1 >= nchunks:
            softmax(1 - slot, 1 - par, first=True)
        else:
            softmax(slot, 1 - par, first=False)
        values(c, slot, par, first=static and c == 0)

    load_queries(0, 0)
    scores(0, 0, 0, 0)
    scores(0, 1, 0, 1)
    softmax(0, 0, first=True)

    def one_tile(t, slot):
        t_next = (t + 1) % ntiles
        load_queries(t_next, 1 - slot)
        lead = min(2, nchunks - 2)
        for c in range(lead):
            step(t, t_next, slot, c)

        def pair_body(j, inner):
            for par in range(2):
                c = 2 * j + par
                scores(t, c + 2, slot, par)
                softmax(slot, 1 - par, first=False)
                values(c, slot, par, first=False)
            return inner

        lax.fori_loop(lead // 2, (nchunks - 2) // 2, pair_body, 0)
        for c in range(nchunks - 2, nchunks):
            step(t, t_next, slot, c)
        finalize(t, slot)

    def tile_pair_body(u, carry):
        one_tile(2 * u, 0)
        one_tile(2 * u + 1, 1)
        return carry

    lax.fori_loop(0, ntiles // 2, tile_pair_body, 0)


def _flash_attention(q, k, v, slopes, lam_pack, subg, *, nbatch, seq, ngroups, nstack, alibi, lam_init,
                     nrows=1024):
    qw = (nstack // 2) * LANES
    tq = nrows // nstack
    tk = min(1024, seq // 2)
    kernel = functools.partial(_flash_kernel, tq=tq, tk=tk, seq=seq, nstack=nstack, alibi=alibi,
                               lam_init=lam_init)
    grid_spec = pltpu.PrefetchScalarGridSpec(
        num_scalar_prefetch=1,
        grid=(nbatch, ngroups),
        in_specs=[pl.BlockSpec((None, seq, qw), lambda b, g, s: (b, 0, g)),
                  pl.BlockSpec((None, seq, LANES), lambda b, g, s: (b, 0, g)),
                  pl.BlockSpec((None, seq, LANES), lambda b, g, s: (b, 0, g)),
                  pl.BlockSpec(lam_pack.shape, lambda b, g, s: (0, 0)),
                  pl.BlockSpec(subg.shape, lambda b, g, s: (0, 0))],
        out_specs=pl.BlockSpec((None, seq, qw), lambda b, g, s: (b, 0, g)),
        scratch_shapes=[pltpu.VMEM((2, nrows, LANES), BF16),
                        pltpu.VMEM((seq, 2 * LANES), BF16),
                        pltpu.VMEM((2, nrows, LANES), F32),
                        pltpu.VMEM((2, nrows, 2 * LANES), F32),
                        pltpu.VMEM((nrows, tk), F32), pltpu.VMEM((nrows, tk), F32),
                        pltpu.VMEM((nrows, tk), BF16), pltpu.VMEM((nrows, tk), BF16),
                        pltpu.VMEM((nrows, LANES), F32), pltpu.VMEM((nrows, LANES), F32)])
    return pl.pallas_call(
        kernel,
        grid_spec=grid_spec,
        out_shape=jax.ShapeDtypeStruct((nbatch, seq, ngroups * qw), BF16),
        compiler_params=_cparams(("arbitrary", "arbitrary")),
        name="flash_alibi" if alibi else "flash_gqa",
    )(slopes, q, k, v, lam_pack, subg)


def _natten_kernel(q_ref, k_ref, v_ref, bias_ref, o_ref, v_sc, s0_sc, s1_sc, *, seq):
    lane = lax.broadcasted_iota(jnp.int32, (1, LANES), 1)
    low_half = lane < HEAD_DIM
    ntiles = seq // NAT_TQ
    rows = seq // GRID_W
    s_bufs = (s0_sc, s1_sc)

    v_sc[:, 0:LANES] = v_ref[...]
    v_sc[:, LANES:2 * LANES] = jnp.ones((seq, LANES), BF16)

    def window(t):
        w0 = jnp.clip(t * NAT_QROWS - WIN_H // 2, 0, rows - NAT_KROWS)
        return pl.multiple_of(w0 * GRID_W, NAT_TQ)

    def scores(t, par):
        q = q_ref[pl.ds(pl.multiple_of(t * NAT_TQ, NAT_TQ), NAT_TQ), :]
        zero = jnp.zeros_like(q)
        q2 = jnp.concatenate([jnp.where(low_half, q, zero), jnp.where(low_half, zero, q)], axis=0)
        s = lax.dot_general(q2, k_ref[pl.ds(window(t), NAT_TK), :], (((1,), (1,)), ((), ())),
                            preferred_element_type=F32)
        cls = jnp.where(t == 0, 0, jnp.where(t == ntiles - 1, 2, 1))
        bias = bias_ref[cls].astype(F32).reshape(2 * NAT_TQ, NAT_TK)
        s_bufs[par][...] = s + bias

    def finish(t, par):
        s = s_bufs[par][...]
        m = jnp.max(s, axis=-1, keepdims=True)
        p = jnp.exp2(s - m).astype(BF16)
        acc = jnp.dot(p, v_sc[pl.ds(window(t), NAT_TK), :], preferred_element_type=F32)
        o = acc[:, 0:LANES] * (1.0 / acc[:, LANES:2 * LANES])
        out = jnp.where(low_half, o[0:NAT_TQ, :], o[NAT_TQ:, :])
        o_ref[pl.ds(pl.multiple_of(t * NAT_TQ, NAT_TQ), NAT_TQ), :] = out.astype(o_ref.dtype)

    scores(jnp.int32(0), 0)

    unroll = next(u for u in (8, 4, 2) if ntiles % u == 0)

    def group_body(u, carry):
        for k in range(unroll):
            t = unroll * u + k
            scores((t + 1) % ntiles, (k + 1) % 2)
            finish(t, k % 2)
        return carry

    lax.fori_loop(0, ntiles // unroll, group_body, 0)


def _natten(q, k, v, bias_tab, *, nbatch, seq):
    npairs = C_HEADS // 2
    blk = pl.BlockSpec((None, seq, LANES), lambda p, b: (b, 0, p))
    return pl.pallas_call(
        functools.partial(_natten_kernel, seq=seq),
        grid=(npairs, nbatch),
        in_specs=[blk, blk, blk,
                  pl.BlockSpec((3, None, 2, NAT_TQ, NAT_TK), lambda p, b: (0, p, 0, 0, 0))],
        out_specs=blk,
        out_shape=jax.ShapeDtypeStruct((nbatch, seq, C_HEADS * HEAD_DIM), BF16),
        scratch_shapes=[pltpu.VMEM((seq, 2 * LANES), BF16),
                        pltpu.VMEM((2 * NAT_TQ, NAT_TK), F32), pltpu.VMEM((2 * NAT_TQ, NAT_TK), F32)],
        compiler_params=_cparams(("arbitrary", "arbitrary")),
        name="natten",
    )(q, k, v, bias_tab)


def _natten_bias_table(rpb, seq):
    rows = seq // GRID_W
    ntiles = rows // NAT_QROWS
    col = jnp.arange(GRID_W)
    cstart = jnp.clip(col - WIN_W // 2, 0, GRID_W - WIN_W)
    col_valid = (col[None, :] >= cstart[:, None]) & (col[None, :] < cstart[:, None] + WIN_W)
    dc_idx = jnp.clip(col[None, :] - col[:, None] + WIN_W - 1, 0, 2 * WIN_W - 2)
    rpb_cols = rpb[:, :, dc_idx]
    tabs = []
    for tile in (0, 1, ntiles - 1):
        r = tile * NAT_QROWS + jnp.arange(NAT_QROWS)
        w0 = int(np.clip(tile * NAT_QROWS - WIN_H // 2, 0, rows - NAT_KROWS))
        kr = w0 + jnp.arange(NAT_KROWS)
        rstart = jnp.clip(r - WIN_H // 2, 0, rows - WIN_H)
        row_valid = (kr[None, :] >= rstart[:, None]) & (kr[None, :] < rstart[:, None] + WIN_H)
        dr_idx = jnp.clip(kr[None, :] - r[:, None] + WIN_H - 1, 0, 2 * WIN_H - 2)
        pick = (dr_idx[:, :, None] == jnp.arange(2 * WIN_H - 1)[None, None, :]).astype(F32)
        b = jnp.einsum("qkd,hdcx->hqckx", pick, rpb_cols * LOG2E,
                       precision=lax.Precision.HIGHEST)
        valid = row_valid[:, None, :, None] & col_valid[None, :, None, :]
        b = jnp.where(valid[None], b, -jnp.inf)
        tabs.append(b.reshape(C_HEADS, NAT_TQ, NAT_TK))
    return jnp.stack(tabs).reshape(3, C_HEADS // 2, 2, NAT_TQ, NAT_TK).astype(BF16)


def _ffn_kernel(x_ref, g1_ref, sh_ref, sc_ref, g2_ref, ng_ref, ma_ref, mb_ref, woa_ref, wob_ref,
                w1_ref, w3_ref, w2_ref, o_ref):
    y = (jnp.dot(ma_ref[...], woa_ref[...], preferred_element_type=F32)
         + jnp.dot(mb_ref[...], wob_ref[...], preferred_element_type=F32))
    x1 = x_ref[...] + g1_ref[...] * y
    h = _norm_mod(x1, ng_ref[...], sh_ref[...], sc_ref[...]).astype(BF16)
    y = None
    for f in range(D_FF // FFN_CHUNK):
        cols = slice(f * FFN_CHUNK, (f + 1) * FFN_CHUNK)
        a = jnp.dot(h, w1_ref[:, cols], preferred_element_type=F32)
        b = jnp.dot(h, w3_ref[:, cols], preferred_element_type=F32)
        d = jnp.dot((a * _sigmoid(a) * b).astype(BF16), w2_ref[cols, :], preferred_element_type=F32)
        y = d if y is None else y + d
    o_ref[...] = x1 + g2_ref[...] * y


def _ffn(x2, mod, norm_g, mixes, wos, w1, w3, w2, seq, tm=512):
    t = x2.shape[0]
    row = lambda i: (i, 0)
    resident = lambda a: pl.BlockSpec(a.shape, lambda i: (0, 0), pipeline_mode=pl.Buffered(1))
    return pl.pallas_call(
        _ffn_kernel,
        grid=(t // tm,),
        in_specs=[pl.BlockSpec((tm, D_MODEL), row),
                  _mod_spec(2, tm, seq), _mod_spec(3, tm, seq), _mod_spec(4, tm, seq), _mod_spec(5, tm, seq),
                  pl.BlockSpec((1, D_MODEL), lambda i: (0, 0)),
                  pl.BlockSpec((tm, mixes[0].shape[1]), row), pl.BlockSpec((tm, mixes[1].shape[1]), row),
                  resident(wos[0]), resident(wos[1]), resident(w1), resident(w3), resident(w2)],
        out_specs=pl.BlockSpec((tm, D_MODEL), row),
        out_shape=jax.ShapeDtypeStruct((t, D_MODEL), F32),
        compiler_params=_cparams(("arbitrary",)),
        name="outproj_ffn_swiglu",
    )(x2, mod, mod, mod, mod, norm_g.reshape(1, D_MODEL), mixes[0], mixes[1], wos[0], wos[1], w1, w3, w2)


def _router_kernel(x_ref, g1_ref, sh_ref, sc_ref, g_ref, mix_ref, wo_ref, rw_ref, rb_ref,
                   x1_ref, h_ref, meta_ref):
    lane = lax.broadcasted_iota(jnp.int32, (1, LANES), 1).astype(F32)
    nsub = 4
    rb = x_ref.shape[0] // nsub
    for r in range(nsub):
        rows = slice(r * rb, (r + 1) * rb)
        x1 = x_ref[rows, :] + g1_ref[...] * jnp.dot(mix_ref[rows, :], wo_ref[...], preferred_element_type=F32)
        x1_ref[rows, :] = x1
        h = _norm_mod(x1, g_ref[...], sh_ref[...], sc_ref[...])
        packed = _pack_halves(h[:, 0:PACK_W], h[:, PACK_W:D_MODEL])
        for s in range(SC_PIECES):
            h_ref[s, rows, :] = packed[:, s * SC_ROW_WORDS:(s + 1) * SC_ROW_WORDS]
        logits = jnp.dot(h, rw_ref[...], preferred_element_type=F32,
                         precision=lax.Precision.HIGHEST) + rb_ref[...]
        logits = jnp.where(lane < N_EXPERTS, logits, -jnp.inf)
        m1 = jnp.max(logits, axis=-1, keepdims=True)
        i1 = jnp.min(jnp.where(logits == m1, lane, float(LANES)), axis=-1, keepdims=True)
        rest = jnp.where(lane == i1, -jnp.inf, logits)
        m2 = jnp.max(rest, axis=-1, keepdims=True)
        i2 = jnp.min(jnp.where(rest == m2, lane, float(LANES)), axis=-1, keepdims=True)
        e = jnp.exp(m2 - m1)
        g1 = 1.0 / (1.0 + e)
        g2 = e * g1
        meta_ref[rows, :] = jnp.where(lane == 0, i1, jnp.where(lane == 1, i2, jnp.where(
            lane == 2, g1, jnp.where(lane == 3, g2, 0.0))))


def _router(x2, mod, norm_g, mix, wo, rw_pad, rb_pad, seq, tm=512):
    t = x2.shape[0]
    row = lambda i: (i, 0)
    const = lambda i: (0, 0)
    return pl.pallas_call(
        _router_kernel,
        grid=(t // tm,),
        in_specs=[pl.BlockSpec((tm, D_MODEL), row),
                  _mod_spec(2, tm, seq), _mod_spec(3, tm, seq), _mod_spec(4, tm, seq),
                  pl.BlockSpec((1, D_MODEL), const),
                  pl.BlockSpec((tm, D_MODEL), row), pl.BlockSpec((D_MODEL, D_MODEL), const),
                  pl.BlockSpec((D_MODEL, LANES), const), pl.BlockSpec((1, LANES), const)],
        out_specs=[pl.BlockSpec((tm, D_MODEL), row),
                   pl.BlockSpec((SC_PIECES, tm, SC_ROW_WORDS), lambda i: (0, i, 0)),
                   pl.BlockSpec((tm, LANES), row)],
        out_shape=[jax.ShapeDtypeStruct((t, D_MODEL), F32),
                   jax.ShapeDtypeStruct((SC_PIECES, t, SC_ROW_WORDS), jnp.int32),
                   jax.ShapeDtypeStruct((t, LANES), F32)],
        compiler_params=_cparams(("arbitrary",)),
        name="outproj_moe_router",
    )(x2, mod, mod, mod, norm_g.reshape(1, D_MODEL), mix, wo, rw_pad, rb_pad)


def _moe_kernel(te_ref, nt_ref, x_ref, w1_ref, w3_ref, w2a_ref, w2b_ref, o_ref, x_sc, g_sc, *, nf, tf):
    i = pl.program_id(0)
    j = pl.program_id(1)
    active = i < nt_ref[0]

    @pl.when(jnp.logical_and(active, j == 0))
    def _():
        x_sc[...] = _unpack_halves(_load_pieces(x_ref)).astype(BF16)

    @pl.when(jnp.logical_and(active, j < nf))
    def _():
        x = x_sc[...]
        a = jnp.dot(x, w1_ref[...], preferred_element_type=F32)
        b = jnp.dot(x, w3_ref[...], preferred_element_type=F32)
        g_sc[j] = (a * _sigmoid(a) * b).astype(BF16)

    @pl.when(jnp.logical_and(active, j >= nf))
    def _():
        ya, yb = None, None
        for f in range(nf):
            g = g_sc[f]
            da = jnp.dot(g, w2a_ref[f * tf:(f + 1) * tf, :], preferred_element_type=F32)
            db = jnp.dot(g, w2b_ref[f * tf:(f + 1) * tf, :], preferred_element_type=F32)
            ya = da if ya is None else ya + da
            yb = db if yb is None else yb + db
        o_ref[...] = _pack_halves(ya, yb)

    @pl.when(jnp.logical_and(jnp.logical_not(active), j >= nf))
    def _():
        o_ref[...] = jnp.zeros(o_ref.shape, o_ref.dtype)


def _moe_grouped(xs, tile_expert, num_tiles, w1, w3, w2, tg, tf=512):
    p = xs.shape[1]
    nf = w1.shape[2] // tf
    tn = SC_ROW_WORDS
    nb = SC_PIECES
    fcl = lambda j: jnp.minimum(j, nf - 1)
    ncl = lambda j: jnp.maximum(j - nf, 0)

    def w2_map(col0, switch):
        def index(i, j, te, nt):
            early = jnp.logical_or(j < switch, i >= nt[0])
            expert = jnp.where(early, te[jnp.maximum(i - 1, 0)], te[i])
            return (expert, 0, jnp.where(early, col0 + nb - 1, col0 + ncl(j)))
        return index

    def w13_map(i, j, te, nt):
        return (te[i], 0, jnp.where(i < nt[0], fcl(j), nf - 1))

    grid_spec = pltpu.PrefetchScalarGridSpec(
        num_scalar_prefetch=2,
        grid=(p // tg, nf + nb),
        in_specs=[pl.BlockSpec((SC_PIECES, tg, SC_ROW_WORDS), lambda i, j, te, nt: (0, i, 0)),
                  pl.BlockSpec((None, D_MODEL, tf), w13_map),
                  pl.BlockSpec((None, D_MODEL, tf), w13_map),
                  pl.BlockSpec((None, nf * tf, tn), w2_map(0, nf // 2)),
                  pl.BlockSpec((None, nf * tf, tn), w2_map(nb, nf // 2 + 2))],
        out_specs=pl.BlockSpec((None, tg, tn), lambda i, j, te, nt: (ncl(j), i, 0)),
        scratch_shapes=[pltpu.VMEM((tg, D_MODEL), BF16), pltpu.VMEM((nf, tg, tf), BF16)])
    return pl.pallas_call(
        functools.partial(_moe_kernel, nf=nf, tf=tf),
        grid_spec=grid_spec,
        out_shape=jax.ShapeDtypeStruct((SC_PIECES, p, SC_ROW_WORDS), jnp.int32),
        compiler_params=_cparams(("arbitrary", "arbitrary")),
        name="moe_grouped",
    )(tile_expert, num_tiles, xs, w1, w3, w2, w2)


def _combine_kernel(x_ref, gate_ref, meta_ref, y1_ref, y2_ref, o_ref):
    meta = meta_ref[...]
    moe = (meta[:, 2:3] * _unpack_halves(_load_pieces(y1_ref))
           + meta[:, 3:4] * _unpack_halves(_load_pieces(y2_ref)))
    o_ref[...] = x_ref[...] + gate_ref[...] * moe


def _combine(x2, mod, meta, yg, seq, tm=512):
    t = x2.shape[0]
    nt = t // tm
    row = lambda i: (i, 0)
    return pl.pallas_call(
        _combine_kernel,
        grid=(nt,),
        in_specs=[pl.BlockSpec((tm, D_MODEL), row), _mod_spec(5, tm, seq),
                  pl.BlockSpec((tm, LANES), row),
                  pl.BlockSpec((SC_PIECES, tm, SC_ROW_WORDS), lambda i: (0, i, 0)),
                  pl.BlockSpec((SC_PIECES, tm, SC_ROW_WORDS), lambda i: (0, nt + i, 0))],
        out_specs=pl.BlockSpec((tm, D_MODEL), row),
        out_shape=jax.ShapeDtypeStruct((t, D_MODEL), F32),
        compiler_params=_cparams(("arbitrary",)),
        name="moe_combine",
    )(x2, mod, meta, yg, yg)


def _sc_gather(table, idx):
    n = idx.shape[0]
    nrows = table.shape[1]
    pieces = n * SC_PIECES
    idx_pieces = jnp.concatenate([idx + s * nrows for s in range(SC_PIECES)]).reshape(1, pieces)
    mesh = plsc.VectorSubcoreMesh(core_axis_name="core", subcore_axis_name="subcore")

    @pl.kernel(out_type=jax.ShapeDtypeStruct((pieces, SC_ROW_WORDS), table.dtype), mesh=mesh, scratch_types=[])
    def gather_kernel(table_hbm, idx_hbm, out_hbm):
        def body(idx_vmem, out_vmem):
            pltpu.sync_copy(table_hbm.at[idx_vmem.at[0]], out_vmem)

        pltpu.emit_pipeline(
            body,
            grid=(pieces // SC_WINDOW,),
            in_specs=[pl.BlockSpec((1, SC_WINDOW), lambda i: (0, i))],
            out_specs=[pl.BlockSpec((SC_WINDOW, SC_ROW_WORDS), lambda i: (i, 0))],
            core_axis_name=("core", "subcore"),
            dimension_semantics=(pltpu.PARALLEL,),
        )(idx_hbm, out_hbm)

    out = gather_kernel(table.reshape(SC_PIECES * nrows, SC_ROW_WORDS), idx_pieces)
    return out.reshape(SC_PIECES, n, SC_ROW_WORDS)


def _sc_invert(pos, nrows):
    n = pos.shape[0]
    src = jnp.broadcast_to(jnp.arange(n, dtype=jnp.int32)[:, None], (n, LANES))
    mesh = plsc.VectorSubcoreMesh(core_axis_name="core", subcore_axis_name="subcore")

    @pl.kernel(out_type=jax.ShapeDtypeStruct((nrows, LANES), jnp.int32), mesh=mesh, scratch_types=[])
    def scatter_kernel(src_hbm, idx_hbm, out_hbm):
        def body(src_vmem, idx_vmem):
            pltpu.sync_copy(src_vmem, out_hbm.at[idx_vmem.at[0]])

        pltpu.emit_pipeline(
            body,
            grid=(n // SC_WINDOW,),
            in_specs=[pl.BlockSpec((SC_WINDOW, LANES), lambda i: (i, 0)),
                      pl.BlockSpec((1, SC_WINDOW), lambda i: (0, i))],
            out_specs=[],
            core_axis_name=("core", "subcore"),
            dimension_semantics=(pltpu.PARALLEL,),
        )(src_hbm, idx_hbm)

    return scatter_kernel(src, pos.reshape(1, n))[:, 0]


def _moe(x2, mod, norm_g, mix, wo, rw_pad, rb_pad, w1, w3, w2, seq, tg=1024):
    t = x2.shape[0]
    x2, h, meta = _router(x2, mod, norm_g, mix, wo, rw_pad, rb_pad, seq)
    e_flat = meta[:, 0:2].astype(jnp.int32).reshape(-1)
    onehot = (e_flat[:, None] == jnp.arange(N_EXPERTS)[None, :]).astype(jnp.int32)
    csum = jnp.cumsum(onehot, axis=0)
    counts = csum[-1]
    rank = jnp.take_along_axis(csum, e_flat[:, None], axis=1)[:, 0] - 1
    padded = ((counts + tg - 1) // tg) * tg
    pend = jnp.cumsum(padded)
    pos = (pend - padded)[e_flat] + rank
    p_rows = 2 * t + N_EXPERTS * tg
    row_token = jnp.clip(_sc_invert(pos, p_rows) // 2, 0, t - 1)
    tile_start = jnp.arange(p_rows // tg, dtype=jnp.int32) * tg
    tile_expert = jnp.minimum(jnp.sum((tile_start[:, None] >= pend[None, :]).astype(jnp.int32), axis=1),
                              N_EXPERTS - 1)
    num_tiles = (pend[-1] // tg).astype(jnp.int32).reshape(1)
    xs = _sc_gather(h, row_token)
    ys = _moe_grouped(xs, tile_expert, num_tiles, w1, w3, w2, tg)
    yg = _sc_gather(ys, jnp.concatenate([pos[0::2], pos[1::2]]))
    return _combine(x2, mod, meta, yg, seq)


def _rope_tables(seq):
    t = np.arange(seq)
    lane = np.arange(LANES)
    d = lane % HEAD_DIM
    pos = np.where((d // 32)[None, :] == 0, (t // GRID_W)[:, None], (t % GRID_W)[:, None]).astype(np.float32)
    inv = (ROPE_THETA ** (-np.arange(16, dtype=np.float32) / 16)).astype(np.float32)
    ang = pos * inv[(d % 16)][None, :]
    return jnp.asarray(np.cos(ang), F32), jnp.asarray(np.sin(ang), F32)


def _pair_gain(g):
    return jnp.concatenate([g, g]).astype(F32)


def _prepare(p, seq):
    even, odd = {}, {}
    w_in = p["w_in_even"][0]
    qa, ka, va, qb, kb, vb = jnp.split(w_in, [512, 640, 768, 1280, 1792], axis=1)
    dup = lambda w: jnp.concatenate([w[:, 0:64], w[:, 0:64], w[:, 64:128], w[:, 64:128]], axis=1)
    even["w_in"] = jnp.concatenate([qa, qb, dup(ka), dup(va), kb, vb], axis=1).astype(BF16)
    gains = jnp.zeros((8, LANES), F32)
    gains = gains.at[0].set(_pair_gain(p["qnorm_a"][0])).at[1].set(_pair_gain(p["knorm_a"][0]))
    gains = gains.at[2].set(_pair_gain(p["qnorm_b"][0])).at[3].set(_pair_gain(p["knorm_b"][0]))
    even["gains"] = gains
    qscale = SCALE * LOG2E
    even["groups"] = ((512, 0, True, qscale), (512, 2, False, qscale), (256, 1, True, 1.0),
                      (256, None, False, 1.0), (512, 3, False, 1.0), (512, None, False, 1.0))
    lam = jnp.zeros((8, LANES), F32)
    for r, name in enumerate(("lam_q1", "lam_k1", "lam_q2", "lam_k2")):
        lam = lam.at[r, 0:HEAD_DIM].set(p[name][0])
    even["lam"] = lam
    even["subg"] = p["subln_b"][0].reshape(1, LANES).astype(F32)
    wo = p["w_out_even"][0].astype(BF16)
    even["wo"] = (wo[0:512], wo[512:1024])
    even["slopes"] = jnp.asarray(LOG2E * 2.0 ** (-8.0 * (np.arange(B_HEADS) + 1.0) / B_HEADS), F32)
    for name in ("ffn_w1", "ffn_w3", "ffn_w2"):
        even[name] = p[name][0].astype(BF16)
    for name in ("ada_w", "ada_b", "norm_mix", "norm_ffn"):
        even[name] = p[name + "_even"][0]
        odd[name] = p[name + "_odd"][0]

    odd["w_qkv"] = p["w_qkv_odd"][0].astype(BF16)
    gains = jnp.zeros((8, LANES), F32)
    odd["gains"] = gains.at[0].set(_pair_gain(p["qnorm_c"][0])).at[1].set(_pair_gain(p["knorm_c"][0]))
    odd["groups"] = ((1024, 0, False, qscale), (1024, 1, False, 1.0), (1024, None, False, 1.0))
    odd["bias_tab"] = _natten_bias_table(p["rpb_c"][0], seq)
    odd["wo"] = (p["w_out_odd"][0].astype(BF16),)
    odd["rw"] = jnp.zeros((D_MODEL, LANES), F32).at[:, 0:N_EXPERTS].set(p["router_w"][0])
    odd["rb"] = jnp.zeros((1, LANES), F32).at[0, 0:N_EXPERTS].set(p["router_b"][0])
    for name in ("moe_w1", "moe_w3", "moe_w2"):
        odd[name] = p[name][0].astype(BF16)
    cos_t, sn_t = _rope_tables(seq)
    gmat = jnp.asarray(np.kron(np.eye(4), np.full((HEAD_DIM, HEAD_DIM), 1.0 / HEAD_DIM)), BF16)
    return even, odd, (cos_t, sn_t, gmat)


def _even_layer(x2, c, ev, shared, nbatch, seq):
    cos_t, sn_t, gmat = shared
    mod = _ada_modulation(c, ev["ada_w"], ev["ada_b"])
    qa, qb, ka, va, kb, vb = _projection(x2, mod, ev["norm_mix"], ev["w_in"], cos_t, sn_t, ev["gains"], gmat,
                                         ev["groups"], seq)
    shp = lambda a: a.reshape(nbatch, seq, a.shape[1])
    lam_init = 0.8 - 0.6 * math.exp(-0.3 * 0)
    mix_a = _flash_attention(shp(qa), shp(ka), shp(va), ev["slopes"], ev["lam"], ev["subg"], nbatch=nbatch, seq=seq,
                             ngroups=A_KV_HEADS, nstack=4, alibi=False, lam_init=lam_init)
    mix_b = _flash_attention(shp(qb), shp(kb), shp(vb), ev["slopes"], ev["lam"], ev["subg"], nbatch=nbatch, seq=seq,
                             ngroups=B_HEADS, nstack=2, alibi=True, lam_init=lam_init)
    t = nbatch * seq
    return _ffn(x2, mod, ev["norm_ffn"], (mix_a.reshape(t, -1), mix_b.reshape(t, -1)), ev["wo"],
                ev["ffn_w1"], ev["ffn_w3"], ev["ffn_w2"], seq)


def _odd_layer(x2, c, od, shared, nbatch, seq):
    cos_t, sn_t, gmat = shared
    mod = _ada_modulation(c, od["ada_w"], od["ada_b"])
    q, k, v = _projection(x2, mod, od["norm_mix"], od["w_qkv"], cos_t, sn_t, od["gains"], gmat, od["groups"], seq)
    shp = lambda a: a.reshape(nbatch, seq, a.shape[1])
    mix = _natten(shp(q), shp(k), shp(v), od["bias_tab"], nbatch=nbatch, seq=seq)
    return _moe(x2, mod, od["norm_ffn"], mix.reshape(nbatch * seq, -1), od["wo"][0], od["rw"], od["rb"],
                od["moe_w1"], od["moe_w3"], od["moe_w2"], seq)


def _trunk(x, c, ev, od, shared):
    nbatch, seq, d = x.shape
    x2 = x.reshape(nbatch * seq, d)
    x2 = _even_layer(x2, c, ev, shared, nbatch, seq)
    x2 = _odd_layer(x2, c, od, shared, nbatch, seq)
    return x2.reshape(nbatch, seq, d)


def kernel(x_prompt, x_sample, c_prompt, c_sample, ada_w_even, ada_b_even, norm_mix_even, norm_ffn_even, w_in_even, qnorm_a, knorm_a, qnorm_b, knorm_b, lam_q1, lam_k1, lam_q2, lam_k2, subln_b, w_out_even, ffn_w1, ffn_w3, ffn_w2, ada_w_odd, ada_b_odd, norm_mix_odd, norm_ffn_odd, w_qkv_odd, qnorm_c, knorm_c, rpb_c, w_out_odd, router_w, router_b, moe_w1, moe_w3, moe_w2):
    params = dict(ada_w_even=ada_w_even, ada_b_even=ada_b_even, norm_mix_even=norm_mix_even,
                  norm_ffn_even=norm_ffn_even, w_in_even=w_in_even, qnorm_a=qnorm_a, knorm_a=knorm_a,
                  qnorm_b=qnorm_b, knorm_b=knorm_b, lam_q1=lam_q1, lam_k1=lam_k1, lam_q2=lam_q2, lam_k2=lam_k2,
                  subln_b=subln_b, w_out_even=w_out_even, ffn_w1=ffn_w1, ffn_w3=ffn_w3, ffn_w2=ffn_w2,
                  ada_w_odd=ada_w_odd, ada_b_odd=ada_b_odd, norm_mix_odd=norm_mix_odd, norm_ffn_odd=norm_ffn_odd,
                  w_qkv_odd=w_qkv_odd, qnorm_c=qnorm_c, knorm_c=knorm_c, rpb_c=rpb_c, w_out_odd=w_out_odd,
                  router_w=router_w, router_b=router_b, moe_w1=moe_w1, moe_w3=moe_w3, moe_w2=moe_w2)
    seq = x_prompt.shape[1]
    ev, od, shared = _prepare(params, seq)
    y_prompt = _trunk(x_prompt, c_prompt, ev, od, shared)
    y_sample = _trunk(x_sample, c_sample, ev, od, shared)
    return (y_prompt, y_sample)
```
